```python
import jax
import jax.numpy as jnp
from jax import lax
import numpy as np

D_MODEL = 1024
BATCH = 4
SEQ = 8192
DEPTH = 1

GRID_W = 64
CTX_LEN = 256
HEAD_DIM = 64
MIX_WIDTH = D_MODEL
FOURIER_HEADS = MIX_WIDTH // (4 * HEAD_DIM)
FOURIER_WIDTH = FOURIER_HEADS * HEAD_DIM
ATTN_HEADS = (MIX_WIDTH - FOURIER_WIDTH) // HEAD_DIM
KV_HEADS = 4
Q_PER_KV = ATTN_HEADS // KV_HEADS
ATTN_WIDTH = ATTN_HEADS * HEAD_DIM
KV_WIDTH = KV_HEADS * HEAD_DIM
KV_COL0 = FOURIER_WIDTH + ATTN_WIDTH
IN_WIDTH = KV_COL0 + 2 * KV_WIDTH
Q_BLOCK = 128
ATTN_SCALE = HEAD_DIM ** -0.5
ROPE_THETA = 10000.0
AXIS_ROT = HEAD_DIM // 2
N_EXPERTS = 256
TOP_K = 8
N_EXPERT_GROUPS = 8
TOPK_GROUPS = 4
EXPERTS_PER_GROUP = N_EXPERTS // N_EXPERT_GROUPS
EXPERT_FF = D_MODEL // 4
SHARED_FF = D_MODEL // 4
ROUTED_SCALE = 2.5
EXPERT_BLOCK = 128
RMS_EPS = 1e-6

kernel_name = 'hybrid_fourier_gqa_moe_dit_block'


def rmsnorm(x, g):
    xf = x.astype(jnp.float32)
    y = xf * lax.rsqrt(jnp.mean(xf * xf, axis=-1, keepdims=True) + RMS_EPS)
    return (y * g.astype(jnp.float32)).astype(x.dtype)


def modulate(x, g, shift, scale):
    return rmsnorm(x, g) * (1 + scale) + shift


def adaln(cond, w_ada, b_ada):
    return jnp.split(jax.nn.silu(cond) @ w_ada + b_ada, 6, axis=-1)


def heads(t, n_heads):
    return t.reshape(t.shape[0], t.shape[1], n_heads, HEAD_DIM)


def axial_rope_tables(n_tokens):
    rows = n_tokens // GRID_W
    row = jnp.broadcast_to(jnp.arange(rows, dtype=jnp.float32)[:, None], (rows, GRID_W)).reshape(-1)
    col = jnp.broadcast_to(jnp.arange(GRID_W, dtype=jnp.float32)[None, :], (rows, GRID_W)).reshape(-1)
    n_freq = AXIS_ROT // 2
    inv_freq = ROPE_THETA ** (-jnp.arange(n_freq, dtype=jnp.float32) / n_freq)
    ang_r = row[:, None] * inv_freq
    ang_c = col[:, None] * inv_freq
    ang = jnp.concatenate([ang_r, ang_r, ang_c, ang_c], axis=-1)
    return jnp.cos(ang), jnp.sin(ang)


def _rotate_half(z):
    a, b = jnp.split(z, 2, axis=-1)
    return jnp.concatenate([-b, a], axis=-1)


def apply_axial_rope(t, cos, sin):
    tf = t.astype(jnp.float32)
    t_row, t_col = jnp.split(tf, 2, axis=-1)
    rot = jnp.concatenate([_rotate_half(t_row), _rotate_half(t_col)], axis=-1)
    return (tf * cos[:, None, :] + rot * sin[:, None, :]).astype(t.dtype)


def fourier_mixer(u, w_fourier):
    b, l = u.shape[:2]
    uh = heads(u, FOURIER_HEADS).transpose(0, 2, 1, 3).astype(jnp.float32)
    spec = jnp.fft.fft2(uh, norm='ortho').real.astype(u.dtype)
    return jnp.einsum('bhld,hde->blhe', spec, w_fourier).reshape(b, l, FOURIER_WIDTH)


def gqa_attend(q, k, v):
    b, lq = q.shape[:2]
    qg = q.reshape(b, lq, KV_HEADS, Q_PER_KV, HEAD_DIM)
    s = jnp.einsum('bqkgd,bskd->bkgqs', qg, k, preferred_element_type=jnp.float32) * ATTN_SCALE
    p = jax.nn.softmax(s, axis=-1).astype(v.dtype)
    o = jnp.einsum('bkgqs,bskd->bqkgd', p, v)
    return o.reshape(b, lq, ATTN_WIDTH)


def blocked_attention(q, k, v):
    b, s = q.shape[:2]
    nb = s // Q_BLOCK
    qb = jnp.moveaxis(q.reshape(b, nb, Q_BLOCK, ATTN_HEADS, HEAD_DIM), 1, 0)
    o = lax.map(lambda qblk: gqa_attend(qblk, k, v), qb)
    return jnp.moveaxis(o, 0, 1).reshape(b, s, ATTN_WIDTH)


def swiglu(t, w_gu, w_down):
    a, g = jnp.split(t @ w_gu, 2, axis=-1)
    return (jax.nn.silu(a) * g) @ w_down


def moe_ffn(h, w_router, router_bias, w_expert_gu, w_expert_down, w_shared_gu, w_shared_down):
    n = h.shape[0]
    scores = jax.nn.sigmoid(jnp.einsum('nd,de->ne', h, w_router, preferred_element_type=jnp.float32))
    sel = scores + router_bias.astype(jnp.float32)
    group_score = lax.top_k(sel.reshape(n, N_EXPERT_GROUPS, EXPERTS_PER_GROUP), 2)[0].sum(-1)
    _, top_groups = lax.top_k(group_score, TOPK_GROUPS)
    group_keep = jnp.any(top_groups[:, :, None] == jnp.arange(N_EXPERT_GROUPS)[None, None, :], axis=1)
    sel = jnp.where(jnp.repeat(group_keep, EXPERTS_PER_GROUP, axis=1), sel, -jnp.inf)
    _, top_e = lax.top_k(sel, TOP_K)
    gate = jnp.take_along_axis(scores, top_e, axis=1)
    gate = gate / jnp.sum(gate, axis=-1, keepdims=True) * ROUTED_SCALE

    n_assign = n * TOP_K
    flat_e = top_e.reshape(n_assign)
    order = jnp.argsort(flat_e)
    sorted_e = flat_e[order]
    sorted_tok = (order // TOP_K).astype(jnp.int32)
    sorted_gate = gate.reshape(n_assign)[order].astype(h.dtype)
    counts = jnp.bincount(flat_e, length=N_EXPERTS)
    padded = (counts + EXPERT_BLOCK - 1) // EXPERT_BLOCK * EXPERT_BLOCK
    pad_end = jnp.cumsum(padded)
    pad_start = pad_end - padded
    start = jnp.cumsum(counts) - counts
    dest = pad_start[sorted_e] + (jnp.arange(n_assign) - start[sorted_e])
    n_blocks = -(-n_assign // EXPERT_BLOCK) + N_EXPERTS
    n_rows = n_blocks * EXPERT_BLOCK
    row_tok = jnp.zeros((n_rows,), jnp.int32).at[dest].set(sorted_tok)
    row_gate = jnp.zeros((n_rows,), h.dtype).at[dest].set(sorted_gate)
    block_start = jnp.arange(n_blocks) * EXPERT_BLOCK
    block_e = jnp.minimum(jnp.searchsorted(pad_end, block_start, side='right'), N_EXPERTS - 1)

    def block_step(y, blk):
        tok, g, e = blk
        out = swiglu(h[tok], w_expert_gu[e], w_expert_down[e])
        return y.at[tok].add(out * g[:, None]), None

    y, _ = lax.scan(block_step, jnp.zeros_like(h),
                    (row_tok.reshape(n_blocks, EXPERT_BLOCK), row_gate.reshape(n_blocks, EXPERT_BLOCK), block_e))
    return y + swiglu(h, w_shared_gu, w_shared_down)


def moe_tokens(t, moe_w):
    b, l, d = t.shape
    return moe_ffn(t.reshape(b * l, d), *moe_w).reshape(b, l, d)


def trunk_layer(x, ctx, c, c_ctx, cos, sin, lw, moe_w, update_ctx):
    norm1_g, w_ada, b_ada, w_in, w_fourier, q_norm_g, k_norm_g, w_out, norm2_g = lw
    b, l = x.shape[:2]
    sh1, sc1, g1, sh2, sc2, g2 = [m[:, None, :] for m in adaln(c, w_ada, b_ada)]
    csh1, csc1, cg1, csh2, csc2, cg2 = adaln(c_ctx, w_ada, b_ada)

    hc = modulate(ctx, norm1_g, csh1, csc1)
    kv_c = hc @ w_in[:, KV_COL0:]
    k_c = rmsnorm(heads(kv_c[..., :KV_WIDTH], KV_HEADS), k_norm_g)
    v_c = heads(kv_c[..., KV_WIDTH:], KV_HEADS)

    h = modulate(x, norm1_g, sh1, sc1)
    z = h @ w_in
    u = z[..., :FOURIER_WIDTH]
    q = apply_axial_rope(rmsnorm(heads(z[..., FOURIER_WIDTH:KV_COL0], ATTN_HEADS), q_norm_g), cos, sin)
    k = apply_axial_rope(rmsnorm(heads(z[..., KV_COL0:KV_COL0 + KV_WIDTH], KV_HEADS), k_norm_g), cos, sin)
    v = heads(z[..., KV_COL0 + KV_WIDTH:], KV_HEADS)
    k_all = jnp.concatenate([k, k_c], axis=1)
    v_all = jnp.concatenate([v, v_c], axis=1)
    mix = jnp.concatenate([fourier_mixer(u, w_fourier), blocked_attention(q, k_all, v_all)], axis=-1)
    x_new = x + g1 * (mix @ w_out)
    x_new = x_new + g2 * moe_tokens(modulate(x_new, norm2_g, sh2, sc2), moe_w)

    if update_ctx:
        zc = hc @ w_in[:, :KV_COL0]
        q_c = rmsnorm(heads(zc[..., FOURIER_WIDTH:], ATTN_HEADS), q_norm_g)
        mix_c = jnp.concatenate([fourier_mixer(zc[..., :FOURIER_WIDTH], w_fourier), gqa_attend(q_c, k_c, v_c)], axis=-1)
        ctx = ctx + cg1 * (mix_c @ w_out)
        ctx = ctx + cg2 * moe_tokens(modulate(ctx, norm2_g, csh2, csc2), moe_w)
    return x_new, ctx


def setup_inputs(seed: int = 0) -> dict:
    key = jax.random.key(seed)
    ks = jax.random.split(key, 20)
    f32 = jnp.float32

    def nrm(k, shape, scale):
        return jax.random.normal(k, shape, f32) * scale

    def gain(k, shape):
        return 1.0 + 0.02 * jax.random.normal(k, shape, f32)

    return {
        'x': nrm(ks[0], (BATCH, SEQ, D_MODEL), 1.0),
        'c': nrm(ks[1], (BATCH, D_MODEL), 1.0),
        'ctx': nrm(ks[2], (BATCH, CTX_LEN, D_MODEL), 1.0),
        'c_ctx': nrm(ks[3], (D_MODEL,), 1.0),
        'norm1_g': gain(ks[4], (DEPTH, D_MODEL)),
        'w_ada': nrm(ks[5], (DEPTH, D_MODEL, 6 * D_MODEL), 0.5 * D_MODEL ** -0.5),
        'b_ada': nrm(ks[6], (DEPTH, 6 * D_MODEL), 0.02),
        'w_in': nrm(ks[7], (DEPTH, D_MODEL, IN_WIDTH), D_MODEL ** -0.5),
        'w_fourier': nrm(ks[8], (DEPTH, FOURIER_HEADS, HEAD_DIM, HEAD_DIM), HEAD_DIM ** -0.5),
        'q_norm_g': gain(ks[9], (DEPTH, HEAD_DIM)),
        'k_norm_g': gain(ks[10], (DEPTH, HEAD_DIM)),
        'w_out': nrm(ks[11], (DEPTH, MIX_WIDTH, D_MODEL), MIX_WIDTH ** -0.5),
        'norm2_g': gain(ks[12], (DEPTH, D_MODEL)),
        'w_router': nrm(ks[13], (DEPTH, D_MODEL, N_EXPERTS), D_MODEL ** -0.5),
        'router_bias': nrm(ks[14], (DEPTH, N_EXPERTS), 0.01),
        'w_expert_gu': nrm(ks[15], (DEPTH, N_EXPERTS, D_MODEL, 2 * EXPERT_FF), D_MODEL ** -0.5),
        'w_expert_down': nrm(ks[16], (DEPTH, N_EXPERTS, EXPERT_FF, D_MODEL), EXPERT_FF ** -0.5),
        'w_shared_gu': nrm(ks[17], (DEPTH, D_MODEL, 2 * SHARED_FF), D_MODEL ** -0.5),
        'w_shared_down': nrm(ks[18], (DEPTH, SHARED_FF, D_MODEL), SHARED_FF ** -0.5),
        'final_norm_g': gain(ks[19], (D_MODEL,)),
    }


def reference(x, c, ctx, c_ctx, norm1_g, w_ada, b_ada, w_in, w_fourier, q_norm_g, k_norm_g, w_out,
              norm2_g, w_router, router_bias, w_expert_gu, w_expert_down, w_shared_gu, w_shared_down,
              final_norm_g):
    cos, sin = axial_rope_tables(x.shape[1])
    for i in range(DEPTH):
        lw = (norm1_g[i], w_ada[i], b_ada[i], w_in[i], w_fourier[i], q_norm_g[i], k_norm_g[i], w_out[i], norm2_g[i])
        moe_w = (w_router[i], router_bias[i], w_expert_gu[i], w_expert_down[i], w_shared_gu[i], w_shared_down[i])
        x, ctx = trunk_layer(x, ctx, c, c_ctx, cos, sin, lw, moe_w, update_ctx=i + 1 < DEPTH)
    return rmsnorm(x, final_norm_g)
```

```python
import functools

import numpy as np
import jax
import jax.numpy as jnp
from jax import lax
from jax.experimental import pallas as pl
from jax.experimental.pallas import tpu as pltpu

F32 = jnp.float32
BF16 = jnp.bfloat16
I32 = jnp.int32

D_MODEL = 1024
GRID_W = 64
HEAD_DIM = 64
FOURIER_HEADS = 4
FOURIER_WIDTH = 256
ATTN_HEADS = 12
KV_HEADS = 4
Q_PER_KV = 3
ATTN_WIDTH = 768
KV_WIDTH = 256
KV_COL0 = 1024
IN_WIDTH = 1536
ATTN_SCALE = HEAD_DIM ** -0.5
ROPE_THETA = 10000.0
AXIS_ROT = HEAD_DIM // 2
N_EXPERTS = 256
TOP_K = 8
N_EXPERT_GROUPS = 8
TOPK_GROUPS = 4
EXPERTS_PER_GROUP = 32
EXPERT_FF = 256
ROUTED_SCALE = 2.5
RMS_EPS = 1e-6

LANES = 128
HALF_D = D_MODEL // 2
DFT_L1 = 64
VMEM_LIMIT = 48 * 1024 * 1024

TOK_TILE = 512
ATTN_BQ = 128
ATTN_BK = 512
ROW_BLOCK = 256
COMB_TILE = 256

_HI = lax.Precision.HIGHEST
_NT = (((1,), (1,)), ((), ()))


def _cp(sem, vmem=VMEM_LIMIT):
    return pltpu.CompilerParams(dimension_semantics=sem, vmem_limit_bytes=vmem)


def _sigmoid(v):
    return 1.0 / (1.0 + jnp.exp(-v))


def _pack_rows(v):
    half = v.shape[1] // 2
    lo = lax.bitcast_convert_type(v[:, :half].astype(BF16).astype(F32), I32)
    hi = lax.bitcast_convert_type(v[:, half:].astype(BF16).astype(F32), I32)
    return lax.shift_right_logical(lo, 16) | (hi & jnp.int32(-65536))


def _unpack_rows(w):
    lo = lax.bitcast_convert_type(lax.shift_left(w, 16), F32)
    hi = lax.bitcast_convert_type(w & jnp.int32(-65536), F32)
    return lo, hi


def _adaln_kernel(c_ref, w_ref, b_ref, o_ref):
    c = c_ref[...]
    s = c * _sigmoid(c)
    o_ref[...] = jnp.dot(s, w_ref[...], precision=_HI, preferred_element_type=F32) + b_ref[...]


def _adaln(cond, w_ada, b_ada):
    rows, d = cond.shape
    n = w_ada.shape[1]
    tn = 1536
    return pl.pallas_call(
        _adaln_kernel,
        grid=(n // tn,),
        in_specs=[pl.BlockSpec((rows, d), lambda j: (0, 0)),
                  pl.BlockSpec((d, tn), lambda j: (0, j)),
                  pl.BlockSpec((1, tn), lambda j: (0, j))],
        out_specs=pl.BlockSpec((rows, tn), lambda j: (0, j)),
        out_shape=jax.ShapeDtypeStruct((rows, n), F32),
        compiler_params=_cp(("arbitrary",)),
        name="adaln",
    )(cond, w_ada, b_ada.reshape(1, n))


def _modulated_norm(x, g, sc, sh):
    ms = jnp.mean(x * x, axis=-1, keepdims=True)
    return x * lax.rsqrt(ms + RMS_EPS) * g * (1.0 + sc) + sh


def _head_norm(chunk, gain, ones_bd):
    sq = chunk * chunk
    hi = sq.astype(BF16)
    lo = (sq - hi.astype(F32)).astype(BF16)
    ss = (jnp.dot(hi, ones_bd, preferred_element_type=F32)
          + jnp.dot(lo, ones_bd, preferred_element_type=F32))
    return chunk * lax.rsqrt(ss * (1.0 / HEAD_DIM) + RMS_EPS) * gain


def _rope(t, cos, sin_a, sin_b):
    return (t * cos + pltpu.roll(t, LANES - AXIS_ROT // 2, 1) * sin_a
            + pltpu.roll(t, AXIS_ROT // 2, 1) * sin_b)


def _inproj_kernel(*refs, rope, with_uq):
    if rope:
        (x_ref, sc_ref, sh_ref, g_ref, w_ref, gq_ref, gk_ref, ones_ref,
         cos_ref, sa_ref, sb_ref) = refs[:11]
        outs = refs[11:]
    else:
        x_ref, sc_ref, sh_ref, g_ref, w_ref, gq_ref, gk_ref, ones_ref = refs[:8]
        outs = refs[8:]
    if with_uq:
        u_ref, q_ref, kt_ref, v_ref = outs
    else:
        kt_ref, v_ref = outs

    h = _modulated_norm(x_ref[0], g_ref[...], sc_ref[0], sh_ref[0])
    z = jnp.dot(h.astype(BF16), w_ref[...], preferred_element_type=F32)
    ones_bd = ones_ref[...]

    def normed(chunk, gain):
        t = _head_norm(chunk, gain, ones_bd)
        if rope:
            t = _rope(t, cos_ref[...], sa_ref[...], sb_ref[...])
        return t

    col = 0
    if with_uq:
        u_ref[0] = z[:, :FOURIER_WIDTH].astype(BF16)
        col = FOURIER_WIDTH
        for j in range(ATTN_WIDTH // LANES):
            t = normed(z[:, col + j * LANES: col + (j + 1) * LANES], gq_ref[...])
            q_ref[0, 2 * j] = t[:, :HEAD_DIM].astype(BF16)
            q_ref[0, 2 * j + 1] = t[:, HEAD_DIM:].astype(BF16)
        col += ATTN_WIDTH
    ks = [normed(z[:, col + j * LANES: col + (j + 1) * LANES], gk_ref[...])
          for j in range(KV_WIDTH // LANES)]
    kt_ref[0] = jnp.concatenate(ks, axis=1).T.astype(BF16)
    col += KV_WIDTH
    for g in range(KV_HEADS):
        v_ref[0, g] = z[:, col + g * HEAD_DIM: col + (g + 1) * HEAD_DIM].astype(BF16)


def _inproj(x, sc, sh, g, w, gq, gk, ones_bd, rope_tabs, with_uq):
    b, t, d = x.shape
    tm = min(TOK_TILE, t)
    wcols = w.shape[1]
    bm = sc.shape[0]
    mod_map = (lambda i, j: (i, 0, 0)) if bm == b else (lambda i, j: (0, 0, 0))
    const2 = lambda i, j: (0, 0)
    in_specs = [pl.BlockSpec((1, tm, d), lambda i, j: (i, j, 0)),
                pl.BlockSpec((1, 1, d), mod_map),
                pl.BlockSpec((1, 1, d), mod_map),
                pl.BlockSpec((1, d), const2),
                pl.BlockSpec((d, wcols), const2),
                pl.BlockSpec((1, LANES), const2),
                pl.BlockSpec((1, LANES), const2),
                pl.BlockSpec((LANES, LANES), const2)]
    args = [x, sc, sh, g, w, gq, gk, ones_bd]
    rope = rope_tabs is not None
    if rope:
        in_specs += [pl.BlockSpec((tm, LANES), lambda i, j: (j, 0))] * 3
        args += list(rope_tabs)
    out_specs, out_shape = [], []
    if with_uq:
        out_specs += [pl.BlockSpec((1, tm, FOURIER_WIDTH), lambda i, j: (i, j, 0)),
                      pl.BlockSpec((1, ATTN_HEADS, tm, HEAD_DIM), lambda i, j: (i, 0, j, 0))]
        out_shape += [jax.ShapeDtypeStruct((b, t, FOURIER_WIDTH), BF16),
                      jax.ShapeDtypeStruct((b, ATTN_HEADS, t, HEAD_DIM), BF16)]
    out_specs += [pl.BlockSpec((1, KV_WIDTH, tm), lambda i, j: (i, 0, j)),
                  pl.BlockSpec((1, KV_HEADS, tm, HEAD_DIM), lambda i, j: (i, 0, j, 0))]
    out_shape += [jax.ShapeDtypeStruct((b, KV_WIDTH, t), BF16),
                  jax.ShapeDtypeStruct((b, KV_HEADS, t, HEAD_DIM), BF16)]
    return pl.pallas_call(
        functools.partial(_inproj_kernel, rope=rope, with_uq=with_uq),
        grid=(b, t // tm),
        in_specs=in_specs,
        out_specs=out_specs,
        out_shape=out_shape,
        compiler_params=_cp(("arbitrary", "arbitrary")),
        name="inproj_latent" if with_uq else "inproj_ctx",
    )(*args)


def _fourier_a_kernel(u_ref, c_ref, s_ref, yr_ref, yi_ref):
    u = u_ref[0]
    yr_ref[0] = jnp.dot(c_ref[...], u, preferred_element_type=F32).astype(BF16)
    yi_ref[0] = jnp.dot(s_ref[...], u, preferred_element_type=F32).astype(BF16)


def _fourier_b_kernel(yr_ref, yi_ref, m_ref, c_ref, s_ref, w_ref, o_ref, *, kb):
    y = jnp.concatenate([yr_ref[0], yi_ref[0]], axis=1)
    x = jnp.einsum("kab,kbc->kac", m_ref[...], y, preferred_element_type=F32)
    xr = x[:, :DFT_L1].reshape(kb * DFT_L1, FOURIER_WIDTH).astype(BF16)
    xi = x[:, DFT_L1:].reshape(kb * DFT_L1, FOURIER_WIDTH).astype(BF16)
    spec = (jnp.dot(xr, c_ref[...], preferred_element_type=F32)
            + jnp.dot(xi, s_ref[...], preferred_element_type=F32))
    o = jnp.dot(spec.astype(BF16), w_ref[...], preferred_element_type=F32)
    for j in range(kb):
        o_ref[0, :, j, :] = o[j * DFT_L1:(j + 1) * DFT_L1]


def _dft_tables(l):
    l2 = l // DFT_L1
    k2 = np.arange(l2)
    ang2 = 2.0 * np.pi * ((k2[:, None] * k2[None, :]) % l2) / l2
    c2 = np.cos(ang2)
    s2n = -np.sin(ang2)
    n1 = np.arange(DFT_L1)
    k = l2 * n1[None, :, None] + k2[:, None, None]
    ang = 2.0 * np.pi * ((k * n1[None, None, :]) % l) / l
    mr, mi = np.cos(ang), -np.sin(ang)
    m = np.concatenate([np.concatenate([mr, -mi], axis=2),
                        np.concatenate([mi, mr], axis=2)], axis=1)
    d = np.arange(HEAD_DIM)
    angc = 2.0 * np.pi * ((d[:, None] * d[None, :]) % HEAD_DIM) / HEAD_DIM
    scale = 1.0 / np.sqrt(float(l) * HEAD_DIM)
    eye = np.eye(FOURIER_HEADS)
    cbd = np.kron(eye, np.cos(angc) * scale)
    sbd = np.kron(eye, np.sin(angc) * scale)
    as_bf = lambda a: jnp.asarray(a, dtype=F32).astype(BF16)
    return as_bf(c2), as_bf(s2n), as_bf(m), as_bf(cbd), as_bf(sbd)


def _fourier(u, w_bd):
    b, l, fw = u.shape
    l2 = l // DFT_L1
    ncol = DFT_L1 * fw
    tn = min(4096, ncol)
    kb = min(8, l2)
    c2, s2n, m, cbd, sbd = _dft_tables(l)
    yr, yi = pl.pallas_call(
        _fourier_a_kernel,
        grid=(b, ncol // tn),
        in_specs=[pl.BlockSpec((1, l2, tn), lambda i, j: (i, 0, j)),
                  pl.BlockSpec((l2, l2), lambda i, j: (0, 0)),
                  pl.BlockSpec((l2, l2), lambda i, j: (0, 0))],
        out_specs=[pl.BlockSpec((1, l2, tn), lambda i, j: (i, 0, j))] * 2,
        out_shape=[jax.ShapeDtypeStruct((b, l2, ncol), BF16)] * 2,
        compiler_params=_cp(("arbitrary", "arbitrary")),
        name="fourier_a",
    )(u.reshape(b, l2, ncol), c2, s2n)
    yr = yr.reshape(b, l2, DFT_L1, fw)
    yi = yi.reshape(b, l2, DFT_L1, fw)
    out = pl.pallas_call(
        functools.partial(_fourier_b_kernel, kb=kb),
        grid=(b, l2 // kb),
        in_specs=[pl.BlockSpec((1, kb, DFT_L1, fw), lambda i, j: (i, j, 0, 0)),
                  pl.BlockSpec((1, kb, DFT_L1, fw), lambda i, j: (i, j, 0, 0)),
                  pl.BlockSpec((kb, 2 * DFT_L1, 2 * DFT_L1), lambda i, j: (j, 0, 0)),
                  pl.BlockSpec((fw, fw), lambda i, j: (0, 0)),
                  pl.BlockSpec((fw, fw), lambda i, j: (0, 0)),
                  pl.BlockSpec((fw, fw), lambda i, j: (0, 0))],
        out_specs=pl.BlockSpec((1, DFT_L1, kb, fw), lambda i, j: (i, 0, j, 0)),
        out_shape=jax.ShapeDtypeStruct((b, DFT_L1, l2, fw), F32),
        compiler_params=_cp(("arbitrary", "arbitrary")),
        name="fourier_b",
    )(yr, yi, m, cbd, sbd, w_bd)
    return out.reshape(b, l, fw)


def _attn_kernel(q_ref, kt_ref, v_ref, ktc_ref, vc_ref, o_ref, m_scr, l_scr, acc_scr, *, bk, nchunks):
    rows = Q_PER_KV * q_ref.shape[2]
    q = q_ref[0].reshape(rows, HEAD_DIM)
    m_scr[...] = jnp.full(m_scr.shape, -jnp.inf, F32)
    l_scr[...] = jnp.zeros(l_scr.shape, F32)
    acc_scr[...] = jnp.zeros(acc_scr.shape, F32)

    def chunk(kt, v):
        s = jnp.dot(q, kt, preferred_element_type=F32)
        m_old = m_scr[...]
        m_new = jnp.maximum(m_old, jnp.max(s, axis=-1, keepdims=True))
        alpha = jnp.exp(m_old - m_new)
        p = jnp.exp(s - m_new)
        l_scr[...] = alpha * l_scr[...] + jnp.sum(p, axis=-1, keepdims=True)
        acc_scr[...] = alpha * acc_scr[...] + jnp.dot(p.astype(BF16), v, preferred_element_type=F32)
        m_scr[...] = m_new

    def body(i, carry):
        off = pl.multiple_of(i * bk, bk)
        chunk(kt_ref[0, :, pl.ds(off, bk)], v_ref[0, 0, pl.ds(off, bk), :])
        return carry

    lax.fori_loop(0, nchunks, body, 0)
    chunk(ktc_ref[0], vc_ref[0, 0])
    o = acc_scr[...] / l_scr[...]
    o_ref[0] = o.reshape(Q_PER_KV, rows // Q_PER_KV, HEAD_DIM).astype(BF16)


def _attention(q, kt, v, ktc, vc):
    b, _, l, _ = q.shape
    c = ktc.shape[2]
    bq = min(ATTN_BQ, l)
    bk = min(ATTN_BK, l)
    rows = Q_PER_KV * bq
    return pl.pallas_call(
        functools.partial(_attn_kernel, bk=bk, nchunks=l // bk),
        grid=(b, KV_HEADS, l // bq),
        in_specs=[pl.BlockSpec((1, Q_PER_KV, bq, HEAD_DIM), lambda i, g, j: (i, g, j, 0)),
                  pl.BlockSpec((1, HEAD_DIM, l), lambda i, g, j: (i, g, 0)),
                  pl.BlockSpec((1, 1, l, HEAD_DIM), lambda i, g, j: (i, g, 0, 0)),
                  pl.BlockSpec((1, HEAD_DIM, c), lambda i, g, j: (i, g, 0)),
                  pl.BlockSpec((1, 1, c, HEAD_DIM), lambda i, g, j: (i, g, 0, 0))],
        out_specs=pl.BlockSpec((1, Q_PER_KV, bq, HEAD_DIM), lambda i, g, j: (i, g, j, 0)),
        out_shape=jax.ShapeDtypeStruct((b, ATTN_HEADS, l, HEAD_DIM), BF16),
        scratch_shapes=[pltpu.VMEM((rows, 1), F32), pltpu.VMEM((rows, 1), F32),
                        pltpu.VMEM((rows, HEAD_DIM), F32)],
        compiler_params=_cp(("arbitrary", "arbitrary", "arbitrary")),
        name="attention",
    )(q, kt, v, ktc, vc)


def _outproj_router_kernel(x_ref, f_ref, a_ref, g1_ref, wo_ref, n2g_ref, sc_ref, sh_ref,
                           wrh_ref, wrl_ref, bias_ref, xn_ref, hp_ref, te_ref, gt_ref, cnt_ref):
    tm = x_ref.shape[1]
    mix = jnp.dot(f_ref[0].astype(BF16), wo_ref[:FOURIER_WIDTH, :], preferred_element_type=F32)
    for h in range(ATTN_HEADS):
        r0 = FOURIER_WIDTH + h * HEAD_DIM
        mix += jnp.dot(a_ref[0, h], wo_ref[r0:r0 + HEAD_DIM, :], preferred_element_type=F32)
    xn = x_ref[0] + g1_ref[0] * mix
    xn_ref[0] = xn
    h2 = _modulated_norm(xn, n2g_ref[...], sc_ref[0], sh_ref[0])
    h2_hi = h2.astype(BF16)
    h2_lo = (h2 - h2_hi.astype(F32)).astype(BF16)
    hp_ref[...] = _pack_rows(h2)

    wrh = wrh_ref[...]
    logits = (lax.dot_general(wrh, h2_hi, _NT, preferred_element_type=F32)
              + lax.dot_general(wrh, h2_lo, _NT, preferred_element_type=F32)
              + lax.dot_general(wrl_ref[...], h2_hi, _NT, preferred_element_type=F32))
    scores = _sigmoid(logits)
    sel = scores + bias_ref[...]

    neg = jnp.float32(-jnp.inf)
    s3 = sel.reshape(N_EXPERT_GROUPS, EXPERTS_PER_GROUP, tm)
    i3 = lax.broadcasted_iota(I32, s3.shape, 1)
    m1 = jnp.max(s3, axis=1, keepdims=True)
    i1 = jnp.min(jnp.where(s3 == m1, i3, EXPERTS_PER_GROUP), axis=1, keepdims=True)
    m2 = jnp.max(jnp.where(i3 == i1, neg, s3), axis=1)
    gs = m1[:, 0, :] + m2
    gi = lax.broadcasted_iota(I32, gs.shape, 0)
    keep = jnp.zeros(gs.shape, jnp.bool_)
    for _ in range(TOPK_GROUPS):
        m = jnp.max(gs, axis=0, keepdims=True)
        idx = jnp.min(jnp.where(gs == m, gi, N_EXPERT_GROUPS), axis=0, keepdims=True)
        hit = gi == idx
        keep = keep | hit
        gs = jnp.where(hit, neg, gs)
    keep3 = jnp.broadcast_to(keep[:, None, :], s3.shape)
    selm = jnp.where(keep3, s3, neg).reshape(N_EXPERTS, tm)

    ei = lax.broadcasted_iota(I32, selm.shape, 0)
    multi = jnp.zeros(selm.shape, F32)
    idxs, gates = [], []
    for _ in range(TOP_K):
        m = jnp.max(selm, axis=0, keepdims=True)
        idx = jnp.min(jnp.where(selm == m, ei, N_EXPERTS), axis=0, keepdims=True)
        hit = ei == idx
        gates.append(jnp.sum(jnp.where(hit, scores, 0.0), axis=0, keepdims=True))
        idxs.append(idx)
        selm = jnp.where(hit, neg, selm)
        multi = multi + hit.astype(F32)
    gate = jnp.concatenate(gates, axis=0)
    gate = gate / jnp.sum(gate, axis=0, keepdims=True) * ROUTED_SCALE
    te_ref[...] = jnp.concatenate(idxs, axis=0)
    gt_ref[...] = gate
    ones = jnp.ones((8, tm), BF16)
    cnt_ref[0] = lax.dot_general(ones, multi.astype(BF16), _NT, preferred_element_type=F32)


def _outproj_router(x, four, attn, g1, w_out, n2g, sc2, sh2, wr_hi, wr_lo, bias):
    b, l, d = x.shape
    tm = min(TOK_TILE, l)
    tpb = l // tm
    n = b * l
    const2 = lambda i, j: (0, 0)
    mod_map = lambda i, j: (i, 0, 0)
    tok_map = lambda i, j: (0, i * tpb + j)
    return pl.pallas_call(
        _outproj_router_kernel,
        grid=(b, tpb),
        in_specs=[pl.BlockSpec((1, tm, d), lambda i, j: (i, j, 0)),
                  pl.BlockSpec((1, tm, FOURIER_WIDTH), lambda i, j: (i, j, 0)),
                  pl.BlockSpec((1, ATTN_HEADS, tm, HEAD_DIM), lambda i, j: (i, 0, j, 0)),
                  pl.BlockSpec((1, 1, d), mod_map),
                  pl.BlockSpec((d, d), const2),
                  pl.BlockSpec((1, d), const2),
                  pl.BlockSpec((1, 1, d), mod_map),
                  pl.BlockSpec((1, 1, d), mod_map),
                  pl.BlockSpec((N_EXPERTS, d), const2),
                  pl.BlockSpec((N_EXPERTS, d), const2),
                  pl.BlockSpec((N_EXPERTS, 1), const2)],
        out_specs=[pl.BlockSpec((1, tm, d), lambda i, j: (i, j, 0)),
                   pl.BlockSpec((tm, HALF_D), lambda i, j: (i * tpb + j, 0)),
                   pl.BlockSpec((TOP_K, tm), tok_map),
                   pl.BlockSpec((TOP_K, tm), tok_map),
                   pl.BlockSpec((1, 8, N_EXPERTS), lambda i, j: (i * tpb + j, 0, 0))],
        out_shape=[jax.ShapeDtypeStruct((b, l, d), F32),
                   jax.ShapeDtypeStruct((n, HALF_D), I32),
                   jax.ShapeDtypeStruct((TOP_K, n), I32),
                   jax.ShapeDtypeStruct((TOP_K, n), F32),
                   jax.ShapeDtypeStruct((n // tm, 8, N_EXPERTS), F32)],
        compiler_params=_cp(("arbitrary", "arbitrary")),
        name="outproj_router",
    )(x, four, attn, g1, w_out, n2g, sc2, sh2, wr_hi, wr_lo, bias)


def _dest_kernel(te_ref, base_ref, tri_ref, d_ref):
    te = te_ref[...]
    tm = te.shape[1]
    ei = lax.broadcasted_iota(I32, (N_EXPERTS, tm), 0)
    hits = [ei == te[k:k + 1, :] for k in range(TOP_K)]
    multi = hits[0].astype(F32)
    for k in range(1, TOP_K):
        multi = multi + hits[k].astype(F32)
    rank = jnp.dot(multi.astype(BF16), tri_ref[...], preferred_element_type=F32)
    pos = rank + base_ref[0]
    rows = [jnp.sum(jnp.where(hits[k], pos, 0.0), axis=0, keepdims=True) for k in range(TOP_K)]
    d_ref[...] = jnp.concatenate(rows, axis=0).astype(I32)


def _dest_rows(top_e, base, tm):
    n = top_e.shape[1]
    tri = jnp.asarray(np.triu(np.ones((tm, tm), np.float32), 1)).astype(BF16)
    return pl.pallas_call(
        _dest_kernel,
        grid=(n // tm,),
        in_specs=[pl.BlockSpec((TOP_K, tm), lambda i: (0, i)),
                  pl.BlockSpec((1, N_EXPERTS, 1), lambda i: (i, 0, 0)),
                  pl.BlockSpec((tm, tm), lambda i: (0, 0))],
        out_specs=pl.BlockSpec((TOP_K, tm), lambda i: (0, i)),
        out_shape=jax.ShapeDtypeStruct((TOP_K, n), I32),
        compiler_params=_cp(("arbitrary",)),
        name="dest_rows",
    )(top_e, base, tri)


def _scatter_kernel(d_ref, hp_ref, xz_ref, xr_ref, sem):
    del xz_ref
    tm = hp_ref.shape[0]

    def body(t, carry):
        for k in range(TOP_K):
            pltpu.make_async_copy(hp_ref.at[pl.ds(t, 1)], xr_ref.at[pl.ds(d_ref[k, t], 1)], sem).start()
        return carry

    lax.fori_loop(0, tm, body, 0)
    for k in range(TOP_K):
        pltpu.make_async_copy(hp_ref, xr_ref.at[pl.ds(0, tm)], sem).wait()


def _scatter_rows(dest, hp, n_rows, tm):
    n = hp.shape[0]
    zeros = jnp.zeros((n_rows, HALF_D), I32)
    return pl.pallas_call(
        _scatter_kernel,
        grid=(n // tm,),
        in_specs=[pl.BlockSpec((TOP_K, tm), lambda i: (0, i), memory_space=pltpu.SMEM),
                  pl.BlockSpec((tm, HALF_D), lambda i: (i, 0)),
                  pl.BlockSpec(memory_space=pl.ANY)],
        out_specs=pl.BlockSpec(memory_space=pl.ANY),
        out_shape=jax.ShapeDtypeStruct((n_rows, HALF_D), I32),
        scratch_shapes=[pltpu.SemaphoreType.DMA(())],
        input_output_aliases={2: 0},
        compiler_params=_cp(("arbitrary",)),
        name="scatter_rows",
    )(dest, hp, zeros)


def _swiglu(h_lo, h_hi, wgu, wdn):
    ag = (jnp.dot(h_lo, wgu[:HALF_D, :], preferred_element_type=F32)
          + jnp.dot(h_hi, wgu[HALF_D:, :], preferred_element_type=F32))
    ff = ag.shape[1] // 2
    a, g = ag[:, :ff], ag[:, ff:]
    mid = (a * _sigmoid(a) * g).astype(BF16)
    return jnp.dot(mid, wdn[...], preferred_element_type=F32)


def _expert_kernel(be_ref, nu_ref, x_ref, wgu_ref, wdn_ref, y_ref, wgu_s, wdn_s):
    i = pl.program_id(0)

    @pl.when(i < nu_ref[0])
    def _():
        prev = be_ref[jnp.maximum(i - 1, 0)]

        @pl.when((i == 0) | (be_ref[i] != prev))
        def _():
            wgu_s[...] = wgu_ref[0].astype(BF16)
            wdn_s[...] = wdn_ref[0].astype(BF16)

        lo, hi = _unpack_rows(x_ref[...])
        y_ref[...] = _pack_rows(_swiglu(lo.astype(BF16), hi.astype(BF16), wgu_s, wdn_s))

    @pl.when(i >= nu_ref[0])
    def _():
        y_ref[...] = jnp.zeros(y_ref.shape, I32)


def _experts(block_e, n_used, xrows, w_gu, w_down):
    n_rows = xrows.shape[0]
    rb = ROW_BLOCK
    d, ff2 = w_gu.shape[1], w_gu.shape[2]
    row_map = lambda i, be, nu: (i, 0)
    grid_spec = pltpu.PrefetchScalarGridSpec(
        num_scalar_prefetch=2,
        grid=(n_rows // rb,),
        in_specs=[pl.BlockSpec((rb, HALF_D), row_map),
                  pl.BlockSpec((1, d, ff2), lambda i, be, nu: (be[i], 0, 0)),
                  pl.BlockSpec((1, ff2 // 2, d), lambda i, be, nu: (be[i], 0, 0))],
        out_specs=pl.BlockSpec((rb, HALF_D), row_map),
        scratch_shapes=[pltpu.VMEM((d, ff2), BF16), pltpu.VMEM((ff2 // 2, d), BF16)],
    )
    return pl.pallas_call(
        _expert_kernel,
        grid_spec=grid_spec,
        out_shape=jax.ShapeDtypeStruct((n_rows, HALF_D), I32),
        compiler_params=_cp(("arbitrary",)),
        name="experts",
    )(block_e, n_used, xrows, w_gu, w_down)


def _combine_kernel(dc_ref, dn_ref, gate_ref, xn_ref, hp_ref, g2_ref, wsgu_ref, wsdn_ref, fng_ref,
                    y_hbm, o_ref, buf, sem):
    i = pl.program_id(0)
    n = pl.num_programs(0)
    tm = xn_ref.shape[0]
    slot = i % 2

    def issue(d_ref, s):
        def body(t, carry):
            for k in range(TOP_K):
                pltpu.make_async_copy(y_hbm.at[pl.ds(d_ref[k, t], 1)], buf.at[s, k, pl.ds(t, 1)],
                                      sem.at[s]).start()
            return carry
        lax.fori_loop(0, tm, body, 0)

    @pl.when(i == 0)
    def _():
        issue(dc_ref, 0)

    @pl.when(i + 1 < n)
    def _():
        issue(dn_ref, 1 - slot)

    lo, hi = _unpack_rows(hp_ref[...])
    shared = _swiglu(lo.astype(BF16), hi.astype(BF16), wsgu_ref, wsdn_ref)

    for k in range(TOP_K):
        pltpu.make_async_copy(y_hbm.at[pl.ds(0, tm)], buf.at[slot, k], sem.at[slot]).wait()
    gate = gate_ref[...]
    y_lo = jnp.zeros((tm, HALF_D), F32)
    y_hi = jnp.zeros((tm, HALF_D), F32)
    for k in range(TOP_K):
        r_lo, r_hi = _unpack_rows(buf[slot, k])
        gk = gate[:, k:k + 1]
        y_lo += gk * r_lo
        y_hi += gk * r_hi
    y = jnp.concatenate([y_lo, y_hi], axis=1) + shared
    xo = xn_ref[...] + g2_ref[0] * y
    ms = jnp.mean(xo * xo, axis=-1, keepdims=True)
    o_ref[...] = xo * lax.rsqrt(ms + RMS_EPS) * fng_ref[...]


def _combine(dest, gate_t, xn, hp, g2, ws_gu, ws_dn, fng, yrows, tokens_per_batch):
    n, d = xn.shape
    tm = min(COMB_TILE, tokens_per_batch)
    nt = n // tm
    tpb = tokens_per_batch // tm
    const2 = lambda i: (0, 0)
    return pl.pallas_call(
        _combine_kernel,
        grid=(nt,),
        in_specs=[pl.BlockSpec((TOP_K, tm), lambda i: (0, i), memory_space=pltpu.SMEM),
                  pl.BlockSpec((TOP_K, tm), lambda i: (0, jnp.minimum(i + 1, nt - 1)),
                               memory_space=pltpu.SMEM),
                  pl.BlockSpec((tm, TOP_K), lambda i: (i, 0)),
                  pl.BlockSpec((tm, d), lambda i: (i, 0)),
                  pl.BlockSpec((tm, HALF_D), lambda i: (i, 0)),
                  pl.BlockSpec((1, 1, d), lambda i: (i // tpb, 0, 0)),
                  pl.BlockSpec(ws_gu.shape, const2),
                  pl.BlockSpec(ws_dn.shape, const2),
                  pl.BlockSpec((1, d), const2),
                  pl.BlockSpec(memory_space=pl.ANY)],
        out_specs=pl.BlockSpec((tm, d), lambda i: (i, 0)),
        out_shape=jax.ShapeDtypeStruct((n, d), F32),
        scratch_shapes=[pltpu.VMEM((2, TOP_K, tm, HALF_D), I32), pltpu.SemaphoreType.DMA((2,))],
        compiler_params=_cp(("arbitrary",)),
        name="combine",
    )(dest, dest, gate_t, xn, hp, g2, ws_gu, ws_dn, fng, yrows)


def _rope_tables(l):
    rows = l // GRID_W
    row = np.repeat(np.arange(rows, dtype=np.float32), GRID_W)
    col = np.tile(np.arange(GRID_W, dtype=np.float32), rows)
    n_freq = AXIS_ROT // 2
    inv_freq = (np.float32(ROPE_THETA) ** (-np.arange(n_freq, dtype=np.float32) / n_freq)).astype(np.float32)
    ang_r = row[:, None] * inv_freq
    ang_c = col[:, None] * inv_freq
    ang = np.concatenate([ang_r, ang_r, ang_c, ang_c], axis=-1).astype(np.float64)
    cos, sin = np.cos(ang), np.sin(ang)
    lane = np.arange(HEAD_DIM) % AXIS_ROT
    first = lane < AXIS_ROT // 2
    sin_a = np.where(first[None, :], -sin, 0.0)
    sin_b = np.where(first[None, :], 0.0, sin)
    two = lambda a: jnp.asarray(np.concatenate([a, a], axis=1), dtype=F32)
    return two(cos), two(sin_a), two(sin_b)


def _layer(x, ctx, c, c_ctx, lw, moe_w, fng):
    norm1_g, w_ada, b_ada, w_in, w_fourier, q_norm_g, k_norm_g, w_out, norm2_g = lw
    w_router, router_bias, w_expert_gu, w_expert_down, w_shared_gu, w_shared_down = moe_w
    b, l, d = x.shape
    n = b * l

    cond = jnp.concatenate([c, c_ctx[None, :], jnp.zeros((8 - b - 1, d), F32)], axis=0)
    mod = _adaln(cond, w_ada, b_ada)
    sh1, sc1, g1, sh2, sc2, g2 = [m[:b, None, :] for m in jnp.split(mod, 6, axis=-1)]
    csh1, csc1 = [m[b:b + 1, None, :] for m in jnp.split(mod, 6, axis=-1)[:2]]

    w_in_bf = w_in.astype(BF16)
    two = lambda g, s: jnp.tile(g * s, 2).reshape(1, LANES)
    gq, gk = two(q_norm_g, ATTN_SCALE), two(k_norm_g, 1.0)
    ones_bd = jnp.asarray(np.kron(np.eye(2), np.ones((HEAD_DIM, HEAD_DIM))), dtype=F32).astype(BF16)
    n1g = norm1_g.reshape(1, d)

    ktc, vc = _inproj(ctx, csc1, csh1, n1g, w_in_bf[:, KV_COL0:], gq, gk, ones_bd, None, False)
    u, q, kt, v = _inproj(x, sc1, sh1, n1g, w_in_bf, gq, gk, ones_bd, _rope_tables(l), True)

    w_bd = jnp.zeros((FOURIER_WIDTH, FOURIER_WIDTH), F32)
    for h in range(FOURIER_HEADS):
        w_bd = lax.dynamic_update_slice(w_bd, w_fourier[h], (h * HEAD_DIM, h * HEAD_DIM))
    four = _fourier(u, w_bd.astype(BF16))
    attn = _attention(q, kt, v, ktc, vc)

    wr_t = w_router.T
    wr_hi = wr_t.astype(BF16)
    wr_lo = (wr_t - wr_hi.astype(F32)).astype(BF16)
    xn, hp, top_e, gate, cnt = _outproj_router(
        x, four, attn, g1, w_out.astype(BF16), norm2_g.reshape(1, d), sc2, sh2,
        wr_hi, wr_lo, router_bias.reshape(N_EXPERTS, 1))

    tm = min(TOK_TILE, l)
    rb = ROW_BLOCK
    counts = cnt[:, 0, :].astype(I32)
    total = jnp.sum(counts, axis=0)
    padded = (total + rb - 1) // rb * rb
    pad_end = jnp.cumsum(padded)
    pad_start = pad_end - padded
    base = pad_start[None, :] + jnp.cumsum(counts, axis=0) - counts
    n_blocks = n * TOP_K // rb + N_EXPERTS
    n_used = (pad_end[-1] // rb).astype(I32)
    blk = jnp.minimum(jnp.arange(n_blocks, dtype=I32), n_used - 1)
    block_e = jnp.minimum(jnp.searchsorted(pad_end, blk * rb, side="right"), N_EXPERTS - 1).astype(I32)

    dest = _dest_rows(top_e, base.astype(F32)[:, :, None], tm)
    xrows = _scatter_rows(dest, hp, n_blocks * rb, tm)
    yrows = _experts(block_e, n_used.reshape(1), xrows, w_expert_gu, w_expert_down)
    return _combine(dest, gate.T, xn.reshape(n, d), hp, g2, w_shared_gu.astype(BF16),
                    w_shared_down.astype(BF16), fng, yrows, l)


def kernel(x, c, ctx, c_ctx, norm1_g, w_ada, b_ada, w_in, w_fourier, q_norm_g, k_norm_g, w_out, norm2_g,
           w_router, router_bias, w_expert_gu, w_expert_down, w_shared_gu, w_shared_down, final_norm_g):
    depth = norm1_g.shape[0]
    assert depth == 1, "context update between layers is not implemented"
    b, l, d = x.shape
    lw = (norm1_g[0], w_ada[0], b_ada[0], w_in[0], w_fourier[0], q_norm_g[0], k_norm_g[0], w_out[0], norm2_g[0])
    moe_w = (w_router[0], router_bias[0], w_expert_gu[0], w_expert_down[0], w_shared_gu[0], w_shared_down[0])
    out = _layer(x, ctx, c, c_ctx, lw, moe_w, final_norm_g.reshape(1, d))
    return out.reshape(b, l, d)
```

```python
import functools

import numpy as np
import jax
import jax.numpy as jnp
from jax import lax
from jax.experimental import pallas as pl
from jax.experimental.pallas import tpu as pltpu

F32 = jnp.float32
BF16 = jnp.bfloat16
I32 = jnp.int32

D_MODEL = 1024
GRID_W = 64
HEAD_DIM = 64
FOURIER_HEADS = 4
FOURIER_WIDTH = 256
ATTN_HEADS = 12
KV_HEADS = 4
Q_PER_KV = 3
ATTN_WIDTH = 768
KV_WIDTH = 256
KV_COL0 = 1024
IN_WIDTH = 1536
ATTN_SCALE = HEAD_DIM ** -0.5
ROPE_THETA = 10000.0
AXIS_ROT = HEAD_DIM // 2
N_EXPERTS = 256
TOP_K = 8
N_EXPERT_GROUPS = 8
TOPK_GROUPS = 4
EXPERTS_PER_GROUP = 32
EXPERT_FF = 256
ROUTED_SCALE = 2.5
RMS_EPS = 1e-6

LANES = 128
HALF_D = D_MODEL // 2
DFT_L1 = 64
VMEM_LIMIT = 48 * 1024 * 1024

TOK_TILE = 512
ATTN_BQ = 256
ATTN_BK = 1024
ONES_ROWS = 16
ROW_BLOCK = 256
COMB_TILE = 256

_HI = lax.Precision.HIGHEST
_NT = (((1,), (1,)), ((), ()))


def _cp(sem, vmem=VMEM_LIMIT):
    return pltpu.CompilerParams(dimension_semantics=sem, vmem_limit_bytes=vmem)


def _sigmoid(v):
    return 1.0 / (1.0 + jnp.exp(-v))


def _pack_rows(v):
    half = v.shape[1] // 2
    lo = lax.bitcast_convert_type(v[:, :half].astype(BF16).astype(F32), I32)
    hi = lax.bitcast_convert_type(v[:, half:].astype(BF16).astype(F32), I32)
    return lax.shift_right_logical(lo, 16) | (hi & jnp.int32(-65536))


def _unpack_rows(w):
    lo = lax.bitcast_convert_type(lax.shift_left(w, 16), F32)
    hi = lax.bitcast_convert_type(w & jnp.int32(-65536), F32)
    return lo, hi


def _adaln_kernel(c_ref, w_ref, b_ref, o_ref):
    c = c_ref[...]
    s = c * _sigmoid(c)
    o_ref[...] = jnp.dot(s, w_ref[...], precision=_HI, preferred_element_type=F32) + b_ref[...]


def _adaln(cond, w_ada, b_ada):
    rows, d = cond.shape
    n = w_ada.shape[1]
    tn = 1536
    return pl.pallas_call(
        _adaln_kernel,
        grid=(n // tn,),
        in_specs=[pl.BlockSpec((rows, d), lambda j: (0, 0)),
                  pl.BlockSpec((d, tn), lambda j: (0, j)),
                  pl.BlockSpec((1, tn), lambda j: (0, j))],
        out_specs=pl.BlockSpec((rows, tn), lambda j: (0, j)),
        out_shape=jax.ShapeDtypeStruct((rows, n), F32),
        compiler_params=_cp(("arbitrary",)),
        name="adaln",
    )(cond, w_ada, b_ada.reshape(1, n))


def _modulated_norm(x, g, sc, sh):
    ms = jnp.mean(x * x, axis=-1, keepdims=True)
    return x * lax.rsqrt(ms + RMS_EPS) * g * (1.0 + sc) + sh


def _head_norm(chunk, gain, ones_bd):
    sq = chunk * chunk
    hi = sq.astype(BF16)
    lo = (sq - hi.astype(F32)).astype(BF16)
    ss = (jnp.dot(hi, ones_bd, preferred_element_type=F32)
          + jnp.dot(lo, ones_bd, preferred_element_type=F32))
    return chunk * lax.rsqrt(ss * (1.0 / HEAD_DIM) + RMS_EPS) * gain


def _rope(t, cos, sin_a, sin_b):
    return (t * cos + pltpu.roll(t, LANES - AXIS_ROT // 2, 1) * sin_a
            + pltpu.roll(t, AXIS_ROT // 2, 1) * sin_b)


def _inproj_kernel(*refs, rope, with_uq):
    if rope:
        (x_ref, sc_ref, sh_ref, g_ref, w_ref, gq_ref, gk_ref, ones_ref,
         cos_ref, sa_ref, sb_ref) = refs[:11]
        outs = refs[11:]
    else:
        x_ref, sc_ref, sh_ref, g_ref, w_ref, gq_ref, gk_ref, ones_ref = refs[:8]
        outs = refs[8:]
    if with_uq:
        u_ref, qt_ref, k_ref, vt_ref = outs
    else:
        k_ref, vt_ref = outs

    h = _modulated_norm(x_ref[0], g_ref[...], sc_ref[0], sh_ref[0])
    z = jnp.dot(h.astype(BF16), w_ref[...], preferred_element_type=F32)
    ones_bd = ones_ref[...]

    def normed(chunk, gain):
        t = _head_norm(chunk, gain, ones_bd)
        if rope:
            t = _rope(t, cos_ref[...], sa_ref[...], sb_ref[...])
        return t

    col = 0
    if with_uq:
        u_ref[0] = z[:, :FOURIER_WIDTH].astype(BF16)
        col = FOURIER_WIDTH
        qs = [normed(z[:, col + j * LANES: col + (j + 1) * LANES], gq_ref[...])
              for j in range(ATTN_WIDTH // LANES)]
        qt_ref[0] = jnp.concatenate(qs, axis=1).T.astype(BF16)
        col += ATTN_WIDTH
    for j in range(KV_WIDTH // LANES):
        t = normed(z[:, col + j * LANES: col + (j + 1) * LANES], gk_ref[...])
        k_ref[0, 2 * j] = t[:, :HEAD_DIM].astype(BF16)
        k_ref[0, 2 * j + 1] = t[:, HEAD_DIM:].astype(BF16)
    col += KV_WIDTH
    vt_ref[0] = z[:, col:col + KV_WIDTH].T.astype(BF16)


def _inproj(x, sc, sh, g, w, gq, gk, ones_bd, rope_tabs, with_uq):
    b, t, d = x.shape
    tm = min(TOK_TILE, t)
    wcols = w.shape[1]
    bm = sc.shape[0]
    mod_map = (lambda i, j: (i, 0, 0)) if bm == b else (lambda i, j: (0, 0, 0))
    const2 = lambda i, j: (0, 0)
    in_specs = [pl.BlockSpec((1, tm, d), lambda i, j: (i, j, 0)),
                pl.BlockSpec((1, 1, d), mod_map),
                pl.BlockSpec((1, 1, d), mod_map),
                pl.BlockSpec((1, d), const2),
                pl.BlockSpec((d, wcols), const2),
                pl.BlockSpec((1, LANES), const2),
                pl.BlockSpec((1, LANES), const2),
                pl.BlockSpec((LANES, LANES), const2)]
    args = [x, sc, sh, g, w, gq, gk, ones_bd]
    rope = rope_tabs is not None
    if rope:
        in_specs += [pl.BlockSpec((tm, LANES), lambda i, j: (j, 0))] * 3
        args += list(rope_tabs)
    out_specs, out_shape = [], []
    if with_uq:
        out_specs += [pl.BlockSpec((1, tm, FOURIER_WIDTH), lambda i, j: (i, j, 0)),
                      pl.BlockSpec((1, ATTN_WIDTH, tm), lambda i, j: (i, 0, j))]
        out_shape += [jax.ShapeDtypeStruct((b, t, FOURIER_WIDTH), BF16),
                      jax.ShapeDtypeStruct((b, ATTN_WIDTH, t), BF16)]
    out_specs += [pl.BlockSpec((1, KV_HEADS, tm, HEAD_DIM), lambda i, j: (i, 0, j, 0)),
                  pl.BlockSpec((1, KV_WIDTH, tm), lambda i, j: (i, 0, j))]
    out_shape += [jax.ShapeDtypeStruct((b, KV_HEADS, t, HEAD_DIM), BF16),
                  jax.ShapeDtypeStruct((b, KV_WIDTH, t), BF16)]
    return pl.pallas_call(
        functools.partial(_inproj_kernel, rope=rope, with_uq=with_uq),
        grid=(b, t // tm),
        in_specs=in_specs,
        out_specs=out_specs,
        out_shape=out_shape,
        compiler_params=_cp(("arbitrary", "arbitrary")),
        name="inproj_latent" if with_uq else "inproj_ctx",
    )(*args)


def _fourier_a_kernel(u_ref, c_ref, s_ref, yr_ref, yi_ref):
    u = u_ref[0]
    yr_ref[0] = jnp.dot(c_ref[...], u, preferred_element_type=F32).astype(BF16)
    yi_ref[0] = jnp.dot(s_ref[...], u, preferred_element_type=F32).astype(BF16)


def _fourier_b_kernel(yr_ref, yi_ref, m_ref, c_ref, s_ref, w_ref, o_ref, *, kb):
    y = jnp.concatenate([yr_ref[0], yi_ref[0]], axis=1)
    x = jnp.einsum("kab,kbc->kac", m_ref[...], y, preferred_element_type=F32)
    xr = x[:, :DFT_L1].reshape(kb * DFT_L1, FOURIER_WIDTH).astype(BF16)
    xi = x[:, DFT_L1:].reshape(kb * DFT_L1, FOURIER_WIDTH).astype(BF16)
    spec = (jnp.dot(xr, c_ref[...], preferred_element_type=F32)
            + jnp.dot(xi, s_ref[...], preferred_element_type=F32))
    o = jnp.dot(spec.astype(BF16), w_ref[...], preferred_element_type=F32)
    for j in range(kb):
        o_ref[0, :, j, :] = o[j * DFT_L1:(j + 1) * DFT_L1]


def _dft_tables(l):
    l2 = l // DFT_L1
    k2 = np.arange(l2)
    ang2 = 2.0 * np.pi * ((k2[:, None] * k2[None, :]) % l2) / l2
    c2 = np.cos(ang2)
    s2n = -np.sin(ang2)
    n1 = np.arange(DFT_L1)
    k = l2 * n1[None, :, None] + k2[:, None, None]
    ang = 2.0 * np.pi * ((k * n1[None, None, :]) % l) / l
    mr, mi = np.cos(ang), -np.sin(ang)
    m = np.concatenate([np.concatenate([mr, -mi], axis=2),
                        np.concatenate([mi, mr], axis=2)], axis=1)
    d = np.arange(HEAD_DIM)
    angc = 2.0 * np.pi * ((d[:, None] * d[None, :]) % HEAD_DIM) / HEAD_DIM
    scale = 1.0 / np.sqrt(float(l) * HEAD_DIM)
    eye = np.eye(FOURIER_HEADS)
    cbd = np.kron(eye, np.cos(angc) * scale)
    sbd = np.kron(eye, np.sin(angc) * scale)
    as_bf = lambda a: jnp.asarray(a, dtype=F32).astype(BF16)
    return as_bf(c2), as_bf(s2n), as_bf(m), as_bf(cbd), as_bf(sbd)


def _fourier(u, w_bd):
    b, l, fw = u.shape
    l2 = l // DFT_L1
    ncol = DFT_L1 * fw
    tn = min(4096, ncol)
    kb = min(8, l2)
    c2, s2n, m, cbd, sbd = _dft_tables(l)
    yr, yi = pl.pallas_call(
        _fourier_a_kernel,
        grid=(b, ncol // tn),
        in_specs=[pl.BlockSpec((1, l2, tn), lambda i, j: (i, 0, j)),
                  pl.BlockSpec((l2, l2), lambda i, j: (0, 0)),
                  pl.BlockSpec((l2, l2), lambda i, j: (0, 0))],
        out_specs=[pl.BlockSpec((1, l2, tn), lambda i, j: (i, 0, j))] * 2,
        out_shape=[jax.ShapeDtypeStruct((b, l2, ncol), BF16)] * 2,
        compiler_params=_cp(("arbitrary", "arbitrary")),
        name="fourier_a",
    )(u.reshape(b, l2, ncol), c2, s2n)
    yr = yr.reshape(b, l2, DFT_L1, fw)
    yi = yi.reshape(b, l2, DFT_L1, fw)
    out = pl.pallas_call(
        functools.partial(_fourier_b_kernel, kb=kb),
        grid=(b, l2 // kb),
        in_specs=[pl.BlockSpec((1, kb, DFT_L1, fw), lambda i, j: (i, j, 0, 0)),
                  pl.BlockSpec((1, kb, DFT_L1, fw), lambda i, j: (i, j, 0, 0)),
                  pl.BlockSpec((kb, 2 * DFT_L1, 2 * DFT_L1), lambda i, j: (j, 0, 0)),
                  pl.BlockSpec((fw, fw), lambda i, j: (0, 0)),
                  pl.BlockSpec((fw, fw), lambda i, j: (0, 0)),
                  pl.BlockSpec((fw, fw), lambda i, j: (0, 0))],
        out_specs=pl.BlockSpec((1, DFT_L1, kb, fw), lambda i, j: (i, 0, j, 0)),
        out_shape=jax.ShapeDtypeStruct((b, DFT_L1, l2, fw), F32),
        compiler_params=_cp(("arbitrary", "arbitrary")),
        name="fourier_b",
    )(yr, yi, m, cbd, sbd, w_bd)
    return out.reshape(b, l, fw)


def _attn_kernel(qt_ref, k_ref, vt_ref, kc_ref, vct_ref, o_ref, m_scr, acc_scr, *s_scrs, bk, nchunks):
    bq = qt_ref.shape[2]
    m_scr[...] = jnp.full(m_scr.shape, -jnp.inf, F32)
    acc_scr[...] = jnp.zeros(acc_scr.shape, F32)

    def scores(k, buf, h):
        nk = k.shape[0]
        qt = qt_ref[0, h * HEAD_DIM:(h + 1) * HEAD_DIM, :]
        s_scrs[buf][:nk, :] = jnp.dot(k, qt, preferred_element_type=F32)

    def softmax_pv(buf, h, vt):
        nk = vt.shape[1]
        s = s_scrs[buf][:nk, :]
        m_old = m_scr[h]
        m_new = jnp.maximum(m_old, jnp.max(s, axis=0, keepdims=True))
        alpha = jnp.exp2(m_old - m_new)
        p = jnp.exp2((s - m_new).astype(BF16))
        vt_ones = jnp.concatenate([vt, jnp.ones((ONES_ROWS, nk), BF16)], axis=0)
        acc_scr[h] = alpha * acc_scr[h] + jnp.dot(vt_ones, p, preferred_element_type=F32)
        m_scr[h] = m_new

    def keys(c):
        if isinstance(c, int) and c == nchunks:
            return kc_ref[0, 0]
        return k_ref[0, 0, pl.ds(pl.multiple_of(c * bk, bk), bk), :]

    def values(c):
        if isinstance(c, int) and c == nchunks:
            return vct_ref[0]
        return vt_ref[0, :, pl.ds(pl.multiple_of(c * bk, bk), bk)]

    nbuf = len(s_scrs)
    pair = 1
    ahead = nbuf - pair
    n_units = Q_PER_KV * (nchunks + 1)

    def group(us, chunk0=0):
        for u in us:
            ua = u + ahead
            if not (isinstance(chunk0, int) and ua >= n_units):
                scores(keys(chunk0 + ua // Q_PER_KV), ua % nbuf, ua % Q_PER_KV)
        for u in us:
            softmax_pv(u % nbuf, u % Q_PER_KV, values(chunk0 + u // Q_PER_KV))

    for u in range(min(ahead, n_units)):
        scores(keys(u // Q_PER_KV), u % nbuf, u % Q_PER_KV)

    chunks_per_iter = nbuf // Q_PER_KV
    n_iter = max(0, (Q_PER_KV * nchunks - ahead) // nbuf)

    def body(j, carry):
        for t in range(0, nbuf, pair):
            group(range(t, t + pair), j * chunks_per_iter)
        return carry

    lax.fori_loop(0, n_iter, body, 0)
    for t in range(n_iter * nbuf, n_units, pair):
        group(range(t, min(t + pair, n_units)))
    outs = [acc_scr[h, :HEAD_DIM, :] / acc_scr[h, HEAD_DIM:HEAD_DIM + 1, :] for h in range(Q_PER_KV)]
    o_t = jnp.concatenate(outs + [jnp.zeros((HEAD_DIM, bq), F32)], axis=0)
    o_ref[0, 0] = o_t.T[:, :Q_PER_KV * HEAD_DIM].astype(BF16)


def _attention(qt, k, vt, kc, vct):
    b, _, l = qt.shape
    c = kc.shape[2]
    bq = min(ATTN_BQ, l)
    bk = min(ATTN_BK, l)
    gw = Q_PER_KV * HEAD_DIM
    return pl.pallas_call(
        functools.partial(_attn_kernel, bk=bk, nchunks=l // bk),
        grid=(b, KV_HEADS, l // bq),
        in_specs=[pl.BlockSpec((1, gw, bq), lambda i, g, j: (i, g, j)),
                  pl.BlockSpec((1, 1, l, HEAD_DIM), lambda i, g, j: (i, g, 0, 0)),
                  pl.BlockSpec((1, HEAD_DIM, l), lambda i, g, j: (i, g, 0)),
                  pl.BlockSpec((1, 1, c, HEAD_DIM), lambda i, g, j: (i, g, 0, 0)),
                  pl.BlockSpec((1, HEAD_DIM, c), lambda i, g, j: (i, g, 0))],
        out_specs=pl.BlockSpec((1, 1, bq, gw), lambda i, g, j: (i, g, j, 0)),
        out_shape=jax.ShapeDtypeStruct((b, KV_HEADS, l, gw), BF16),
        scratch_shapes=[pltpu.VMEM((Q_PER_KV, 1, bq), F32),
                        pltpu.VMEM((Q_PER_KV, HEAD_DIM + ONES_ROWS, bq), F32),
                        *[pltpu.VMEM((max(bk, c), bq), F32)] * (2 * Q_PER_KV)],
        compiler_params=_cp(("arbitrary", "arbitrary", "arbitrary")),
        name="attention",
    )(qt, k, vt, kc, vct)


def _outproj_router_kernel(x_ref, f_ref, a_ref, g1_ref, wo_ref, n2g_ref, sc_ref, sh_ref,
                           wrh_ref, wrl_ref, bias_ref, xn_ref, hp_ref, te_ref, gt_ref, cnt_ref):
    tm = x_ref.shape[1]
    mix = jnp.dot(f_ref[0].astype(BF16), wo_ref[:FOURIER_WIDTH, :], preferred_element_type=F32)
    gw = Q_PER_KV * HEAD_DIM
    for g in range(KV_HEADS):
        r0 = FOURIER_WIDTH + g * gw
        mix += jnp.dot(a_ref[0, g], wo_ref[r0:r0 + gw, :], preferred_element_type=F32)
    xn = x_ref[0] + g1_ref[0] * mix
    xn_ref[0] = xn
    h2 = _modulated_norm(xn, n2g_ref[...], sc_ref[0], sh_ref[0])
    h2_hi = h2.astype(BF16)
    h2_lo = (h2 - h2_hi.astype(F32)).astype(BF16)
    hp_ref[...] = _pack_rows(h2)

    wrh = wrh_ref[...]
    logits = (lax.dot_general(wrh, h2_hi, _NT, preferred_element_type=F32)
              + lax.dot_general(wrh, h2_lo, _NT, preferred_element_type=F32)
              + lax.dot_general(wrl_ref[...], h2_hi, _NT, preferred_element_type=F32))
    scores = _sigmoid(logits)
    sel = scores + bias_ref[...]

    neg = jnp.float32(-jnp.inf)
    s3 = sel.reshape(N_EXPERT_GROUPS, EXPERTS_PER_GROUP, tm)
    i3 = lax.broadcasted_iota(I32, s3.shape, 1)
    m1 = jnp.max(s3, axis=1, keepdims=True)
    i1 = jnp.min(jnp.where(s3 == m1, i3, EXPERTS_PER_GROUP), axis=1, keepdims=True)
    m2 = jnp.max(jnp.where(i3 == i1, neg, s3), axis=1)
    gs = m1[:, 0, :] + m2
    gi = lax.broadcasted_iota(I32, gs.shape, 0)
    keep = jnp.zeros(gs.shape, jnp.bool_)
    for _ in range(TOPK_GROUPS):
        m = jnp.max(gs, axis=0, keepdims=True)
        idx = jnp.min(jnp.where(gs == m, gi, N_EXPERT_GROUPS), axis=0, keepdims=True)
        hit = gi == idx
        keep = keep | hit
        gs = jnp.where(hit, neg, gs)
    keep3 = jnp.broadcast_to(keep[:, None, :], s3.shape)
    selm = jnp.where(keep3, s3, neg).reshape(N_EXPERTS, tm)

    ei = lax.broadcasted_iota(I32, selm.shape, 0)
    multi = jnp.zeros(selm.shape, F32)
    idxs, gates = [], []
    for _ in range(TOP_K):
        m = jnp.max(selm, axis=0, keepdims=True)
        idx = jnp.min(jnp.where(selm == m, ei, N_EXPERTS), axis=0, keepdims=True)
        hit = ei == idx
        gates.append(jnp.sum(jnp.where(hit, scores, 0.0), axis=0, keepdims=True))
        idxs.append(idx)
        selm = jnp.where(hit, neg, selm)
        multi = multi + hit.astype(F32)
    gate = jnp.concatenate(gates, axis=0)
    gate = gate / jnp.sum(gate, axis=0, keepdims=True) * ROUTED_SCALE
    te_ref[...] = jnp.concatenate(idxs, axis=0)
    gt_ref[...] = gate
    ones = jnp.ones((8, tm), BF16)
    cnt_ref[0] = lax.dot_general(ones, multi.astype(BF16), _NT, preferred_element_type=F32)


def _outproj_router(x, four, attn, g1, w_out, n2g, sc2, sh2, wr_hi, wr_lo, bias):
    b, l, d = x.shape
    tm = min(TOK_TILE, l)
    tpb = l // tm
    n = b * l
    const2 = lambda i, j: (0, 0)
    mod_map = lambda i, j: (i, 0, 0)
    tok_map = lambda i, j: (0, i * tpb + j)
    return pl.pallas_call(
        _outproj_router_kernel,
        grid=(b, tpb),
        in_specs=[pl.BlockSpec((1, tm, d), lambda i, j: (i, j, 0)),
                  pl.BlockSpec((1, tm, FOURIER_WIDTH), lambda i, j: (i, j, 0)),
                  pl.BlockSpec((1, KV_HEADS, tm, Q_PER_KV * HEAD_DIM), lambda i, j: (i, 0, j, 0)),
                  pl.BlockSpec((1, 1, d), mod_map),
                  pl.BlockSpec((d, d), const2),
                  pl.BlockSpec((1, d), const2),
                  pl.BlockSpec((1, 1, d), mod_map),
                  pl.BlockSpec((1, 1, d), mod_map),
                  pl.BlockSpec((N_EXPERTS, d), const2),
                  pl.BlockSpec((N_EXPERTS, d), const2),
                  pl.BlockSpec((N_EXPERTS, 1), const2)],
        out_specs=[pl.BlockSpec((1, tm, d), lambda i, j: (i, j, 0)),
                   pl.BlockSpec((tm, HALF_D), lambda i, j: (i * tpb + j, 0)),
                   pl.BlockSpec((TOP_K, tm), tok_map),
                   pl.BlockSpec((TOP_K, tm), tok_map),
                   pl.BlockSpec((1, 8, N_EXPERTS), lambda i, j: (i * tpb + j, 0, 0))],
        out_shape=[jax.ShapeDtypeStruct((b, l, d), F32),
                   jax.ShapeDtypeStruct((n, HALF_D), I32),
                   jax.ShapeDtypeStruct((TOP_K, n), I32),
                   jax.ShapeDtypeStruct((TOP_K, n), F32),
                   jax.ShapeDtypeStruct((n // tm, 8, N_EXPERTS), F32)],
        compiler_params=_cp(("arbitrary", "arbitrary")),
        name="outproj_router",
    )(x, four, attn, g1, w_out, n2g, sc2, sh2, wr_hi, wr_lo, bias)


def _dest_kernel(te_ref, base_ref, tri_ref, d_ref):
    te = te_ref[...]
    tm = te.shape[1]
    ei = lax.broadcasted_iota(I32, (N_EXPERTS, tm), 0)
    hits = [ei == te[k:k + 1, :] for k in range(TOP_K)]
    multi = hits[0].astype(F32)
    for k in range(1, TOP_K):
        multi = multi + hits[k].astype(F32)
    rank = jnp.dot(multi.astype(BF16), tri_ref[...], preferred_element_type=F32)
    pos = rank + base_ref[0]
    rows = [jnp.sum(jnp.where(hits[k], pos, 0.0), axis=0, keepdims=True) for k in range(TOP_K)]
    d_ref[...] = jnp.concatenate(rows, axis=0).astype(I32)


def _dest_rows(top_e, base, tm):
    n = top_e.shape[1]
    tri = jnp.asarray(np.triu(np.ones((tm, tm), np.float32), 1)).astype(BF16)
    return pl.pallas_call(
        _dest_kernel,
        grid=(n // tm,),
        in_specs=[pl.BlockSpec((TOP_K, tm), lambda i: (0, i)),
                  pl.BlockSpec((1, N_EXPERTS, 1), lambda i: (i, 0, 0)),
                  pl.BlockSpec((tm, tm), lambda i: (0, 0))],
        out_specs=pl.BlockSpec((TOP_K, tm), lambda i: (0, i)),
        out_shape=jax.ShapeDtypeStruct((TOP_K, n), I32),
        compiler_params=_cp(("arbitrary",)),
        name="dest_rows",
    )(top_e, base, tri)


def _scatter_kernel(d_ref, hp_ref, xz_ref, xr_ref, sem):
    del xz_ref
    tm = hp_ref.shape[0]

    def body(t, carry):
        for k in range(TOP_K):
            pltpu.make_async_copy(hp_ref.at[pl.ds(t, 1)], xr_ref.at[pl.ds(d_ref[k, t], 1)], sem).start()
        return carry

    lax.fori_loop(0, tm, body, 0)
    for k in range(TOP_K):
        pltpu.make_async_copy(hp_ref, xr_ref.at[pl.ds(0, tm)], sem).wait()


def _scatter_rows(dest, hp, n_rows, tm):
    n = hp.shape[0]
    zeros = jnp.zeros((n_rows, HALF_D), I32)
    return pl.pallas_call(
        _scatter_kernel,
        grid=(n // tm,),
        in_specs=[pl.BlockSpec((TOP_K, tm), lambda i: (0, i), memory_space=pltpu.SMEM),
                  pl.BlockSpec((tm, HALF_D), lambda i: (i, 0)),
                  pl.BlockSpec(memory_space=pl.ANY)],
        out_specs=pl.BlockSpec(memory_space=pl.ANY),
        out_shape=jax.ShapeDtypeStruct((n_rows, HALF_D), I32),
        scratch_shapes=[pltpu.SemaphoreType.DMA(())],
        input_output_aliases={2: 0},
        compiler_params=_cp(("arbitrary",)),
        name="scatter_rows",
    )(dest, hp, zeros)


def _swiglu(h_lo, h_hi, wgu, wdn):
    ag = (jnp.dot(h_lo, wgu[:HALF_D, :], preferred_element_type=F32)
          + jnp.dot(h_hi, wgu[HALF_D:, :], preferred_element_type=F32))
    ff = ag.shape[1] // 2
    a, g = ag[:, :ff], ag[:, ff:]
    mid = (a * _sigmoid(a) * g).astype(BF16)
    return jnp.dot(mid, wdn[...], preferred_element_type=F32)


def _expert_kernel(be_ref, nu_ref, x_ref, wgu_ref, wdn_ref, y_ref, wgu_s, wdn_s):
    i = pl.program_id(0)

    @pl.when(i < nu_ref[0])
    def _():
        prev = be_ref[jnp.maximum(i - 1, 0)]

        @pl.when((i == 0) | (be_ref[i] != prev))
        def _():
            wgu_s[...] = wgu_ref[0].astype(BF16)
            wdn_s[...] = wdn_ref[0].astype(BF16)

        lo, hi = _unpack_rows(x_ref[...])
        y_ref[...] = _pack_rows(_swiglu(lo.astype(BF16), hi.astype(BF16), wgu_s, wdn_s))

    @pl.when(i >= nu_ref[0])
    def _():
        y_ref[...] = jnp.zeros(y_ref.shape, I32)


def _experts(block_e, n_used, xrows, w_gu, w_down):
    n_rows = xrows.shape[0]
    rb = ROW_BLOCK
    d, ff2 = w_gu.shape[1], w_gu.shape[2]
    row_map = lambda i, be, nu: (i, 0)
    grid_spec = pltpu.PrefetchScalarGridSpec(
        num_scalar_prefetch=2,
        grid=(n_rows // rb,),
        in_specs=[pl.BlockSpec((rb, HALF_D), row_map),
                  pl.BlockSpec((1, d, ff2), lambda i, be, nu: (be[i], 0, 0)),
                  pl.BlockSpec((1, ff2 // 2, d), lambda i, be, nu: (be[i], 0, 0))],
        out_specs=pl.BlockSpec((rb, HALF_D), row_map),
        scratch_shapes=[pltpu.VMEM((d, ff2), BF16), pltpu.VMEM((ff2 // 2, d), BF16)],
    )
    return pl.pallas_call(
        _expert_kernel,
        grid_spec=grid_spec,
        out_shape=jax.ShapeDtypeStruct((n_rows, HALF_D), I32),
        compiler_params=_cp(("arbitrary",)),
        name="experts",
    )(block_e, n_used, xrows, w_gu, w_down)


def _combine_kernel(dc_ref, dn_ref, gate_ref, xn_ref, hp_ref, g2_ref, wsgu_ref, wsdn_ref, fng_ref,
                    y_hbm, o_ref, buf, sem):
    i = pl.program_id(0)
    n = pl.num_programs(0)
    tm = xn_ref.shape[0]
    slot = i % 2

    def issue(d_ref, s):
        def body(t, carry):
            for k in range(TOP_K):
                pltpu.make_async_copy(y_hbm.at[pl.ds(d_ref[k, t], 1)], buf.at[s, k, pl.ds(t, 1)],
                                      sem.at[s]).start()
            return carry
        lax.fori_loop(0, tm, body, 0)

    @pl.when(i == 0)
    def _():
        issue(dc_ref, 0)

    @pl.when(i + 1 < n)
    def _():
        issue(dn_ref, 1 - slot)

    lo, hi = _unpack_rows(hp_ref[...])
    shared = _swiglu(lo.astype(BF16), hi.astype(BF16), wsgu_ref, wsdn_ref)

    for k in range(TOP_K):
        pltpu.make_async_copy(y_hbm.at[pl.ds(0, tm)], buf.at[slot, k], sem.at[slot]).wait()
    gate = gate_ref[...]
    y_lo = jnp.zeros((tm, HALF_D), F32)
    y_hi = jnp.zeros((tm, HALF_D), F32)
    for k in range(TOP_K):
        r_lo, r_hi = _unpack_rows(buf[slot, k])
        gk = gate[:, k:k + 1]
        y_lo += gk * r_lo
        y_hi += gk * r_hi
    y = jnp.concatenate([y_lo, y_hi], axis=1) + shared
    xo = xn_ref[...] + g2_ref[0] * y
    ms = jnp.mean(xo * xo, axis=-1, keepdims=True)
    o_ref[...] = xo * lax.rsqrt(ms + RMS_EPS) * fng_ref[...]


def _combine(dest, gate_t, xn, hp, g2, ws_gu, ws_dn, fng, yrows, tokens_per_batch):
    n, d = xn.shape
    tm = min(COMB_TILE, tokens_per_batch)
    nt = n // tm
    tpb = tokens_per_batch // tm
    const2 = lambda i: (0, 0)
    return pl.pallas_call(
        _combine_kernel,
        grid=(nt,),
        in_specs=[pl.BlockSpec((TOP_K, tm), lambda i: (0, i), memory_space=pltpu.SMEM),
                  pl.BlockSpec((TOP_K, tm), lambda i: (0, jnp.minimum(i + 1, nt - 1)),
                               memory_space=pltpu.SMEM),
                  pl.BlockSpec((tm, TOP_K), lambda i: (i, 0)),
                  pl.BlockSpec((tm, d), lambda i: (i, 0)),
                  pl.BlockSpec((tm, HALF_D), lambda i: (i, 0)),
                  pl.BlockSpec((1, 1, d), lambda i: (i // tpb, 0, 0)),
                  pl.BlockSpec(ws_gu.shape, const2),
                  pl.BlockSpec(ws_dn.shape, const2),
                  pl.BlockSpec((1, d), const2),
                  pl.BlockSpec(memory_space=pl.ANY)],
        out_specs=pl.BlockSpec((tm, d), lambda i: (i, 0)),
        out_shape=jax.ShapeDtypeStruct((n, d), F32),
        scratch_shapes=[pltpu.VMEM((2, TOP_K, tm, HALF_D), I32), pltpu.SemaphoreType.DMA((2,))],
        compiler_params=_cp(("arbitrary",)),
        name="combine",
    )(dest, dest, gate_t, xn, hp, g2, ws_gu, ws_dn, fng, yrows)


def _rope_tables(l):
    rows = l // GRID_W
    row = np.repeat(np.arange(rows, dtype=np.float32), GRID_W)
    col = np.tile(np.arange(GRID_W, dtype=np.float32), rows)
    n_freq = AXIS_ROT // 2
    inv_freq = (np.float32(ROPE_THETA) ** (-np.arange(n_freq, dtype=np.float32) / n_freq)).astype(np.float32)
    ang_r = row[:, None] * inv_freq
    ang_c = col[:, None] * inv_freq
    ang = np.concatenate([ang_r, ang_r, ang_c, ang_c], axis=-1).astype(np.float64)
    cos, sin = np.cos(ang), np.sin(ang)
    lane = np.arange(HEAD_DIM) % AXIS_ROT
    first = lane < AXIS_ROT // 2
    sin_a = np.where(first[None, :], -sin, 0.0)
    sin_b = np.where(first[None, :], 0.0, sin)
    two = lambda a: jnp.asarray(np.concatenate([a, a], axis=1), dtype=F32)
    return two(cos), two(sin_a), two(sin_b)


def _layer(x, ctx, c, c_ctx, lw, moe_w, fng):
    norm1_g, w_ada, b_ada, w_in, w_fourier, q_norm_g, k_norm_g, w_out, norm2_g = lw
    w_router, router_bias, w_expert_gu, w_expert_down, w_shared_gu, w_shared_down = moe_w
    b, l, d = x.shape
    n = b * l

    cond = jnp.concatenate([c, c_ctx[None, :], jnp.zeros((8 - b - 1, d), F32)], axis=0)
    mod = _adaln(cond, w_ada, b_ada)
    sh1, sc1, g1, sh2, sc2, g2 = [m[:b, None, :] for m in jnp.split(mod, 6, axis=-1)]
    csh1, csc1 = [m[b:b + 1, None, :] for m in jnp.split(mod, 6, axis=-1)[:2]]

    w_in_bf = w_in.astype(BF16)
    two = lambda g, s: jnp.tile(g * s, 2).reshape(1, LANES)
    gq, gk = two(q_norm_g, ATTN_SCALE * float(np.log2(np.e))), two(k_norm_g, 1.0)
    ones_bd = jnp.asarray(np.kron(np.eye(2), np.ones((HEAD_DIM, HEAD_DIM))), dtype=F32).astype(BF16)
    n1g = norm1_g.reshape(1, d)

    kc, vct = _inproj(ctx, csc1, csh1, n1g, w_in_bf[:, KV_COL0:], gq, gk, ones_bd, None, False)
    u, qt, k, vt = _inproj(x, sc1, sh1, n1g, w_in_bf, gq, gk, ones_bd, _rope_tables(l), True)

    w_bd = jnp.zeros((FOURIER_WIDTH, FOURIER_WIDTH), F32)
    for h in range(FOURIER_HEADS):
        w_bd = lax.dynamic_update_slice(w_bd, w_fourier[h], (h * HEAD_DIM, h * HEAD_DIM))
    four = _fourier(u, w_bd.astype(BF16))
    attn = _attention(qt, k, vt, kc, vct)

    wr_t = w_router.T
    wr_hi = wr_t.astype(BF16)
    wr_lo = (wr_t - wr_hi.astype(F32)).astype(BF16)
    xn, hp, top_e, gate, cnt = _outproj_router(
        x, four, attn, g1, w_out.astype(BF16), norm2_g.reshape(1, d), sc2, sh2,
        wr_hi, wr_lo, router_bias.reshape(N_EXPERTS, 1))

    tm = min(TOK_TILE, l)
    rb = ROW_BLOCK
    counts = cnt[:, 0, :].astype(I32)
    total = jnp.sum(counts, axis=0)
    padded = (total + rb - 1) // rb * rb
    pad_end = jnp.cumsum(padded)
    pad_start = pad_end - padded
    base = pad_start[None, :] + jnp.cumsum(counts, axis=0) - counts
    n_blocks = n * TOP_K // rb + N_EXPERTS
    n_used = (pad_end[-1] // rb).astype(I32)
    blk = jnp.minimum(jnp.arange(n_blocks, dtype=I32), n_used - 1)
    block_e = jnp.minimum(jnp.searchsorted(pad_end, blk * rb, side="right"), N_EXPERTS - 1).astype(I32)

    dest = _dest_rows(top_e, base.astype(F32)[:, :, None], tm)
    xrows = _scatter_rows(dest, hp, n_blocks * rb, tm)
    yrows = _experts(block_e, n_used.reshape(1), xrows, w_expert_gu, w_expert_down)
    return _combine(dest, gate.T, xn.reshape(n, d), hp, g2, w_shared_gu.astype(BF16),
                    w_shared_down.astype(BF16), fng, yrows, l)


def kernel(x, c, ctx, c_ctx, norm1_g, w_ada, b_ada, w_in, w_fourier, q_norm_g, k_norm_g, w_out, norm2_g,
           w_router, router_bias, w_expert_gu, w_expert_down, w_shared_gu, w_shared_down, final_norm_g):
    depth = norm1_g.shape[0]
    assert depth == 1, "context update between layers is not implemented"
    b, l, d = x.shape
    lw = (norm1_g[0], w_ada[0], b_ada[0], w_in[0], w_fourier[0], q_norm_g[0], k_norm_g[0], w_out[0], norm2_g[0])
    moe_w = (w_router[0], router_bias[0], w_expert_gu[0], w_expert_down[0], w_shared_gu[0], w_shared_down[0])
    out = _layer(x, ctx, c, c_ctx, lw, moe_w, final_norm_g.reshape(1, d))
    return out.reshape(b, l, d)
```

```python
import functools

import numpy as np
import jax
import jax.numpy as jnp
from jax import lax
from jax.experimental import pallas as pl
from jax.experimental.pallas import tpu as pltpu
from jax.experimental.pallas import tpu_sc as plsc

F32 = jnp.float32
BF16 = jnp.bfloat16
I32 = jnp.int32

D_MODEL = 1024
GRID_W = 64
HEAD_DIM = 64
FOURIER_HEADS = 4
FOURIER_WIDTH = 256
ATTN_HEADS = 12
KV_HEADS = 4
Q_PER_KV = 3
ATTN_WIDTH = 768
KV_WIDTH = 256
KV_COL0 = 1024
IN_WIDTH = 1536
ATTN_SCALE = HEAD_DIM ** -0.5
ROPE_THETA = 10000.0
AXIS_ROT = HEAD_DIM // 2
N_EXPERTS = 256
TOP_K = 8
N_EXPERT_GROUPS = 8
TOPK_GROUPS = 4
EXPERTS_PER_GROUP = 32
EXPERT_FF = 256
ROUTED_SCALE = 2.5
RMS_EPS = 1e-6

LANES = 128
HALF_D = D_MODEL // 2
DFT_L1 = 64
VMEM_LIMIT = 48 * 1024 * 1024

TOK_TILE = 512
ATTN_BQ = 256
ATTN_BK = 1024
ONES_ROWS = 16
ROW_BLOCK = 256
COMB_TILE = 256
SC_WINDOW = 128
SC_SPLIT = 2

_HI = lax.Precision.HIGHEST
_NT = (((1,), (1,)), ((), ()))


def _cp(sem, vmem=VMEM_LIMIT):
    return pltpu.CompilerParams(dimension_semantics=sem, vmem_limit_bytes=vmem)


def _sigmoid(v):
    return 1.0 / (1.0 + jnp.exp(-v))


def _pack_rows(v):
    half = v.shape[1] // 2
    lo = lax.bitcast_convert_type(v[:, :half].astype(BF16).astype(F32), I32)
    hi = lax.bitcast_convert_type(v[:, half:].astype(BF16).astype(F32), I32)
    return lax.shift_right_logical(lo, 16) | (hi & jnp.int32(-65536))


def _unpack_rows(w):
    lo = lax.bitcast_convert_type(lax.shift_left(w, 16), F32)
    hi = lax.bitcast_convert_type(w & jnp.int32(-65536), F32)
    return lo, hi


def _adaln_kernel(c_ref, w_ref, b_ref, o_ref):
    c = c_ref[...]
    s = c * _sigmoid(c)
    o_ref[...] = jnp.dot(s, w_ref[...], precision=_HI, preferred_element_type=F32) + b_ref[...]


def _adaln(cond, w_ada, b_ada):
    rows, d = cond.shape
    n = w_ada.shape[1]
    tn = 1536
    return pl.pallas_call(
        _adaln_kernel,
        grid=(n // tn,),
        in_specs=[pl.BlockSpec((rows, d), lambda j: (0, 0)),
                  pl.BlockSpec((d, tn), lambda j: (0, j)),
                  pl.BlockSpec((1, tn), lambda j: (0, j))],
        out_specs=pl.BlockSpec((rows, tn), lambda j: (0, j)),
        out_shape=jax.ShapeDtypeStruct((rows, n), F32),
        compiler_params=_cp(("arbitrary",)),
        name="adaln",
    )(cond, w_ada, b_ada.reshape(1, n))


def _modulated_norm(x, g, sc, sh):
    ms = jnp.mean(x * x, axis=-1, keepdims=True)
    return x * lax.rsqrt(ms + RMS_EPS) * g * (1.0 + sc) + sh


def _head_norm(chunk, gain, ones_bd):
    sq = chunk * chunk
    hi = sq.astype(BF16)
    lo = (sq - hi.astype(F32)).astype(BF16)
    ss = (jnp.dot(hi, ones_bd, preferred_element_type=F32)
          + jnp.dot(lo, ones_bd, preferred_element_type=F32))
    return chunk * lax.rsqrt(ss * (1.0 / HEAD_DIM) + RMS_EPS) * gain


def _rope(t, cos, sin_a, sin_b):
    return (t * cos + pltpu.roll(t, LANES - AXIS_ROT // 2, 1) * sin_a
            + pltpu.roll(t, AXIS_ROT // 2, 1) * sin_b)


def _inproj_kernel(*refs, rope, with_uq):
    if rope:
        (x_ref, sc_ref, sh_ref, g_ref, w_ref, gq_ref, gk_ref, ones_ref,
         cos_ref, sa_ref, sb_ref) = refs[:11]
        outs = refs[11:]
    else:
        x_ref, sc_ref, sh_ref, g_ref, w_ref, gq_ref, gk_ref, ones_ref = refs[:8]
        outs = refs[8:]
    if with_uq:
        u_ref, qt_ref, k_ref, vt_ref = outs
    else:
        k_ref, vt_ref = outs

    h = _modulated_norm(x_ref[0], g_ref[...], sc_ref[0], sh_ref[0])
    z = jnp.dot(h.astype(BF16), w_ref[...], preferred_element_type=F32)
    ones_bd = ones_ref[...]

    def normed(chunk, gain):
        t = _head_norm(chunk, gain, ones_bd)
        if rope:
            t = _rope(t, cos_ref[...], sa_ref[...], sb_ref[...])
        return t

    col = 0
    if with_uq:
        u_ref[0] = z[:, :FOURIER_WIDTH].astype(BF16)
        col = FOURIER_WIDTH
        qs = [normed(z[:, col + j * LANES: col + (j + 1) * LANES], gq_ref[...])
              for j in range(ATTN_WIDTH // LANES)]
        qt_ref[0] = jnp.concatenate(qs, axis=1).T.astype(BF16)
        col += ATTN_WIDTH
    for j in range(KV_WIDTH // LANES):
        t = normed(z[:, col + j * LANES: col + (j + 1) * LANES], gk_ref[...])
        k_ref[0, 2 * j] = t[:, :HEAD_DIM].astype(BF16)
        k_ref[0, 2 * j + 1] = t[:, HEAD_DIM:].astype(BF16)
    col += KV_WIDTH
    vt_ref[0] = z[:, col:col + KV_WIDTH].T.astype(BF16)


def _inproj(x, sc, sh, g, w, gq, gk, ones_bd, rope_tabs, with_uq):
    b, t, d = x.shape
    tm = min(TOK_TILE, t)
    wcols = w.shape[1]
    bm = sc.shape[0]
    mod_map = (lambda i, j: (i, 0, 0)) if bm == b else (lambda i, j: (0, 0, 0))
    const2 = lambda i, j: (0, 0)
    in_specs = [pl.BlockSpec((1, tm, d), lambda i, j: (i, j, 0)),
                pl.BlockSpec((1, 1, d), mod_map),
                pl.BlockSpec((1, 1, d), mod_map),
                pl.BlockSpec((1, d), const2),
                pl.BlockSpec((d, wcols), const2),
                pl.BlockSpec((1, LANES), const2),
                pl.BlockSpec((1, LANES), const2),
                pl.BlockSpec((LANES, LANES), const2)]
    args = [x, sc, sh, g, w, gq, gk, ones_bd]
    rope = rope_tabs is not None
    if rope:
        in_specs += [pl.BlockSpec((tm, LANES), lambda i, j: (j, 0))] * 3
        args += list(rope_tabs)
    out_specs, out_shape = [], []
    if with_uq:
        out_specs += [pl.BlockSpec((1, tm, FOURIER_WIDTH), lambda i, j: (i, j, 0)),
                      pl.BlockSpec((1, ATTN_WIDTH, tm), lambda i, j: (i, 0, j))]
        out_shape += [jax.ShapeDtypeStruct((b, t, FOURIER_WIDTH), BF16),
                      jax.ShapeDtypeStruct((b, ATTN_WIDTH, t), BF16)]
    out_specs += [pl.BlockSpec((1, KV_HEADS, tm, HEAD_DIM), lambda i, j: (i, 0, j, 0)),
                  pl.BlockSpec((1, KV_WIDTH, tm), lambda i, j: (i, 0, j))]
    out_shape += [jax.ShapeDtypeStruct((b, KV_HEADS, t, HEAD_DIM), BF16),
                  jax.ShapeDtypeStruct((b, KV_WIDTH, t), BF16)]
    return pl.pallas_call(
        functools.partial(_inproj_kernel, rope=rope, with_uq=with_uq),
        grid=(b, t // tm),
        in_specs=in_specs,
        out_specs=out_specs,
        out_shape=out_shape,
        compiler_params=_cp(("arbitrary", "arbitrary")),
        name="inproj_latent" if with_uq else "inproj_ctx",
    )(*args)


def _fourier_a_kernel(u_ref, c_ref, s_ref, yr_ref, yi_ref):
    u = u_ref[0]
    yr_ref[0] = jnp.dot(c_ref[...], u, preferred_element_type=F32).astype(BF16)
    yi_ref[0] = jnp.dot(s_ref[...], u, preferred_element_type=F32).astype(BF16)


def _fourier_b_kernel(yr_ref, yi_ref, m_ref, c_ref, s_ref, w_ref, o_ref, *, kb):
    y = jnp.concatenate([yr_ref[0], yi_ref[0]], axis=1)
    x = jnp.einsum("kab,kbc->kac", m_ref[...], y, preferred_element_type=F32)
    xr = x[:, :DFT_L1].reshape(kb * DFT_L1, FOURIER_WIDTH).astype(BF16)
    xi = x[:, DFT_L1:].reshape(kb * DFT_L1, FOURIER_WIDTH).astype(BF16)
    spec = (jnp.dot(xr, c_ref[...], preferred_element_type=F32)
            + jnp.dot(xi, s_ref[...], preferred_element_type=F32))
    o = jnp.dot(spec.astype(BF16), w_ref[...], preferred_element_type=F32)
    for j in range(kb):
        o_ref[0, :, j, :] = o[j * DFT_L1:(j + 1) * DFT_L1]


def _dft_tables(l):
    l2 = l // DFT_L1
    k2 = np.arange(l2)
    ang2 = 2.0 * np.pi * ((k2[:, None] * k2[None, :]) % l2) / l2
    c2 = np.cos(ang2)
    s2n = -np.sin(ang2)
    n1 = np.arange(DFT_L1)
    k = l2 * n1[None, :, None] + k2[:, None, None]
    ang = 2.0 * np.pi * ((k * n1[None, None, :]) % l) / l
    mr, mi = np.cos(ang), -np.sin(ang)
    m = np.concatenate([np.concatenate([mr, -mi], axis=2),
                        np.concatenate([mi, mr], axis=2)], axis=1)
    d = np.arange(HEAD_DIM)
    angc = 2.0 * np.pi * ((d[:, None] * d[None, :]) % HEAD_DIM) / HEAD_DIM
    scale = 1.0 / np.sqrt(float(l) * HEAD_DIM)
    eye = np.eye(FOURIER_HEADS)
    cbd = np.kron(eye, np.cos(angc) * scale)
    sbd = np.kron(eye, np.sin(angc) * scale)
    as_bf = lambda a: jnp.asarray(a, dtype=F32).astype(BF16)
    return as_bf(c2), as_bf(s2n), as_bf(m), as_bf(cbd), as_bf(sbd)


def _fourier(u, w_bd):
    b, l, fw = u.shape
    l2 = l // DFT_L1
    ncol = DFT_L1 * fw
    tn = min(4096, ncol)
    kb = min(8, l2)
    c2, s2n, m, cbd, sbd = _dft_tables(l)
    yr, yi = pl.pallas_call(
        _fourier_a_kernel,
        grid=(b, ncol // tn),
        in_specs=[pl.BlockSpec((1, l2, tn), lambda i, j: (i, 0, j)),
                  pl.BlockSpec((l2, l2), lambda i, j: (0, 0)),
                  pl.BlockSpec((l2, l2), lambda i, j: (0, 0))],
        out_specs=[pl.BlockSpec((1, l2, tn), lambda i, j: (i, 0, j))] * 2,
        out_shape=[jax.ShapeDtypeStruct((b, l2, ncol), BF16)] * 2,
        compiler_params=_cp(("arbitrary", "arbitrary")),
        name="fourier_a",
    )(u.reshape(b, l2, ncol), c2, s2n)
    yr = yr.reshape(b, l2, DFT_L1, fw)
    yi = yi.reshape(b, l2, DFT_L1, fw)
    out = pl.pallas_call(
        functools.partial(_fourier_b_kernel, kb=kb),
        grid=(b, l2 // kb),
        in_specs=[pl.BlockSpec((1, kb, DFT_L1, fw), lambda i, j: (i, j, 0, 0)),
                  pl.BlockSpec((1, kb, DFT_L1, fw), lambda i, j: (i, j, 0, 0)),
                  pl.BlockSpec((kb, 2 * DFT_L1, 2 * DFT_L1), lambda i, j: (j, 0, 0)),
                  pl.BlockSpec((fw, fw), lambda i, j: (0, 0)),
                  pl.BlockSpec((fw, fw), lambda i, j: (0, 0)),
                  pl.BlockSpec((fw, fw), lambda i, j: (0, 0))],
        out_specs=pl.BlockSpec((1, DFT_L1, kb, fw), lambda i, j: (i, 0, j, 0)),
        out_shape=jax.ShapeDtypeStruct((b, DFT_L1, l2, fw), F32),
        compiler_params=_cp(("arbitrary", "arbitrary")),
        name="fourier_b",
    )(yr, yi, m, cbd, sbd, w_bd)
    return out.reshape(b, l, fw)


def _attn_kernel(qt_ref, k_ref, vt_ref, kc_ref, vct_ref, o_ref, m_scr, acc_scr, *s_scrs, bk, nchunks):
    bq = qt_ref.shape[2]
    m_scr[...] = jnp.full(m_scr.shape, -jnp.inf, F32)
    acc_scr[...] = jnp.zeros(acc_scr.shape, F32)

    def scores(k, buf, h):
        nk = k.shape[0]
        qt = qt_ref[0, h * HEAD_DIM:(h + 1) * HEAD_DIM, :]
        s_scrs[buf][:nk, :] = jnp.dot(k, qt, preferred_element_type=F32)

    def softmax_pv(buf, h, vt):
        nk = vt.shape[1]
        s = s_scrs[buf][:nk, :]
        m_old = m_scr[h]
        m_new = jnp.maximum(m_old, jnp.max(s, axis=0, keepdims=True))
        alpha = jnp.exp2(m_old - m_new)
        p = jnp.exp2((s - m_new).astype(BF16))
        vt_ones = jnp.concatenate([vt, jnp.ones((ONES_ROWS, nk), BF16)], axis=0)
        acc_scr[h] = alpha * acc_scr[h] + jnp.dot(vt_ones, p, preferred_element_type=F32)
        m_scr[h] = m_new

    def keys(c):
        if isinstance(c, int) and c == nchunks:
            return kc_ref[0, 0]
        return k_ref[0, 0, pl.ds(pl.multiple_of(c * bk, bk), bk), :]

    def values(c):
        if isinstance(c, int) and c == nchunks:
            return vct_ref[0]
        return vt_ref[0, :, pl.ds(pl.multiple_of(c * bk, bk), bk)]

    nbuf = len(s_scrs)
    pair = 1
    ahead = nbuf - pair
    n_units = Q_PER_KV * (nchunks + 1)

    def group(us, chunk0=0):
        for u in us:
            ua = u + ahead
            if not (isinstance(chunk0, int) and ua >= n_units):
                scores(keys(chunk0 + ua // Q_PER_KV), ua % nbuf, ua % Q_PER_KV)
        for u in us:
            softmax_pv(u % nbuf, u % Q_PER_KV, values(chunk0 + u // Q_PER_KV))

    for u in range(min(ahead, n_units)):
        scores(keys(u // Q_PER_KV), u % nbuf, u % Q_PER_KV)

    chunks_per_iter = nbuf // Q_PER_KV
    n_iter = max(0, (Q_PER_KV * nchunks - ahead) // nbuf)

    def body(j, carry):
        for t in range(0, nbuf, pair):
            group(range(t, t + pair), j * chunks_per_iter)
        return carry

    lax.fori_loop(0, n_iter, body, 0)
    for t in range(n_iter * nbuf, n_units, pair):
        group(range(t, min(t + pair, n_units)))
    outs = [acc_scr[h, :HEAD_DIM, :] / acc_scr[h, HEAD_DIM:HEAD_DIM + 1, :] for h in range(Q_PER_KV)]
    o_t = jnp.concatenate(outs + [jnp.zeros((HEAD_DIM, bq), F32)], axis=0)
    o_ref[0, 0] = o_t.T[:, :Q_PER_KV * HEAD_DIM].astype(BF16)


def _attention(qt, k, vt, kc, vct):
    b, _, l = qt.shape
    c = kc.shape[2]
    bq = min(ATTN_BQ, l)
    bk = min(ATTN_BK, l)
    gw = Q_PER_KV * HEAD_DIM
    return pl.pallas_call(
        functools.partial(_attn_kernel, bk=bk, nchunks=l // bk),
        grid=(b, KV_HEADS, l // bq),
        in_specs=[pl.BlockSpec((1, gw, bq), lambda i, g, j: (i, g, j)),
                  pl.BlockSpec((1, 1, l, HEAD_DIM), lambda i, g, j: (i, g, 0, 0)),
                  pl.BlockSpec((1, HEAD_DIM, l), lambda i, g, j: (i, g, 0)),
                  pl.BlockSpec((1, 1, c, HEAD_DIM), lambda i, g, j: (i, g, 0, 0)),
                  pl.BlockSpec((1, HEAD_DIM, c), lambda i, g, j: (i, g, 0))],
        out_specs=pl.BlockSpec((1, 1, bq, gw), lambda i, g, j: (i, g, j, 0)),
        out_shape=jax.ShapeDtypeStruct((b, KV_HEADS, l, gw), BF16),
        scratch_shapes=[pltpu.VMEM((Q_PER_KV, 1, bq), F32),
                        pltpu.VMEM((Q_PER_KV, HEAD_DIM + ONES_ROWS, bq), F32),
                        *[pltpu.VMEM((max(bk, c), bq), F32)] * (2 * Q_PER_KV)],
        compiler_params=_cp(("arbitrary", "arbitrary", "arbitrary")),
        name="attention",
    )(qt, k, vt, kc, vct)


def _outproj_router_kernel(x_ref, f_ref, a_ref, g1_ref, wo_ref, n2g_ref, sc_ref, sh_ref,
                           wrh_ref, wrl_ref, bias_ref, xn_ref, hp_ref, te_ref, gt_ref, cnt_ref):
    tm = x_ref.shape[1]
    mix = jnp.dot(f_ref[0].astype(BF16), wo_ref[:FOURIER_WIDTH, :], preferred_element_type=F32)
    gw = Q_PER_KV * HEAD_DIM
    for g in range(KV_HEADS):
        r0 = FOURIER_WIDTH + g * gw
        mix += jnp.dot(a_ref[0, g], wo_ref[r0:r0 + gw, :], preferred_element_type=F32)
    xn = x_ref[0] + g1_ref[0] * mix
    xn_ref[0] = xn
    h2 = _modulated_norm(xn, n2g_ref[...], sc_ref[0], sh_ref[0])
    h2_hi = h2.astype(BF16)
    h2_lo = (h2 - h2_hi.astype(F32)).astype(BF16)
    hp_ref[...] = _pack_rows(h2)

    wrh = wrh_ref[...]
    logits = (lax.dot_general(wrh, h2_hi, _NT, preferred_element_type=F32)
              + lax.dot_general(wrh, h2_lo, _NT, preferred_element_type=F32)
              + lax.dot_general(wrl_ref[...], h2_hi, _NT, preferred_element_type=F32))
    scores = _sigmoid(logits)
    sel = scores + bias_ref[...]

    neg = jnp.float32(-jnp.inf)
    s3 = sel.reshape(N_EXPERT_GROUPS, EXPERTS_PER_GROUP, tm)
    i3 = lax.broadcasted_iota(I32, s3.shape, 1)
    m1 = jnp.max(s3, axis=1, keepdims=True)
    i1 = jnp.min(jnp.where(s3 == m1, i3, EXPERTS_PER_GROUP), axis=1, keepdims=True)
    m2 = jnp.max(jnp.where(i3 == i1, neg, s3), axis=1)
    gs = m1[:, 0, :] + m2
    gi = lax.broadcasted_iota(I32, gs.shape, 0)
    keep = jnp.zeros(gs.shape, jnp.bool_)
    for _ in range(TOPK_GROUPS):
        m = jnp.max(gs, axis=0, keepdims=True)
        idx = jnp.min(jnp.where(gs == m, gi, N_EXPERT_GROUPS), axis=0, keepdims=True)
        hit = gi == idx
        keep = keep | hit
        gs = jnp.where(hit, neg, gs)
    keep3 = jnp.broadcast_to(keep[:, None, :], s3.shape)
    selm = jnp.where(keep3, s3, neg).reshape(N_EXPERTS, tm)

    ei = lax.broadcasted_iota(I32, selm.shape, 0)
    multi = jnp.zeros(selm.shape, F32)
    idxs, gates = [], []
    for _ in range(TOP_K):
        m = jnp.max(selm, axis=0, keepdims=True)
        idx = jnp.min(jnp.where(selm == m, ei, N_EXPERTS), axis=0, keepdims=True)
        hit = ei == idx
        gates.append(jnp.sum(jnp.where(hit, scores, 0.0), axis=0, keepdims=True))
        idxs.append(idx)
        selm = jnp.where(hit, neg, selm)
        multi = multi + hit.astype(F32)
    gate = jnp.concatenate(gates, axis=0)
    gate = gate / jnp.sum(gate, axis=0, keepdims=True) * ROUTED_SCALE
    te_ref[...] = jnp.concatenate(idxs, axis=0)
    gt_ref[...] = gate
    ones = jnp.ones((8, tm), BF16)
    cnt_ref[0] = lax.dot_general(ones, multi.astype(BF16), _NT, preferred_element_type=F32)


def _outproj_router(x, four, attn, g1, w_out, n2g, sc2, sh2, wr_hi, wr_lo, bias):
    b, l, d = x.shape
    tm = min(TOK_TILE, l)
    tpb = l // tm
    n = b * l
    const2 = lambda i, j: (0, 0)
    mod_map = lambda i, j: (i, 0, 0)
    tok_map = lambda i, j: (0, i * tpb + j)
    return pl.pallas_call(
        _outproj_router_kernel,
        grid=(b, tpb),
        in_specs=[pl.BlockSpec((1, tm, d), lambda i, j: (i, j, 0)),
                  pl.BlockSpec((1, tm, FOURIER_WIDTH), lambda i, j: (i, j, 0)),
                  pl.BlockSpec((1, KV_HEADS, tm, Q_PER_KV * HEAD_DIM), lambda i, j: (i, 0, j, 0)),
                  pl.BlockSpec((1, 1, d), mod_map),
                  pl.BlockSpec((d, d), const2),
                  pl.BlockSpec((1, d), const2),
                  pl.BlockSpec((1, 1, d), mod_map),
                  pl.BlockSpec((1, 1, d), mod_map),
                  pl.BlockSpec((N_EXPERTS, d), const2),
                  pl.BlockSpec((N_EXPERTS, d), const2),
                  pl.BlockSpec((N_EXPERTS, 1), const2)],
        out_specs=[pl.BlockSpec((1, tm, d), lambda i, j: (i, j, 0)),
                   pl.BlockSpec((tm, HALF_D), lambda i, j: (i * tpb + j, 0)),
                   pl.BlockSpec((TOP_K, tm), tok_map),
                   pl.BlockSpec((TOP_K, tm), tok_map),
                   pl.BlockSpec((1, 8, N_EXPERTS), lambda i, j: (i * tpb + j, 0, 0))],
        out_shape=[jax.ShapeDtypeStruct((b, l, d), F32),
                   jax.ShapeDtypeStruct((n, HALF_D), I32),
                   jax.ShapeDtypeStruct((TOP_K, n), I32),
                   jax.ShapeDtypeStruct((TOP_K, n), F32),
                   jax.ShapeDtypeStruct((n // tm, 8, N_EXPERTS), F32)],
        compiler_params=_cp(("arbitrary", "arbitrary")),
        name="outproj_router",
    )(x, four, attn, g1, w_out, n2g, sc2, sh2, wr_hi, wr_lo, bias)


def _dest_kernel(te_ref, base_ref, tri_ref, d_ref):
    te = te_ref[...]
    tm = te.shape[1]
    ei = lax.broadcasted_iota(I32, (N_EXPERTS, tm), 0)
    hits = [ei == te[k:k + 1, :] for k in range(TOP_K)]
    multi = hits[0].astype(F32)
    for k in range(1, TOP_K):
        multi = multi + hits[k].astype(F32)
    rank = jnp.dot(multi.astype(BF16), tri_ref[...], preferred_element_type=F32)
    pos = rank + base_ref[0]
    rows = [jnp.sum(jnp.where(hits[k], pos, 0.0), axis=0, keepdims=True) for k in range(TOP_K)]
    d_ref[...] = jnp.concatenate(rows, axis=0).astype(I32)


def _dest_rows(top_e, base, tm):
    n = top_e.shape[1]
    tri = jnp.asarray(np.triu(np.ones((tm, tm), np.float32), 1)).astype(BF16)
    return pl.pallas_call(
        _dest_kernel,
        grid=(n // tm,),
        in_specs=[pl.BlockSpec((TOP_K, tm), lambda i: (0, i)),
                  pl.BlockSpec((1, N_EXPERTS, 1), lambda i: (i, 0, 0)),
                  pl.BlockSpec((tm, tm), lambda i: (0, 0))],
        out_specs=pl.BlockSpec((TOP_K, tm), lambda i: (0, i)),
        out_shape=jax.ShapeDtypeStruct((TOP_K, n), I32),
        compiler_params=_cp(("arbitrary",)),
        name="dest_rows",
    )(top_e, base, tri)


def _scatter_kernel(d_ref, hp_ref, xz_ref, xr_ref, sem):
    del xz_ref
    tm = hp_ref.shape[0]

    def body(t, carry):
        for k in range(TOP_K):
            pltpu.make_async_copy(hp_ref.at[pl.ds(t, 1)], xr_ref.at[pl.ds(d_ref[k, t], 1)], sem).start()
        return carry

    lax.fori_loop(0, tm, body, 0)
    for k in range(TOP_K):
        pltpu.make_async_copy(hp_ref, xr_ref.at[pl.ds(0, tm)], sem).wait()


def _scatter_rows(dest, hp, n_rows, tm):
    n = hp.shape[0]
    zeros = jnp.zeros((n_rows, HALF_D), I32)
    return pl.pallas_call(
        _scatter_kernel,
        grid=(n // tm,),
        in_specs=[pl.BlockSpec((TOP_K, tm), lambda i: (0, i), memory_space=pltpu.SMEM),
                  pl.BlockSpec((tm, HALF_D), lambda i: (i, 0)),
                  pl.BlockSpec(memory_space=pl.ANY)],
        out_specs=pl.BlockSpec(memory_space=pl.ANY),
        out_shape=jax.ShapeDtypeStruct((n_rows, HALF_D), I32),
        scratch_shapes=[pltpu.SemaphoreType.DMA(())],
        input_output_aliases={2: 0},
        compiler_params=_cp(("arbitrary",)),
        name="scatter_rows",
    )(dest, hp, zeros)


def _swiglu(h_lo, h_hi, wgu, wdn):
    ag = (jnp.dot(h_lo, wgu[:HALF_D, :], preferred_element_type=F32)
          + jnp.dot(h_hi, wgu[HALF_D:, :], preferred_element_type=F32))
    ff = ag.shape[1] // 2
    a, g = ag[:, :ff], ag[:, ff:]
    mid = (a * _sigmoid(a) * g).astype(BF16)
    return jnp.dot(mid, wdn[...], preferred_element_type=F32)


def _expert_kernel(be_ref, nu_ref, x_ref, wgu_ref, wdn_ref, y_ref, wgu_s, wdn_s):
    i = pl.program_id(0)

    @pl.when(i < nu_ref[0])
    def _():
        prev = be_ref[jnp.maximum(i - 1, 0)]

        @pl.when((i == 0) | (be_ref[i] != prev))
        def _():
            wgu_s[...] = wgu_ref[0].astype(BF16)
            wdn_s[...] = wdn_ref[0].astype(BF16)

        lo, hi = _unpack_rows(x_ref[...])
        y_ref[...] = _pack_rows(_swiglu(lo.astype(BF16), hi.astype(BF16), wgu_s, wdn_s))

    @pl.when(i >= nu_ref[0])
    def _():
        y_ref[...] = jnp.zeros(y_ref.shape, I32)


def _experts(block_e, n_used, xrows, w_gu, w_down):
    n_rows = xrows.shape[0]
    rb = ROW_BLOCK
    d, ff2 = w_gu.shape[1], w_gu.shape[2]
    row_map = lambda i, be, nu: (i, 0)
    grid_spec = pltpu.PrefetchScalarGridSpec(
        num_scalar_prefetch=2,
        grid=(n_rows // rb,),
        in_specs=[pl.BlockSpec((rb, HALF_D), row_map),
                  pl.BlockSpec((1, d, ff2), lambda i, be, nu: (be[i], 0, 0)),
                  pl.BlockSpec((1, ff2 // 2, d), lambda i, be, nu: (be[i], 0, 0))],
        out_specs=pl.BlockSpec((rb, HALF_D), row_map),
        scratch_shapes=[pltpu.VMEM((d, ff2), BF16), pltpu.VMEM((ff2 // 2, d), BF16)],
    )
    return pl.pallas_call(
        _expert_kernel,
        grid_spec=grid_spec,
        out_shape=jax.ShapeDtypeStruct((n_rows, HALF_D), I32),
        compiler_params=_cp(("arbitrary",)),
        name="experts",
    )(block_e, n_used, xrows, w_gu, w_down)


def _sc_gather_rows(table, idx):
    rows_out, width = idx.shape[0], table.shape[1]
    w = width // SC_SPLIT
    table = table.reshape(table.shape[0] * SC_SPLIT, w)
    idx = (idx[:, None] * SC_SPLIT + jnp.arange(SC_SPLIT, dtype=I32)[None, :]).reshape(-1)
    m = idx.shape[0]
    mesh = plsc.VectorSubcoreMesh(core_axis_name="c", subcore_axis_name="s")

    @pl.kernel(out_type=jax.ShapeDtypeStruct((m, w), table.dtype), mesh=mesh, name="sc_gather_rows")
    def gather(t_hbm, i_hbm, o_hbm):
        def body(i_vmem, o_vmem):
            pltpu.sync_copy(t_hbm.at[i_vmem.at[0]], o_vmem)

        pltpu.emit_pipeline(
            body,
            grid=(m // SC_WINDOW,),
            in_specs=[pl.BlockSpec((1, SC_WINDOW), lambda i: (0, i))],
            out_specs=[pl.BlockSpec((SC_WINDOW, w), lambda i: (i, 0))],
            core_axis_name=("c", "s"),
            dimension_semantics=(pltpu.PARALLEL,),
        )(i_hbm, o_hbm)

    return gather(table, idx.reshape(1, m)).reshape(rows_out, width)


def _combine_kernel(gate_ref, xn_ref, hp_ref, g2_ref, wsgu_ref, wsdn_ref, fng_ref, yg_ref, o_ref):
    tm = xn_ref.shape[0]
    lo, hi = _unpack_rows(hp_ref[...])
    shared = _swiglu(lo.astype(BF16), hi.astype(BF16), wsgu_ref, wsdn_ref)
    gate = gate_ref[...]
    y_lo = jnp.zeros((tm, HALF_D), F32)
    y_hi = jnp.zeros((tm, HALF_D), F32)
    for k in range(TOP_K):
        r_lo, r_hi = _unpack_rows(yg_ref[k])
        gk = gate[:, k:k + 1]
        y_lo += gk * r_lo
        y_hi += gk * r_hi
    y = jnp.concatenate([y_lo, y_hi], axis=1) + shared
    xo = xn_ref[...] + g2_ref[0] * y
    ms = jnp.mean(xo * xo, axis=-1, keepdims=True)
    o_ref[...] = xo * lax.rsqrt(ms + RMS_EPS) * fng_ref[...]


def _combine(dest, gate_t, xn, hp, g2, ws_gu, ws_dn, fng, yrows, tokens_per_batch):
    n, d = xn.shape
    tm = min(COMB_TILE, tokens_per_batch)
    nt = n // tm
    tpb = tokens_per_batch // tm
    const2 = lambda i: (0, 0)
    yg = _sc_gather_rows(yrows, dest.reshape(TOP_K * n)).reshape(TOP_K, n, HALF_D)
    return pl.pallas_call(
        _combine_kernel,
        grid=(nt,),
        in_specs=[pl.BlockSpec((tm, TOP_K), lambda i: (i, 0)),
                  pl.BlockSpec((tm, d), lambda i: (i, 0)),
                  pl.BlockSpec((tm, HALF_D), lambda i: (i, 0)),
                  pl.BlockSpec((1, 1, d), lambda i: (i // tpb, 0, 0)),
                  pl.BlockSpec(ws_gu.shape, const2),
                  pl.BlockSpec(ws_dn.shape, const2),
                  pl.BlockSpec((1, d), const2),
                  pl.BlockSpec((TOP_K, tm, HALF_D), lambda i: (0, i, 0))],
        out_specs=pl.BlockSpec((tm, d), lambda i: (i, 0)),
        out_shape=jax.ShapeDtypeStruct((n, d), F32),
        compiler_params=_cp(("arbitrary",)),
        name="combine",
    )(gate_t, xn, hp, g2, ws_gu, ws_dn, fng, yg)


def _rope_tables(l):
    rows = l // GRID_W
    row = np.repeat(np.arange(rows, dtype=np.float32), GRID_W)
    col = np.tile(np.arange(GRID_W, dtype=np.float32), rows)
    n_freq = AXIS_ROT // 2
    inv_freq = (np.float32(ROPE_THETA) ** (-np.arange(n_freq, dtype=np.float32) / n_freq)).astype(np.float32)
    ang_r = row[:, None] * inv_freq
    ang_c = col[:, None] * inv_freq
    ang = np.concatenate([ang_r, ang_r, ang_c, ang_c], axis=-1).astype(np.float64)
    cos, sin = np.cos(ang), np.sin(ang)
    lane = np.arange(HEAD_DIM) % AXIS_ROT
    first = lane < AXIS_ROT // 2
    sin_a = np.where(first[None, :], -sin, 0.0)
    sin_b = np.where(first[None, :], 0.0, sin)
    two = lambda a: jnp.asarray(np.concatenate([a, a], axis=1), dtype=F32)
    return two(cos), two(sin_a), two(sin_b)


def _layer(x, ctx, c, c_ctx, lw, moe_w, fng):
    norm1_g, w_ada, b_ada, w_in, w_fourier, q_norm_g, k_norm_g, w_out, norm2_g = lw
    w_router, router_bias, w_expert_gu, w_expert_down, w_shared_gu, w_shared_down = moe_w
    b, l, d = x.shape
    n = b * l

    cond = jnp.concatenate([c, c_ctx[None, :], jnp.zeros((8 - b - 1, d), F32)], axis=0)
    mod = _adaln(cond, w_ada, b_ada)
    sh1, sc1, g1, sh2, sc2, g2 = [m[:b, None, :] for m in jnp.split(mod, 6, axis=-1)]
    csh1, csc1 = [m[b:b + 1, None, :] for m in jnp.split(mod, 6, axis=-1)[:2]]

    w_in_bf = w_in.astype(BF16)
    two = lambda g, s: jnp.tile(g * s, 2).reshape(1, LANES)
    gq, gk = two(q_norm_g, ATTN_SCALE * float(np.log2(np.e))), two(k_norm_g, 1.0)
    ones_bd = jnp.asarray(np.kron(np.eye(2), np.ones((HEAD_DIM, HEAD_DIM))), dtype=F32).astype(BF16)
    n1g = norm1_g.reshape(1, d)

    kc, vct = _inproj(ctx, csc1, csh1, n1g, w_in_bf[:, KV_COL0:], gq, gk, ones_bd, None, False)
    u, qt, k, vt = _inproj(x, sc1, sh1, n1g, w_in_bf, gq, gk, ones_bd, _rope_tables(l), True)

    w_bd = jnp.zeros((FOURIER_WIDTH, FOURIER_WIDTH), F32)
    for h in range(FOURIER_HEADS):
        w_bd = lax.dynamic_update_slice(w_bd, w_fourier[h], (h * HEAD_DIM, h * HEAD_DIM))
    four = _fourier(u, w_bd.astype(BF16))
    attn = _attention(qt, k, vt, kc, vct)

    wr_t = w_router.T
    wr_hi = wr_t.astype(BF16)
    wr_lo = (wr_t - wr_hi.astype(F32)).astype(BF16)
    xn, hp, top_e, gate, cnt = _outproj_router(
        x, four, attn, g1, w_out.astype(BF16), norm2_g.reshape(1, d), sc2, sh2,
        wr_hi, wr_lo, router_bias.reshape(N_EXPERTS, 1))

    tm = min(TOK_TILE, l)
    rb = ROW_BLOCK
    counts = cnt[:, 0, :].astype(I32)
    total = jnp.sum(counts, axis=0)
    padded = (total + rb - 1) // rb * rb
    pad_end = jnp.cumsum(padded)
    pad_start = pad_end - padded
    base = pad_start[None, :] + jnp.cumsum(counts, axis=0) - counts
    n_blocks = n * TOP_K // rb + N_EXPERTS
    n_used = (pad_end[-1] // rb).astype(I32)
    blk = jnp.minimum(jnp.arange(n_blocks, dtype=I32), n_used - 1)
    block_e = jnp.minimum(jnp.searchsorted(pad_end, blk * rb, side="right"), N_EXPERTS - 1).astype(I32)

    dest = _dest_rows(top_e, base.astype(F32)[:, :, None], tm)
    xrows = _scatter_rows(dest, hp, n_blocks * rb, tm)
    yrows = _experts(block_e, n_used.reshape(1), xrows, w_expert_gu, w_expert_down)
    return _combine(dest, gate.T, xn.reshape(n, d), hp, g2, w_shared_gu.astype(BF16),
                    w_shared_down.astype(BF16), fng, yrows, l)


def kernel(x, c, ctx, c_ctx, norm1_g, w_ada, b_ada, w_in, w_fourier, q_norm_g, k_norm_g, w_out, norm2_g,
           w_router, router_bias, w_expert_gu, w_expert_down, w_shared_gu, w_shared_down, final_norm_g):
    depth = norm1_g.shape[0]
    assert depth == 1, "context update between layers is not implemented"
    b, l, d = x.shape
    lw = (norm1_g[0], w_ada[0], b_ada[0], w_in[0], w_fourier[0], q_norm_g[0], k_norm_g[0], w_out[0], norm2_g[0])
    moe_w = (w_router[0], router_bias[0], w_expert_gu[0], w_expert_down[0], w_shared_gu[0], w_shared_down[0])
    out = _layer(x, ctx, c, c_ctx, lw, moe_w, final_norm_g.reshape(1, d))
    return out.reshape(b, l, d)
```

```python
import functools

import numpy as np
import jax
import jax.numpy as jnp
from jax import lax
from jax.experimental import pallas as pl
from jax.experimental.pallas import tpu as pltpu
from jax.experimental.pallas import tpu_sc as plsc

F32 = jnp.float32
BF16 = jnp.bfloat16
I32 = jnp.int32

D_MODEL = 1024
GRID_W = 64
HEAD_DIM = 64
FOURIER_HEADS = 4
FOURIER_WIDTH = 256
ATTN_HEADS = 12
KV_HEADS = 4
Q_PER_KV = 3
ATTN_WIDTH = 768
KV_WIDTH = 256
KV_COL0 = 1024
IN_WIDTH = 1536
ATTN_SCALE = HEAD_DIM ** -0.5
ROPE_THETA = 10000.0
AXIS_ROT = HEAD_DIM // 2
N_EXPERTS = 256
TOP_K = 8
N_EXPERT_GROUPS = 8
TOPK_GROUPS = 4
EXPERTS_PER_GROUP = 32
EXPERT_FF = 256
ROUTED_SCALE = 2.5
RMS_EPS = 1e-6

LANES = 128
PACK_W = D_MODEL // 4
DFT_L1 = 64
VMEM_LIMIT = 48 * 1024 * 1024

TOK_TILE = 512
ATTN_BQ = 256
ATTN_BK = 1024
ONES_ROWS = 16
ROW_BLOCK = 256
COMB_TILE = 256
SC_WINDOW = 128
SC_CORES = 2
SC_SUBCORES = 16

_HI = lax.Precision.HIGHEST
_NT = (((1,), (1,)), ((), ()))


def _cp(sem, vmem=VMEM_LIMIT):
    return pltpu.CompilerParams(dimension_semantics=sem, vmem_limit_bytes=vmem)


def _sigmoid(v):
    return 1.0 / (1.0 + jnp.exp(-v))


def _pack_rows(v):
    w = v.shape[1] // 4

    def pack(lo, hi):
        lo = lax.bitcast_convert_type(lo.astype(BF16).astype(F32), I32)
        hi = lax.bitcast_convert_type(hi.astype(BF16).astype(F32), I32)
        return lax.shift_right_logical(lo, 16) | (hi & jnp.int32(-65536))

    return pack(v[:, :w], v[:, 2 * w:3 * w]), pack(v[:, w:2 * w], v[:, 3 * w:])


def _unpack_rows(a, b):
    lo = lambda t: lax.bitcast_convert_type(lax.shift_left(t, 16), F32)
    hi = lambda t: lax.bitcast_convert_type(t & jnp.int32(-65536), F32)
    return [lo(a), lo(b), hi(a), hi(b)]


def _adaln_kernel(c_ref, w_ref, b_ref, o_ref):
    c = c_ref[...]
    s = c * _sigmoid(c)
    o_ref[...] = jnp.dot(s, w_ref[...], precision=_HI, preferred_element_type=F32) + b_ref[...]


def _adaln(cond, w_ada, b_ada):
    rows, d = cond.shape
    n = w_ada.shape[1]
    tn = 1536
    return pl.pallas_call(
        _adaln_kernel,
        grid=(n // tn,),
        in_specs=[pl.BlockSpec((rows, d), lambda j: (0, 0)),
                  pl.BlockSpec((d, tn), lambda j: (0, j)),
                  pl.BlockSpec((1, tn), lambda j: (0, j))],
        out_specs=pl.BlockSpec((rows, tn), lambda j: (0, j)),
        out_shape=jax.ShapeDtypeStruct((rows, n), F32),
        compiler_params=_cp(("arbitrary",)),
        name="adaln",
    )(cond, w_ada, b_ada.reshape(1, n))


def _modulated_norm(x, g, sc, sh):
    ms = jnp.mean(x * x, axis=-1, keepdims=True)
    return x * lax.rsqrt(ms + RMS_EPS) * g * (1.0 + sc) + sh


def _head_norm(chunk, gain, ones_bd):
    sq = chunk * chunk
    hi = sq.astype(BF16)
    lo = (sq - hi.astype(F32)).astype(BF16)
    ss = (jnp.dot(hi, ones_bd, preferred_element_type=F32)
          + jnp.dot(lo, ones_bd, preferred_element_type=F32))
    return chunk * lax.rsqrt(ss * (1.0 / HEAD_DIM) + RMS_EPS) * gain


def _rope(t, cos, sin_a, sin_b):
    return (t * cos + pltpu.roll(t, LANES - AXIS_ROT // 2, 1) * sin_a
            + pltpu.roll(t, AXIS_ROT // 2, 1) * sin_b)


def _inproj_kernel(*refs, rope, with_uq):
    if rope:
        (x_ref, sc_ref, sh_ref, g_ref, w_ref, gq_ref, gk_ref, ones_ref,
         cos_ref, sa_ref, sb_ref) = refs[:11]
        outs = refs[11:]
    else:
        x_ref, sc_ref, sh_ref, g_ref, w_ref, gq_ref, gk_ref, ones_ref = refs[:8]
        outs = refs[8:]
    if with_uq:
        u_ref, qt_ref, k_ref, vt_ref = outs
    else:
        k_ref, vt_ref = outs

    h = _modulated_norm(x_ref[0], g_ref[...], sc_ref[0], sh_ref[0])
    z = jnp.dot(h.astype(BF16), w_ref[...], preferred_element_type=F32)
    ones_bd = ones_ref[...]

    def normed(chunk, gain):
        t = _head_norm(chunk, gain, ones_bd)
        if rope:
            t = _rope(t, cos_ref[...], sa_ref[...], sb_ref[...])
        return t

    col = 0
    if with_uq:
        u_ref[0] = z[:, :FOURIER_WIDTH].astype(BF16)
        col = FOURIER_WIDTH
        qs = [normed(z[:, col + j * LANES: col + (j + 1) * LANES], gq_ref[...])
              for j in range(ATTN_WIDTH // LANES)]
        qt_ref[0] = jnp.concatenate(qs, axis=1).T.astype(BF16)
        col += ATTN_WIDTH
    for j in range(KV_WIDTH // LANES):
        t = normed(z[:, col + j * LANES: col + (j + 1) * LANES], gk_ref[...])
        k_ref[0, 2 * j] = t[:, :HEAD_DIM].astype(BF16)
        k_ref[0, 2 * j + 1] = t[:, HEAD_DIM:].astype(BF16)
    col += KV_WIDTH
    vt_ref[0] = z[:, col:col + KV_WIDTH].T.astype(BF16)


def _inproj(x, sc, sh, g, w, gq, gk, ones_bd, rope_tabs, with_uq):
    b, t, d = x.shape
    tm = min(TOK_TILE, t)
    wcols = w.shape[1]
    bm = sc.shape[0]
    mod_map = (lambda i, j: (i, 0, 0)) if bm == b else (lambda i, j: (0, 0, 0))
    const2 = lambda i, j: (0, 0)
    in_specs = [pl.BlockSpec((1, tm, d), lambda i, j: (i, j, 0)),
                pl.BlockSpec((1, 1, d), mod_map),
                pl.BlockSpec((1, 1, d), mod_map),
                pl.BlockSpec((1, d), const2),
                pl.BlockSpec((d, wcols), const2),
                pl.BlockSpec((1, LANES), const2),
                pl.BlockSpec((1, LANES), const2),
                pl.BlockSpec((LANES, LANES), const2)]
    args = [x, sc, sh, g, w, gq, gk, ones_bd]
    rope = rope_tabs is not None
    if rope:
        in_specs += [pl.BlockSpec((tm, LANES), lambda i, j: (j, 0))] * 3
        args += list(rope_tabs)
    out_specs, out_shape = [], []
    if with_uq:
        out_specs += [pl.BlockSpec((1, tm, FOURIER_WIDTH), lambda i, j: (i, j, 0)),
                      pl.BlockSpec((1, ATTN_WIDTH, tm), lambda i, j: (i, 0, j))]
        out_shape += [jax.ShapeDtypeStruct((b, t, FOURIER_WIDTH), BF16),
                      jax.ShapeDtypeStruct((b, ATTN_WIDTH, t), BF16)]
    out_specs += [pl.BlockSpec((1, KV_HEADS, tm, HEAD_DIM), lambda i, j: (i, 0, j, 0)),
                  pl.BlockSpec((1, KV_WIDTH, tm), lambda i, j: (i, 0, j))]
    out_shape += [jax.ShapeDtypeStruct((b, KV_HEADS, t, HEAD_DIM), BF16),
                  jax.ShapeDtypeStruct((b, KV_WIDTH, t), BF16)]
    return pl.pallas_call(
        functools.partial(_inproj_kernel, rope=rope, with_uq=with_uq),
        grid=(b, t // tm),
        in_specs=in_specs,
        out_specs=out_specs,
        out_shape=out_shape,
        compiler_params=_cp(("arbitrary", "arbitrary")),
        name="inproj_latent" if with_uq else "inproj_ctx",
    )(*args)


def _fourier_a_kernel(u_ref, c_ref, s_ref, yr_ref, yi_ref):
    u = u_ref[0]
    yr_ref[0] = jnp.dot(c_ref[...], u, preferred_element_type=F32).astype(BF16)
    yi_ref[0] = jnp.dot(s_ref[...], u, preferred_element_type=F32).astype(BF16)


def _fourier_b_kernel(yr_ref, yi_ref, m_ref, c_ref, s_ref, w_ref, o_ref, *, kb):
    y = jnp.concatenate([yr_ref[0], yi_ref[0]], axis=1)
    x = jnp.einsum("kab,kbc->kac", m_ref[...], y, preferred_element_type=F32)
    xr = x[:, :DFT_L1].reshape(kb * DFT_L1, FOURIER_WIDTH).astype(BF16)
    xi = x[:, DFT_L1:].reshape(kb * DFT_L1, FOURIER_WIDTH).astype(BF16)
    spec = (jnp.dot(xr, c_ref[...], preferred_element_type=F32)
            + jnp.dot(xi, s_ref[...], preferred_element_type=F32))
    o = jnp.dot(spec.astype(BF16), w_ref[...], preferred_element_type=F32)
    for j in range(kb):
        o_ref[0, :, j, :] = o[j * DFT_L1:(j + 1) * DFT_L1]


def _dft_tables(l):
    l2 = l // DFT_L1
    k2 = np.arange(l2)
    ang2 = 2.0 * np.pi * ((k2[:, None] * k2[None, :]) % l2) / l2
    c2 = np.cos(ang2)
    s2n = -np.sin(ang2)
    n1 = np.arange(DFT_L1)
    k = l2 * n1[None, :, None] + k2[:, None, None]
    ang = 2.0 * np.pi * ((k * n1[None, None, :]) % l) / l
    mr, mi = np.cos(ang), -np.sin(ang)
    m = np.concatenate([np.concatenate([mr, -mi], axis=2),
                        np.concatenate([mi, mr], axis=2)], axis=1)
    d = np.arange(HEAD_DIM)
    angc = 2.0 * np.pi * ((d[:, None] * d[None, :]) % HEAD_DIM) / HEAD_DIM
    scale = 1.0 / np.sqrt(float(l) * HEAD_DIM)
    eye = np.eye(FOURIER_HEADS)
    cbd = np.kron(eye, np.cos(angc) * scale)
    sbd = np.kron(eye, np.sin(angc) * scale)
    as_bf = lambda a: jnp.asarray(a, dtype=F32).astype(BF16)
    return as_bf(c2), as_bf(s2n), as_bf(m), as_bf(cbd), as_bf(sbd)


def _fourier(u, w_bd):
    b, l, fw = u.shape
    l2 = l // DFT_L1
    ncol = DFT_L1 * fw
    tn = min(4096, ncol)
    kb = min(8, l2)
    c2, s2n, m, cbd, sbd = _dft_tables(l)
    yr, yi = pl.pallas_call(
        _fourier_a_kernel,
        grid=(b, ncol // tn),
        in_specs=[pl.BlockSpec((1, l2, tn), lambda i, j: (i, 0, j)),
                  pl.BlockSpec((l2, l2), lambda i, j: (0, 0)),
                  pl.BlockSpec((l2, l2), lambda i, j: (0, 0))],
        out_specs=[pl.BlockSpec((1, l2, tn), lambda i, j: (i, 0, j))] * 2,
        out_shape=[jax.ShapeDtypeStruct((b, l2, ncol), BF16)] * 2,
        compiler_params=_cp(("arbitrary", "arbitrary")),
        name="fourier_a",
    )(u.reshape(b, l2, ncol), c2, s2n)
    yr = yr.reshape(b, l2, DFT_L1, fw)
    yi = yi.reshape(b, l2, DFT_L1, fw)
    out = pl.pallas_call(
        functools.partial(_fourier_b_kernel, kb=kb),
        grid=(b, l2 // kb),
        in_specs=[pl.BlockSpec((1, kb, DFT_L1, fw), lambda i, j: (i, j, 0, 0)),
                  pl.BlockSpec((1, kb, DFT_L1, fw), lambda i, j: (i, j, 0, 0)),
                  pl.BlockSpec((kb, 2 * DFT_L1, 2 * DFT_L1), lambda i, j: (j, 0, 0)),
                  pl.BlockSpec((fw, fw), lambda i, j: (0, 0)),
                  pl.BlockSpec((fw, fw), lambda i, j: (0, 0)),
                  pl.BlockSpec((fw, fw), lambda i, j: (0, 0))],
        out_specs=pl.BlockSpec((1, DFT_L1, kb, fw), lambda i, j: (i, 0, j, 0)),
        out_shape=jax.ShapeDtypeStruct((b, DFT_L1, l2, fw), F32),
        compiler_params=_cp(("arbitrary", "arbitrary")),
        name="fourier_b",
    )(yr, yi, m, cbd, sbd, w_bd)
    return out.reshape(b, l, fw)


def _attn_kernel(qt_ref, k_ref, vt_ref, kc_ref, vct_ref, o_ref, m_scr, acc_scr, *s_scrs, bk, nchunks):
    bq = qt_ref.shape[2]
    m_scr[...] = jnp.full(m_scr.shape, -jnp.inf, F32)
    acc_scr[...] = jnp.zeros(acc_scr.shape, F32)

    def scores(k, buf, h):
        nk = k.shape[0]
        qt = qt_ref[0, h * HEAD_DIM:(h + 1) * HEAD_DIM, :]
        s_scrs[buf][:nk, :] = jnp.dot(k, qt, preferred_element_type=F32)

    def softmax_pv(buf, h, vt):
        nk = vt.shape[1]
        s = s_scrs[buf][:nk, :]
        m_old = m_scr[h]
        m_new = jnp.maximum(m_old, jnp.max(s, axis=0, keepdims=True))
        alpha = jnp.exp2(m_old - m_new)
        p = jnp.exp2((s - m_new).astype(BF16))
        vt_ones = jnp.concatenate([vt, jnp.ones((ONES_ROWS, nk), BF16)], axis=0)
        acc_scr[h] = alpha * acc_scr[h] + jnp.dot(vt_ones, p, preferred_element_type=F32)
        m_scr[h] = m_new

    def keys(c):
        if isinstance(c, int) and c == nchunks:
            return kc_ref[0, 0]
        return k_ref[0, 0, pl.ds(pl.multiple_of(c * bk, bk), bk), :]

    def values(c):
        if isinstance(c, int) and c == nchunks:
            return vct_ref[0]
        return vt_ref[0, :, pl.ds(pl.multiple_of(c * bk, bk), bk)]

    nbuf = len(s_scrs)
    pair = 1
    ahead = nbuf - pair
    n_units = Q_PER_KV * (nchunks + 1)

    def group(us, chunk0=0):
        for u in us:
            ua = u + ahead
            if not (isinstance(chunk0, int) and ua >= n_units):
                scores(keys(chunk0 + ua // Q_PER_KV), ua % nbuf, ua % Q_PER_KV)
        for u in us:
            softmax_pv(u % nbuf, u % Q_PER_KV, values(chunk0 + u // Q_PER_KV))

    for u in range(min(ahead, n_units)):
        scores(keys(u // Q_PER_KV), u % nbuf, u % Q_PER_KV)

    chunks_per_iter = nbuf // Q_PER_KV
    n_iter = max(0, (Q_PER_KV * nchunks - ahead) // nbuf)

    def body(j, carry):
        for t in range(0, nbuf, pair):
            group(range(t, t + pair), j * chunks_per_iter)
        return carry

    lax.fori_loop(0, n_iter, body, 0)
    for t in range(n_iter * nbuf, n_units, pair):
        group(range(t, min(t + pair, n_units)))
    outs = [acc_scr[h, :HEAD_DIM, :] / acc_scr[h, HEAD_DIM:HEAD_DIM + 1, :] for h in range(Q_PER_KV)]
    o_t = jnp.concatenate(outs + [jnp.zeros((HEAD_DIM, bq), F32)], axis=0)
    o_ref[0, 0] = o_t.T[:, :Q_PER_KV * HEAD_DIM].astype(BF16)


def _attention(qt, k, vt, kc, vct):
    b, _, l = qt.shape
    c = kc.shape[2]
    bq = min(ATTN_BQ, l)
    bk = min(ATTN_BK, l)
    gw = Q_PER_KV * HEAD_DIM
    return pl.pallas_call(
        functools.partial(_attn_kernel, bk=bk, nchunks=l // bk),
        grid=(b, KV_HEADS, l // bq),
        in_specs=[pl.BlockSpec((1, gw, bq), lambda i, g, j: (i, g, j)),
                  pl.BlockSpec((1, 1, l, HEAD_DIM), lambda i, g, j: (i, g, 0, 0)),
                  pl.BlockSpec((1, HEAD_DIM, l), lambda i, g, j: (i, g, 0)),
                  pl.BlockSpec((1, 1, c, HEAD_DIM), lambda i, g, j: (i, g, 0, 0)),
                  pl.BlockSpec((1, HEAD_DIM, c), lambda i, g, j: (i, g, 0))],
        out_specs=pl.BlockSpec((1, 1, bq, gw), lambda i, g, j: (i, g, j, 0)),
        out_shape=jax.ShapeDtypeStruct((b, KV_HEADS, l, gw), BF16),
        scratch_shapes=[pltpu.VMEM((Q_PER_KV, 1, bq), F32),
                        pltpu.VMEM((Q_PER_KV, HEAD_DIM + ONES_ROWS, bq), F32),
                        *[pltpu.VMEM((max(bk, c), bq), F32)] * (2 * Q_PER_KV)],
        compiler_params=_cp(("arbitrary", "arbitrary", "arbitrary")),
        name="attention",
    )(qt, k, vt, kc, vct)


def _outproj_router_kernel(x_ref, f_ref, a_ref, g1_ref, wo_ref, n2g_ref, sc_ref, sh_ref,
                           wrh_ref, wrl_ref, bias_ref, xn_ref, hpa_ref, hpb_ref, te_ref, gt_ref, cnt_ref):
    tm = x_ref.shape[1]
    mix = jnp.dot(f_ref[0].astype(BF16), wo_ref[:FOURIER_WIDTH, :], preferred_element_type=F32)
    gw = Q_PER_KV * HEAD_DIM
    for g in range(KV_HEADS):
        r0 = FOURIER_WIDTH + g * gw
        mix += jnp.dot(a_ref[0, g], wo_ref[r0:r0 + gw, :], preferred_element_type=F32)
    xn = x_ref[0] + g1_ref[0] * mix
    xn_ref[0] = xn
    h2 = _modulated_norm(xn, n2g_ref[...], sc_ref[0], sh_ref[0])
    h2_hi = h2.astype(BF16)
    h2_lo = (h2 - h2_hi.astype(F32)).astype(BF16)
    hpa_ref[...], hpb_ref[...] = _pack_rows(h2)

    wrh = wrh_ref[...]
    logits = (lax.dot_general(wrh, h2_hi, _NT, preferred_element_type=F32)
              + lax.dot_general(wrh, h2_lo, _NT, preferred_element_type=F32)
              + lax.dot_general(wrl_ref[...], h2_hi, _NT, preferred_element_type=F32))
    scores = _sigmoid(logits)
    sel = scores + bias_ref[...]

    neg = jnp.float32(-jnp.inf)
    s3 = sel.reshape(N_EXPERT_GROUPS, EXPERTS_PER_GROUP, tm)
    i3 = lax.broadcasted_iota(I32, s3.shape, 1)
    m1 = jnp.max(s3, axis=1, keepdims=True)
    i1 = jnp.min(jnp.where(s3 == m1, i3, EXPERTS_PER_GROUP), axis=1, keepdims=True)
    m2 = jnp.max(jnp.where(i3 == i1, neg, s3), axis=1)
    gs = m1[:, 0, :] + m2
    gi = lax.broadcasted_iota(I32, gs.shape, 0)
    keep = jnp.zeros(gs.shape, jnp.bool_)
    for _ in range(TOPK_GROUPS):
        m = jnp.max(gs, axis=0, keepdims=True)
        idx = jnp.min(jnp.where(gs == m, gi, N_EXPERT_GROUPS), axis=0, keepdims=True)
        hit = gi == idx
        keep = keep | hit
        gs = jnp.where(hit, neg, gs)
    keep3 = jnp.broadcast_to(keep[:, None, :], s3.shape)
    selm = jnp.where(keep3, s3, neg).reshape(N_EXPERTS, tm)

    ei = lax.broadcasted_iota(I32, selm.shape, 0)
    multi = jnp.zeros(selm.shape, F32)
    idxs, gates = [], []
    for _ in range(TOP_K):
        m = jnp.max(selm, axis=0, keepdims=True)
        idx = jnp.min(jnp.where(selm == m, ei, N_EXPERTS), axis=0, keepdims=True)
        hit = ei == idx
        gates.append(jnp.sum(jnp.where(hit, scores, 0.0), axis=0, keepdims=True))
        idxs.append(idx)
        selm = jnp.where(hit, neg, selm)
        multi = multi + hit.astype(F32)
    gate = jnp.concatenate(gates, axis=0)
    gate = gate / jnp.sum(gate, axis=0, keepdims=True) * ROUTED_SCALE
    te_ref[...] = jnp.concatenate(idxs, axis=0)
    gt_ref[...] = gate
    ones = jnp.ones((8, tm), BF16)
    cnt_ref[0] = lax.dot_general(ones, multi.astype(BF16), _NT, preferred_element_type=F32)


def _outproj_router(x, four, attn, g1, w_out, n2g, sc2, sh2, wr_hi, wr_lo, bias):
    b, l, d = x.shape
    tm = min(TOK_TILE, l)
    tpb = l // tm
    n = b * l
    const2 = lambda i, j: (0, 0)
    mod_map = lambda i, j: (i, 0, 0)
    tok_map = lambda i, j: (0, i * tpb + j)
    return pl.pallas_call(
        _outproj_router_kernel,
        grid=(b, tpb),
        in_specs=[pl.BlockSpec((1, tm, d), lambda i, j: (i, j, 0)),
                  pl.BlockSpec((1, tm, FOURIER_WIDTH), lambda i, j: (i, j, 0)),
                  pl.BlockSpec((1, KV_HEADS, tm, Q_PER_KV * HEAD_DIM), lambda i, j: (i, 0, j, 0)),
                  pl.BlockSpec((1, 1, d), mod_map),
                  pl.BlockSpec((d, d), const2),
                  pl.BlockSpec((1, d), const2),
                  pl.BlockSpec((1, 1, d), mod_map),
                  pl.BlockSpec((1, 1, d), mod_map),
                  pl.BlockSpec((N_EXPERTS, d), const2),
                  pl.BlockSpec((N_EXPERTS, d), const2),
                  pl.BlockSpec((N_EXPERTS, 1), const2)],
        out_specs=[pl.BlockSpec((1, tm, d), lambda i, j: (i, j, 0)),
                   pl.BlockSpec((tm, PACK_W), lambda i, j: (i * tpb + j, 0)),
                   pl.BlockSpec((tm, PACK_W), lambda i, j: (i * tpb + j, 0)),
                   pl.BlockSpec((TOP_K, tm), tok_map),
                   pl.BlockSpec((TOP_K, tm), tok_map),
                   pl.BlockSpec((1, 8, N_EXPERTS), lambda i, j: (i * tpb + j, 0, 0))],
        out_shape=[jax.ShapeDtypeStruct((b, l, d), F32),
                   jax.ShapeDtypeStruct((n, PACK_W), I32),
                   jax.ShapeDtypeStruct((n, PACK_W), I32),
                   jax.ShapeDtypeStruct((TOP_K, n), I32),
                   jax.ShapeDtypeStruct((TOP_K, n), F32),
                   jax.ShapeDtypeStruct((n // tm, 8, N_EXPERTS), F32)],
        compiler_params=_cp(("arbitrary", "arbitrary")),
        name="outproj_router",
    )(x, four, attn, g1, w_out, n2g, sc2, sh2, wr_hi, wr_lo, bias)


def _dest_kernel(te_ref, base_ref, tri_ref, d_ref):
    te = te_ref[...]
    tm = te.shape[1]
    ei = lax.broadcasted_iota(I32, (N_EXPERTS, tm), 0)
    hits = [ei == te[k:k + 1, :] for k in range(TOP_K)]
    multi = hits[0].astype(F32)
    for k in range(1, TOP_K):
        multi = multi + hits[k].astype(F32)
    rank = jnp.dot(multi.astype(BF16), tri_ref[...], preferred_element_type=F32)
    pos = rank + base_ref[0]
    rows = [jnp.sum(jnp.where(hits[k], pos, 0.0), axis=0, keepdims=True) for k in range(TOP_K)]
    d_ref[...] = jnp.concatenate(rows, axis=0).astype(I32)


def _dest_rows(top_e, base, tm):
    n = top_e.shape[1]
    tri = jnp.asarray(np.triu(np.ones((tm, tm), np.float32), 1)).astype(BF16)
    return pl.pallas_call(
        _dest_kernel,
        grid=(n // tm,),
        in_specs=[pl.BlockSpec((TOP_K, tm), lambda i: (0, i)),
                  pl.BlockSpec((1, N_EXPERTS, 1), lambda i: (i, 0, 0)),
                  pl.BlockSpec((tm, tm), lambda i: (0, 0))],
        out_specs=pl.BlockSpec((TOP_K, tm), lambda i: (0, i)),
        out_shape=jax.ShapeDtypeStruct((TOP_K, n), I32),
        compiler_params=_cp(("arbitrary",)),
        name="dest_rows",
    )(top_e, base, tri)


def _sc_mesh():
    return plsc.VectorSubcoreMesh(core_axis_name="c", subcore_axis_name="s",
                                  num_cores=SC_CORES, num_subcores=SC_SUBCORES)


def _sc_scatter_rows(src, dest, n_rows):
    n, w = src.shape
    kk = dest.shape[0]

    @pl.kernel(out_type=jax.ShapeDtypeStruct((n_rows, w), src.dtype), mesh=_sc_mesh(), name="sc_scatter_rows")
    def scatter(x_hbm, d_hbm, o_hbm):
        def body(x_vmem, d_vmem):
            for k in range(kk):
                pltpu.sync_copy(x_vmem, o_hbm.at[d_vmem.at[k]])

        pltpu.emit_pipeline(
            body,
            grid=(n // SC_WINDOW,),
            in_specs=[pl.BlockSpec((SC_WINDOW, w), lambda i: (i, 0)),
                      pl.BlockSpec((kk, SC_WINDOW), lambda i: (0, i))],
            out_specs=[],
            core_axis_name=("c", "s"),
            dimension_semantics=(pltpu.PARALLEL,),
        )(x_hbm, d_hbm)

    return scatter(src, dest)


def _swiglu(quarters, wgu, wdn):
    ag = None
    for j, q in enumerate(quarters):
        part = jnp.dot(q, wgu[j * PACK_W:(j + 1) * PACK_W, :], preferred_element_type=F32)
        ag = part if ag is None else ag + part
    ff = ag.shape[1] // 2
    a, g = ag[:, :ff], ag[:, ff:]
    mid = (a * _sigmoid(a) * g).astype(BF16)
    return jnp.dot(mid, wdn[...], preferred_element_type=F32)


def _expert_kernel(be_ref, nv_ref, xa_ref, xb_ref, wgu_ref, wdn_ref, ya_ref, yb_ref, wgu_s, wdn_s):
    i = pl.program_id(0)
    nvalid = nv_ref[i]

    @pl.when(nvalid > 0)
    def _():
        prev = be_ref[jnp.maximum(i - 1, 0)]

        @pl.when((i == 0) | (be_ref[i] != prev))
        def _():
            wgu_s[...] = wgu_ref[0].astype(BF16)
            wdn_s[...] = wdn_ref[0].astype(BF16)

        valid = lax.broadcasted_iota(I32, xa_ref.shape, 0) < nvalid
        xa = jnp.where(valid, xa_ref[...], 0)
        xb = jnp.where(valid, xb_ref[...], 0)
        quarters = [q.astype(BF16) for q in _unpack_rows(xa, xb)]
        ya_ref[...], yb_ref[...] = _pack_rows(_swiglu(quarters, wgu_s, wdn_s))

    @pl.when(nvalid == 0)
    def _():
        ya_ref[...] = jnp.zeros(ya_ref.shape, I32)
        yb_ref[...] = jnp.zeros(yb_ref.shape, I32)


def _experts(block_e, block_nvalid, xa, xb, w_gu, w_down):
    n_rows = xa.shape[0]
    rb = ROW_BLOCK
    d, ff2 = w_gu.shape[1], w_gu.shape[2]
    row_map = lambda i, be, nv: (i, 0)
    grid_spec = pltpu.PrefetchScalarGridSpec(
        num_scalar_prefetch=2,
        grid=(n_rows // rb,),
        in_specs=[pl.BlockSpec((rb, PACK_W), row_map),
                  pl.BlockSpec((rb, PACK_W), row_map),
                  pl.BlockSpec((1, d, ff2), lambda i, be, nv: (be[i], 0, 0)),
                  pl.BlockSpec((1, ff2 // 2, d), lambda i, be, nv: (be[i], 0, 0))],
        out_specs=[pl.BlockSpec((rb, PACK_W), row_map)] * 2,
        scratch_shapes=[pltpu.VMEM((d, ff2), BF16), pltpu.VMEM((ff2 // 2, d), BF16)],
    )
    return pl.pallas_call(
        _expert_kernel,
        grid_spec=grid_spec,
        out_shape=[jax.ShapeDtypeStruct((n_rows, PACK_W), I32)] * 2,
        compiler_params=_cp(("arbitrary",)),
        name="experts",
    )(block_e, block_nvalid, xa, xb, w_gu, w_down)


def _sc_gather_rows(table, idx):
    m = idx.shape[1]
    w = table.shape[1]

    @pl.kernel(out_type=jax.ShapeDtypeStruct((m, w), table.dtype), mesh=_sc_mesh(), name="sc_gather_rows")
    def gather(t_hbm, i_hbm, o_hbm):
        def body(i_vmem, o_vmem):
            pltpu.sync_copy(t_hbm.at[i_vmem.at[0]], o_vmem)

        pltpu.emit_pipeline(
            body,
            grid=(m // SC_WINDOW,),
            in_specs=[pl.BlockSpec((1, SC_WINDOW), lambda i: (0, i))],
            out_specs=[pl.BlockSpec((SC_WINDOW, w), lambda i: (i, 0))],
            core_axis_name=("c", "s"),
            dimension_semantics=(pltpu.PARALLEL,),
        )(i_hbm, o_hbm)

    return gather(table, idx)


def _combine_kernel(gate_ref, xn_ref, hpa_ref, hpb_ref, g2_ref, wsgu_ref, wsdn_ref, fng_ref,
                    yga_ref, ygb_ref, o_ref):
    quarters = [q.astype(BF16) for q in _unpack_rows(hpa_ref[...], hpb_ref[...])]
    shared = _swiglu(quarters, wsgu_ref, wsdn_ref)
    gate = gate_ref[...]
    acc = None
    for k in range(TOP_K):
        gk = gate[:, k:k + 1]
        rows = [gk * q for q in _unpack_rows(yga_ref[k], ygb_ref[k])]
        acc = rows if acc is None else [a + r for a, r in zip(acc, rows)]
    y = jnp.concatenate(acc, axis=1) + shared
    xo = xn_ref[...] + g2_ref[0] * y
    ms = jnp.mean(xo * xo, axis=-1, keepdims=True)
    o_ref[...] = xo * lax.rsqrt(ms + RMS_EPS) * fng_ref[...]


def _combine(dest, gate_t, xn, hpa, hpb, g2, ws_gu, ws_dn, fng, ya, yb, tokens_per_batch):
    n, d = xn.shape
    tm = min(COMB_TILE, tokens_per_batch)
    nt = n // tm
    tpb = tokens_per_batch // tm
    const2 = lambda i: (0, 0)
    row_map = lambda i: (i, 0)
    idx = dest.reshape(1, TOP_K * n)
    yga = _sc_gather_rows(ya, idx).reshape(TOP_K, n, PACK_W)
    ygb = _sc_gather_rows(yb, idx).reshape(TOP_K, n, PACK_W)
    return pl.pallas_call(
        _combine_kernel,
        grid=(nt,),
        in_specs=[pl.BlockSpec((tm, TOP_K), row_map),
                  pl.BlockSpec((tm, d), row_map),
                  pl.BlockSpec((tm, PACK_W), row_map),
                  pl.BlockSpec((tm, PACK_W), row_map),
                  pl.BlockSpec((1, 1, d), lambda i: (i // tpb, 0, 0)),
                  pl.BlockSpec(ws_gu.shape, const2),
                  pl.BlockSpec(ws_dn.shape, const2),
                  pl.BlockSpec((1, d), const2),
                  pl.BlockSpec((TOP_K, tm, PACK_W), lambda i: (0, i, 0)),
                  pl.BlockSpec((TOP_K, tm, PACK_W), lambda i: (0, i, 0))],
        out_specs=pl.BlockSpec((tm, d), row_map),
        out_shape=jax.ShapeDtypeStruct((n, d), F32),
        compiler_params=_cp(("arbitrary",)),
        name="combine",
    )(gate_t, xn, hpa, hpb, g2, ws_gu, ws_dn, fng, yga, ygb)


def _rope_tables(l):
    rows = l // GRID_W
    row = np.repeat(np.arange(rows, dtype=np.float32), GRID_W)
    col = np.tile(np.arange(GRID_W, dtype=np.float32), rows)
    n_freq = AXIS_ROT // 2
    inv_freq = (np.float32(ROPE_THETA) ** (-np.arange(n_freq, dtype=np.float32) / n_freq)).astype(np.float32)
    ang_r = row[:, None] * inv_freq
    ang_c = col[:, None] * inv_freq
    ang = np.concatenate([ang_r, ang_r, ang_c, ang_c], axis=-1).astype(np.float64)
    cos, sin = np.cos(ang), np.sin(ang)
    lane = np.arange(HEAD_DIM) % AXIS_ROT
    first = lane < AXIS_ROT // 2
    sin_a = np.where(first[None, :], -sin, 0.0)
    sin_b = np.where(first[None, :], 0.0, sin)
    two = lambda a: jnp.asarray(np.concatenate([a, a], axis=1), dtype=F32)
    return two(cos), two(sin_a), two(sin_b)


def _layer(x, ctx, c, c_ctx, lw, moe_w, fng):
    norm1_g, w_ada, b_ada, w_in, w_fourier, q_norm_g, k_norm_g, w_out, norm2_g = lw
    w_router, router_bias, w_expert_gu, w_expert_down, w_shared_gu, w_shared_down = moe_w
    b, l, d = x.shape
    n = b * l

    cond = jnp.concatenate([c, c_ctx[None, :], jnp.zeros((8 - b - 1, d), F32)], axis=0)
    mod = _adaln(cond, w_ada, b_ada)
    sh1, sc1, g1, sh2, sc2, g2 = [m[:b, None, :] for m in jnp.split(mod, 6, axis=-1)]
    csh1, csc1 = [m[b:b + 1, None, :] for m in jnp.split(mod, 6, axis=-1)[:2]]

    w_in_bf = w_in.astype(BF16)
    two = lambda g, s: jnp.tile(g * s, 2).reshape(1, LANES)
    gq, gk = two(q_norm_g, ATTN_SCALE * float(np.log2(np.e))), two(k_norm_g, 1.0)
    ones_bd = jnp.asarray(np.kron(np.eye(2), np.ones((HEAD_DIM, HEAD_DIM))), dtype=F32).astype(BF16)
    n1g = norm1_g.reshape(1, d)

    kc, vct = _inproj(ctx, csc1, csh1, n1g, w_in_bf[:, KV_COL0:], gq, gk, ones_bd, None, False)
    u, qt, k, vt = _inproj(x, sc1, sh1, n1g, w_in_bf, gq, gk, ones_bd, _rope_tables(l), True)

    w_bd = jnp.zeros((FOURIER_WIDTH, FOURIER_WIDTH), F32)
    for h in range(FOURIER_HEADS):
        w_bd = lax.dynamic_update_slice(w_bd, w_fourier[h], (h * HEAD_DIM, h * HEAD_DIM))
    four = _fourier(u, w_bd.astype(BF16))
    attn = _attention(qt, k, vt, kc, vct)

    wr_t = w_router.T
    wr_hi = wr_t.astype(BF16)
    wr_lo = (wr_t - wr_hi.astype(F32)).astype(BF16)
    xn, hpa, hpb, top_e, gate, cnt = _outproj_router(
        x, four, attn, g1, w_out.astype(BF16), norm2_g.reshape(1, d), sc2, sh2,
        wr_hi, wr_lo, router_bias.reshape(N_EXPERTS, 1))

    tm = min(TOK_TILE, l)
    rb = ROW_BLOCK
    counts = cnt[:, 0, :].astype(I32)
    total = jnp.sum(counts, axis=0)
    padded = (total + rb - 1) // rb * rb
    pad_end = jnp.cumsum(padded)
    pad_start = pad_end - padded
    base = pad_start[None, :] + jnp.cumsum(counts, axis=0) - counts
    n_blocks = n * TOP_K // rb + N_EXPERTS
    n_used = pad_end[-1] // rb
    blk_row = jnp.minimum(jnp.arange(n_blocks, dtype=I32), n_used - 1) * rb
    block_e = jnp.sum((pad_end[None, :] <= blk_row[:, None]).astype(I32), axis=1)
    block_e = jnp.minimum(block_e, N_EXPERTS - 1)
    row_end = (pad_start + total)[block_e]
    block_nvalid = jnp.where(jnp.arange(n_blocks) < n_used, jnp.clip(row_end - blk_row, 0, rb), 0).astype(I32)

    dest = _dest_rows(top_e, base.astype(F32)[:, :, None], tm)
    xa = _sc_scatter_rows(hpa, dest, n_blocks * rb)
    xb = _sc_scatter_rows(hpb, dest, n_blocks * rb)
    ya, yb = _experts(block_e, block_nvalid, xa, xb, w_expert_gu, w_expert_down)
    return _combine(dest, gate.T, xn.reshape(n, d), hpa, hpb, g2, w_shared_gu.astype(BF16),
                    w_shared_down.astype(BF16), fng, ya, yb, l)


def kernel(x, c, ctx, c_ctx, norm1_g, w_ada, b_ada, w_in, w_fourier, q_norm_g, k_norm_g, w_out, norm2_g,
           w_router, router_bias, w_expert_gu, w_expert_down, w_shared_gu, w_shared_down, final_norm_g):
    depth = norm1_g.shape[0]
    assert depth == 1, "context update between layers is not implemented"
    b, l, d = x.shape
    lw = (norm1_g[0], w_ada[0], b_ada[0], w_in[0], w_fourier[0], q_norm_g[0], k_norm_g[0], w_out[0], norm2_g[0])
    moe_w = (w_router[0], router_bias[0], w_expert_gu[0], w_expert_down[0], w_shared_gu[0], w_shared_down[0])
    out = _layer(x, ctx, c, c_ctx, lw, moe_w, final_norm_g.reshape(1, d))
    return out.reshape(b, l, d)
```

```python
import functools

import numpy as np
import jax
import jax.numpy as jnp
from jax import lax
from jax.experimental import pallas as pl
from jax.experimental.pallas import tpu as pltpu
from jax.experimental.pallas import tpu_sc as plsc

F32 = jnp.float32
BF16 = jnp.bfloat16
I32 = jnp.int32

D_MODEL = 1024
GRID_W = 64
HEAD_DIM = 64
FOURIER_HEADS = 4
FOURIER_WIDTH = 256
ATTN_HEADS = 12
KV_HEADS = 4
Q_PER_KV = 3
ATTN_WIDTH = 768
KV_WIDTH = 256
KV_COL0 = 1024
IN_WIDTH = 1536
ATTN_SCALE = HEAD_DIM ** -0.5
ROPE_THETA = 10000.0
AXIS_ROT = HEAD_DIM // 2
N_EXPERTS = 256
TOP_K = 8
N_EXPERT_GROUPS = 8
TOPK_GROUPS = 4
EXPERTS_PER_GROUP = 32
EXPERT_FF = 256
ROUTED_SCALE = 2.5
RMS_EPS = 1e-6

LANES = 128
PACK_W = D_MODEL // 4
DFT_L1 = 64
VMEM_LIMIT = 48 * 1024 * 1024

TOK_TILE = 512
ATTN_BQ = 512
ATTN_BK = 1024
ONES_ROWS = 16
ROW_BLOCK = 512
COMB_TILE = 256
SC_WINDOW = 128
SC_CORES = 2
SC_SUBCORES = 16

_HI = lax.Precision.HIGHEST
_NT = (((1,), (1,)), ((), ()))


def _cp(sem, vmem=VMEM_LIMIT):
    return pltpu.CompilerParams(dimension_semantics=sem, vmem_limit_bytes=vmem)


def _sigmoid(v):
    return 1.0 / (1.0 + jnp.exp(-v))


def _pack_rows(v):
    w = v.shape[1] // 4

    def pack(lo, hi):
        lo = lax.bitcast_convert_type(lo.astype(BF16).astype(F32), I32)
        hi = lax.bitcast_convert_type(hi.astype(BF16).astype(F32), I32)
        return lax.shift_right_logical(lo, 16) | (hi & jnp.int32(-65536))

    return pack(v[:, :w], v[:, 2 * w:3 * w]), pack(v[:, w:2 * w], v[:, 3 * w:])


def _unpack_rows(a, b):
    lo = lambda t: lax.bitcast_convert_type(lax.shift_left(t, 16), F32)
    hi = lambda t: lax.bitcast_convert_type(t & jnp.int32(-65536), F32)
    return [lo(a), lo(b), hi(a), hi(b)]


def _adaln_kernel(c_ref, w_ref, b_ref, o_ref):
    c = c_ref[...]
    s = c * _sigmoid(c)
    o_ref[...] = jnp.dot(s, w_ref[...], precision=_HI, preferred_element_type=F32) + b_ref[...]


def _adaln(cond, w_ada, b_ada):
    rows, d = cond.shape
    n = w_ada.shape[1]
    tn = 1536
    return pl.pallas_call(
        _adaln_kernel,
        grid=(n // tn,),
        in_specs=[pl.BlockSpec((rows, d), lambda j: (0, 0)),
                  pl.BlockSpec((d, tn), lambda j: (0, j)),
                  pl.BlockSpec((1, tn), lambda j: (0, j))],
        out_specs=pl.BlockSpec((rows, tn), lambda j: (0, j)),
        out_shape=jax.ShapeDtypeStruct((rows, n), F32),
        compiler_params=_cp(("arbitrary",)),
        name="adaln",
    )(cond, w_ada, b_ada.reshape(1, n))


def _modulated_norm(x, g, sc, sh):
    ms = jnp.mean(x * x, axis=-1, keepdims=True)
    return x * lax.rsqrt(ms + RMS_EPS) * g * (1.0 + sc) + sh


def _head_norm(chunk, gain, ones_bd):
    sq = chunk * chunk
    hi = sq.astype(BF16)
    lo = (sq - hi.astype(F32)).astype(BF16)
    ss = (jnp.dot(hi, ones_bd, preferred_element_type=F32)
          + jnp.dot(lo, ones_bd, preferred_element_type=F32))
    return chunk * lax.rsqrt(ss * (1.0 / HEAD_DIM) + RMS_EPS) * gain


def _rope(t, cos, sin_a, sin_b):
    return (t * cos + pltpu.roll(t, LANES - AXIS_ROT // 2, 1) * sin_a
            + pltpu.roll(t, AXIS_ROT // 2, 1) * sin_b)


def _inproj_kernel(*refs, rope, with_uq):
    if rope:
        (x_ref, sc_ref, sh_ref, g_ref, w_ref, gq_ref, gk_ref, ones_ref,
         cos_ref, sa_ref, sb_ref) = refs[:11]
        outs = refs[11:]
    else:
        x_ref, sc_ref, sh_ref, g_ref, w_ref, gq_ref, gk_ref, ones_ref = refs[:8]
        outs = refs[8:]
    if with_uq:
        u_ref, qt_ref, k_ref, vt_ref = outs
    else:
        k_ref, vt_ref = outs

    h = _modulated_norm(x_ref[0], g_ref[...], sc_ref[0], sh_ref[0])
    z = jnp.dot(h.astype(BF16), w_ref[...], preferred_element_type=F32)
    ones_bd = ones_ref[...]

    def normed(chunk, gain):
        t = _head_norm(chunk, gain, ones_bd)
        if rope:
            t = _rope(t, cos_ref[...], sa_ref[...], sb_ref[...])
        return t

    col = 0
    if with_uq:
        u_ref[0] = z[:, :FOURIER_WIDTH].astype(BF16)
        col = FOURIER_WIDTH
        qs = [normed(z[:, col + j * LANES: col + (j + 1) * LANES], gq_ref[...])
              for j in range(ATTN_WIDTH // LANES)]
        qt_ref[0] = jnp.concatenate(qs, axis=1).T.astype(BF16)
        col += ATTN_WIDTH
    for j in range(KV_WIDTH // LANES):
        t = normed(z[:, col + j * LANES: col + (j + 1) * LANES], gk_ref[...])
        k_ref[0, 2 * j] = t[:, :HEAD_DIM].astype(BF16)
        k_ref[0, 2 * j + 1] = t[:, HEAD_DIM:].astype(BF16)
    col += KV_WIDTH
    vt_ref[0] = z[:, col:col + KV_WIDTH].T.astype(BF16)


def _inproj(x, sc, sh, g, w, gq, gk, ones_bd, rope_tabs, with_uq):
    b, t, d = x.shape
    tm = min(TOK_TILE, t)
    wcols = w.shape[1]
    bm = sc.shape[0]
    mod_map = (lambda i, j: (i, 0, 0)) if bm == b else (lambda i, j: (0, 0, 0))
    const2 = lambda i, j: (0, 0)
    in_specs = [pl.BlockSpec((1, tm, d), lambda i, j: (i, j, 0)),
                pl.BlockSpec((1, 1, d), mod_map),
                pl.BlockSpec((1, 1, d), mod_map),
                pl.BlockSpec((1, d), const2),
                pl.BlockSpec((d, wcols), const2),
                pl.BlockSpec((1, LANES), const2),
                pl.BlockSpec((1, LANES), const2),
                pl.BlockSpec((LANES, LANES), const2)]
    args = [x, sc, sh, g, w, gq, gk, ones_bd]
    rope = rope_tabs is not None
    if rope:
        in_specs += [pl.BlockSpec((tm, LANES), lambda i, j: (j, 0))] * 3
        args += list(rope_tabs)
    out_specs, out_shape = [], []
    if with_uq:
        out_specs += [pl.BlockSpec((1, tm, FOURIER_WIDTH), lambda i, j: (i, j, 0)),
                      pl.BlockSpec((1, ATTN_WIDTH, tm), lambda i, j: (i, 0, j))]
        out_shape += [jax.ShapeDtypeStruct((b, t, FOURIER_WIDTH), BF16),
                      jax.ShapeDtypeStruct((b, ATTN_WIDTH, t), BF16)]
    out_specs += [pl.BlockSpec((1, KV_HEADS, tm, HEAD_DIM), lambda i, j: (i, 0, j, 0)),
                  pl.BlockSpec((1, KV_WIDTH, tm), lambda i, j: (i, 0, j))]
    out_shape += [jax.ShapeDtypeStruct((b, KV_HEADS, t, HEAD_DIM), BF16),
                  jax.ShapeDtypeStruct((b, KV_WIDTH, t), BF16)]
    return pl.pallas_call(
        functools.partial(_inproj_kernel, rope=rope, with_uq=with_uq),
        grid=(b, t // tm),
        in_specs=in_specs,
        out_specs=out_specs,
        out_shape=out_shape,
        compiler_params=_cp(("arbitrary", "arbitrary")),
        name="inproj_latent" if with_uq else "inproj_ctx",
    )(*args)


def _fourier_a_kernel(u_ref, c_ref, s_ref, yr_ref, yi_ref):
    u = u_ref[0]
    yr_ref[0] = jnp.dot(c_ref[...], u, preferred_element_type=F32).astype(BF16)
    yi_ref[0] = jnp.dot(s_ref[...], u, preferred_element_type=F32).astype(BF16)


def _fourier_b_kernel(yr_ref, yi_ref, m_ref, c_ref, s_ref, w_ref, o_ref, *, kb):
    y = jnp.concatenate([yr_ref[0], yi_ref[0]], axis=1)
    x = jnp.einsum("kab,kbc->kac", m_ref[...], y, preferred_element_type=F32)
    xr = x[:, :DFT_L1].reshape(kb * DFT_L1, FOURIER_WIDTH).astype(BF16)
    xi = x[:, DFT_L1:].reshape(kb * DFT_L1, FOURIER_WIDTH).astype(BF16)
    spec = (jnp.dot(xr, c_ref[...], preferred_element_type=F32)
            + jnp.dot(xi, s_ref[...], preferred_element_type=F32))
    o = jnp.dot(spec.astype(BF16), w_ref[...], preferred_element_type=F32)
    for j in range(kb):
        o_ref[0, :, j, :] = o[j * DFT_L1:(j + 1) * DFT_L1]


def _dft_tables(l):
    l2 = l // DFT_L1
    k2 = np.arange(l2)
    ang2 = 2.0 * np.pi * ((k2[:, None] * k2[None, :]) % l2) / l2
    c2 = np.cos(ang2)
    s2n = -np.sin(ang2)
    n1 = np.arange(DFT_L1)
    k = l2 * n1[None, :, None] + k2[:, None, None]
    ang = 2.0 * np.pi * ((k * n1[None, None, :]) % l) / l
    mr, mi = np.cos(ang), -np.sin(ang)
    m = np.concatenate([np.concatenate([mr, -mi], axis=2),
                        np.concatenate([mi, mr], axis=2)], axis=1)
    d = np.arange(HEAD_DIM)
    angc = 2.0 * np.pi * ((d[:, None] * d[None, :]) % HEAD_DIM) / HEAD_DIM
    scale = 1.0 / np.sqrt(float(l) * HEAD_DIM)
    eye = np.eye(FOURIER_HEADS)
    cbd = np.kron(eye, np.cos(angc) * scale)
    sbd = np.kron(eye, np.sin(angc) * scale)
    as_bf = lambda a: jnp.asarray(a, dtype=F32).astype(BF16)
    return as_bf(c2), as_bf(s2n), as_bf(m), as_bf(cbd), as_bf(sbd)


def _fourier(u, w_bd):
    b, l, fw = u.shape
    l2 = l // DFT_L1
    ncol = DFT_L1 * fw
    tn = min(4096, ncol)
    kb = min(8, l2)
    c2, s2n, m, cbd, sbd = _dft_tables(l)
    yr, yi = pl.pallas_call(
        _fourier_a_kernel,
        grid=(b, ncol // tn),
        in_specs=[pl.BlockSpec((1, l2, tn), lambda i, j: (i, 0, j)),
                  pl.BlockSpec((l2, l2), lambda i, j: (0, 0)),
                  pl.BlockSpec((l2, l2), lambda i, j: (0, 0))],
        out_specs=[pl.BlockSpec((1, l2, tn), lambda i, j: (i, 0, j))] * 2,
        out_shape=[jax.ShapeDtypeStruct((b, l2, ncol), BF16)] * 2,
        compiler_params=_cp(("arbitrary", "arbitrary")),
        name="fourier_a",
    )(u.reshape(b, l2, ncol), c2, s2n)
    yr = yr.reshape(b, l2, DFT_L1, fw)
    yi = yi.reshape(b, l2, DFT_L1, fw)
    out = pl.pallas_call(
        functools.partial(_fourier_b_kernel, kb=kb),
        grid=(b, l2 // kb),
        in_specs=[pl.BlockSpec((1, kb, DFT_L1, fw), lambda i, j: (i, j, 0, 0)),
                  pl.BlockSpec((1, kb, DFT_L1, fw), lambda i, j: (i, j, 0, 0)),
                  pl.BlockSpec((kb, 2 * DFT_L1, 2 * DFT_L1), lambda i, j: (j, 0, 0)),
                  pl.BlockSpec((fw, fw), lambda i, j: (0, 0)),
                  pl.BlockSpec((fw, fw), lambda i, j: (0, 0)),
                  pl.BlockSpec((fw, fw), lambda i, j: (0, 0))],
        out_specs=pl.BlockSpec((1, DFT_L1, kb, fw), lambda i, j: (i, 0, j, 0)),
        out_shape=jax.ShapeDtypeStruct((b, DFT_L1, l2, fw), F32),
        compiler_params=_cp(("arbitrary", "arbitrary")),
        name="fourier_b",
    )(yr, yi, m, cbd, sbd, w_bd)
    return out.reshape(b, l, fw)


def _attn_kernel(qt_ref, k_ref, vt_ref, kc_ref, vct_ref, o_ref, m_scr, acc_scr, *s_scrs, bk, nchunks):
    bq = qt_ref.shape[2]
    m_scr[...] = jnp.full(m_scr.shape, -jnp.inf, F32)
    acc_scr[...] = jnp.zeros(acc_scr.shape, F32)

    def scores(k, buf, h):
        nk = k.shape[0]
        qt = qt_ref[0, h * HEAD_DIM:(h + 1) * HEAD_DIM, :]
        s_scrs[buf][:nk, :] = jnp.dot(k, qt, preferred_element_type=F32)

    def softmax_pv(buf, h, vt):
        nk = vt.shape[1]
        s = s_scrs[buf][:nk, :]
        m_old = m_scr[h]
        m_new = jnp.maximum(m_old, jnp.max(s, axis=0, keepdims=True))
        alpha = jnp.exp2(m_old - m_new)
        p = jnp.exp2((s - m_new).astype(BF16))
        vt_ones = jnp.concatenate([vt, jnp.ones((ONES_ROWS, nk), BF16)], axis=0)
        acc_scr[h] = alpha * acc_scr[h] + jnp.dot(vt_ones, p, preferred_element_type=F32)
        m_scr[h] = m_new

    def keys(c):
        if isinstance(c, int) and c == nchunks:
            return kc_ref[0, 0]
        return k_ref[0, 0, pl.ds(pl.multiple_of(c * bk, bk), bk), :]

    def values(c):
        if isinstance(c, int) and c == nchunks:
            return vct_ref[0]
        return vt_ref[0, :, pl.ds(pl.multiple_of(c * bk, bk), bk)]

    nbuf = len(s_scrs)
    pair = 1
    ahead = nbuf - pair
    n_units = Q_PER_KV * (nchunks + 1)

    def group(us, chunk0=0):
        for u in us:
            ua = u + ahead
            if not (isinstance(chunk0, int) and ua >= n_units):
                scores(keys(chunk0 + ua // Q_PER_KV), ua % nbuf, ua % Q_PER_KV)
        for u in us:
            softmax_pv(u % nbuf, u % Q_PER_KV, values(chunk0 + u // Q_PER_KV))

    for u in range(min(ahead, n_units)):
        scores(keys(u // Q_PER_KV), u % nbuf, u % Q_PER_KV)

    chunks_per_iter = nbuf // Q_PER_KV
    n_iter = max(0, (Q_PER_KV * nchunks - ahead) // nbuf)

    def body(j, carry):
        for t in range(0, nbuf, pair):
            group(range(t, t + pair), j * chunks_per_iter)
        return carry

    lax.fori_loop(0, n_iter, body, 0)
    for t in range(n_iter * nbuf, n_units, pair):
        group(range(t, min(t + pair, n_units)))
    outs = [acc_scr[h, :HEAD_DIM, :] / acc_scr[h, HEAD_DIM:HEAD_DIM + 1, :] for h in range(Q_PER_KV)]
    o_t = jnp.concatenate(outs + [jnp.zeros((HEAD_DIM, bq), F32)], axis=0)
    o_ref[0, 0] = o_t.T[:, :Q_PER_KV * HEAD_DIM].astype(BF16)


def _attention(qt, k, vt, kc, vct):
    b, _, l = qt.shape
    c = kc.shape[2]
    bq = min(ATTN_BQ, l)
    bk = min(ATTN_BK, l)
    gw = Q_PER_KV * HEAD_DIM
    return pl.pallas_call(
        functools.partial(_attn_kernel, bk=bk, nchunks=l // bk),
        grid=(b, KV_HEADS, l // bq),
        in_specs=[pl.BlockSpec((1, gw, bq), lambda i, g, j: (i, g, j)),
                  pl.BlockSpec((1, 1, l, HEAD_DIM), lambda i, g, j: (i, g, 0, 0)),
                  pl.BlockSpec((1, HEAD_DIM, l), lambda i, g, j: (i, g, 0)),
                  pl.BlockSpec((1, 1, c, HEAD_DIM), lambda i, g, j: (i, g, 0, 0)),
                  pl.BlockSpec((1, HEAD_DIM, c), lambda i, g, j: (i, g, 0))],
        out_specs=pl.BlockSpec((1, 1, bq, gw), lambda i, g, j: (i, g, j, 0)),
        out_shape=jax.ShapeDtypeStruct((b, KV_HEADS, l, gw), BF16),
        scratch_shapes=[pltpu.VMEM((Q_PER_KV, 1, bq), F32),
                        pltpu.VMEM((Q_PER_KV, HEAD_DIM + ONES_ROWS, bq), F32),
                        *[pltpu.VMEM((max(bk, c), bq), F32)] * (2 * Q_PER_KV)],
        compiler_params=_cp(("arbitrary", "arbitrary", "arbitrary")),
        name="attention",
    )(qt, k, vt, kc, vct)


def _outproj_router_kernel(x_ref, f_ref, a_ref, g1_ref, wo_ref, n2g_ref, sc_ref, sh_ref,
                           wrh_ref, wrl_ref, bias_ref, xn_ref, hpa_ref, hpb_ref, te_ref, gt_ref, cnt_ref):
    tm = x_ref.shape[1]
    mix = jnp.dot(f_ref[0].astype(BF16), wo_ref[:FOURIER_WIDTH, :], preferred_element_type=F32)
    gw = Q_PER_KV * HEAD_DIM
    for g in range(KV_HEADS):
        r0 = FOURIER_WIDTH + g * gw
        mix += jnp.dot(a_ref[0, g], wo_ref[r0:r0 + gw, :], preferred_element_type=F32)
    xn = x_ref[0] + g1_ref[0] * mix
    xn_ref[0] = xn
    h2 = _modulated_norm(xn, n2g_ref[...], sc_ref[0], sh_ref[0])
    h2_hi = h2.astype(BF16)
    h2_lo = (h2 - h2_hi.astype(F32)).astype(BF16)
    hpa_ref[...], hpb_ref[...] = _pack_rows(h2)

    wrh = wrh_ref[...]
    logits = (lax.dot_general(wrh, h2_hi, _NT, preferred_element_type=F32)
              + lax.dot_general(wrh, h2_lo, _NT, preferred_element_type=F32)
              + lax.dot_general(wrl_ref[...], h2_hi, _NT, preferred_element_type=F32))
    scores = _sigmoid(logits)
    sel = scores + bias_ref[...]

    neg = jnp.float32(-jnp.inf)
    s3 = sel.reshape(N_EXPERT_GROUPS, EXPERTS_PER_GROUP, tm)
    i3 = lax.broadcasted_iota(I32, s3.shape, 1)
    m1 = jnp.max(s3, axis=1, keepdims=True)
    i1 = jnp.min(jnp.where(s3 == m1, i3, EXPERTS_PER_GROUP), axis=1, keepdims=True)
    m2 = jnp.max(jnp.where(i3 == i1, neg, s3), axis=1)
    gs = m1[:, 0, :] + m2
    gi = lax.broadcasted_iota(I32, gs.shape, 0)
    keep = jnp.zeros(gs.shape, jnp.bool_)
    for _ in range(TOPK_GROUPS):
        m = jnp.max(gs, axis=0, keepdims=True)
        idx = jnp.min(jnp.where(gs == m, gi, N_EXPERT_GROUPS), axis=0, keepdims=True)
        hit = gi == idx
        keep = keep | hit
        gs = jnp.where(hit, neg, gs)
    keep3 = jnp.broadcast_to(keep[:, None, :], s3.shape)
    selm = jnp.where(keep3, s3, neg).reshape(N_EXPERTS, tm)

    ei = lax.broadcasted_iota(I32, selm.shape, 0)
    multi = jnp.zeros(selm.shape, F32)
    idxs, gates = [], []
    for _ in range(TOP_K):
        m = jnp.max(selm, axis=0, keepdims=True)
        idx = jnp.min(jnp.where(selm == m, ei, N_EXPERTS), axis=0, keepdims=True)
        hit = ei == idx
        gates.append(jnp.sum(jnp.where(hit, scores, 0.0), axis=0, keepdims=True))
        idxs.append(idx)
        selm = jnp.where(hit, neg, selm)
        multi = multi + hit.astype(F32)
    gate = jnp.concatenate(gates, axis=0)
    gate = gate / jnp.sum(gate, axis=0, keepdims=True) * ROUTED_SCALE
    te_ref[...] = jnp.concatenate(idxs, axis=0)
    gt_ref[...] = gate
    ones = jnp.ones((8, tm), BF16)
    cnt_ref[0] = lax.dot_general(ones, multi.astype(BF16), _NT, preferred_element_type=F32)


def _outproj_router(x, four, attn, g1, w_out, n2g, sc2, sh2, wr_hi, wr_lo, bias):
    b, l, d = x.shape
    tm = min(TOK_TILE, l)
    tpb = l // tm
    n = b * l
    const2 = lambda i, j: (0, 0)
    mod_map = lambda i, j: (i, 0, 0)
    tok_map = lambda i, j: (0, i * tpb + j)
    return pl.pallas_call(
        _outproj_router_kernel,
        grid=(b, tpb),
        in_specs=[pl.BlockSpec((1, tm, d), lambda i, j: (i, j, 0)),
                  pl.BlockSpec((1, tm, FOURIER_WIDTH), lambda i, j: (i, j, 0)),
                  pl.BlockSpec((1, KV_HEADS, tm, Q_PER_KV * HEAD_DIM), lambda i, j: (i, 0, j, 0)),
                  pl.BlockSpec((1, 1, d), mod_map),
                  pl.BlockSpec((d, d), const2),
                  pl.BlockSpec((1, d), const2),
                  pl.BlockSpec((1, 1, d), mod_map),
                  pl.BlockSpec((1, 1, d), mod_map),
                  pl.BlockSpec((N_EXPERTS, d), const2),
                  pl.BlockSpec((N_EXPERTS, d), const2),
                  pl.BlockSpec((N_EXPERTS, 1), const2)],
        out_specs=[pl.BlockSpec((1, tm, d), lambda i, j: (i, j, 0)),
                   pl.BlockSpec((tm, PACK_W), lambda i, j: (i * tpb + j, 0)),
                   pl.BlockSpec((tm, PACK_W), lambda i, j: (i * tpb + j, 0)),
                   pl.BlockSpec((TOP_K, tm), tok_map),
                   pl.BlockSpec((TOP_K, tm), tok_map),
                   pl.BlockSpec((1, 8, N_EXPERTS), lambda i, j: (i * tpb + j, 0, 0))],
        out_shape=[jax.ShapeDtypeStruct((b, l, d), F32),
                   jax.ShapeDtypeStruct((n, PACK_W), I32),
                   jax.ShapeDtypeStruct((n, PACK_W), I32),
                   jax.ShapeDtypeStruct((TOP_K, n), I32),
                   jax.ShapeDtypeStruct((TOP_K, n), F32),
                   jax.ShapeDtypeStruct((n // tm, 8, N_EXPERTS), F32)],
        compiler_params=_cp(("arbitrary", "arbitrary")),
        name="outproj_router",
    )(x, four, attn, g1, w_out, n2g, sc2, sh2, wr_hi, wr_lo, bias)


def _dest_kernel(te_ref, base_ref, tri_ref, d_ref):
    te = te_ref[...]
    tm = te.shape[1]
    ei = lax.broadcasted_iota(I32, (N_EXPERTS, tm), 0)
    hits = [ei == te[k:k + 1, :] for k in range(TOP_K)]
    multi = hits[0].astype(F32)
    for k in range(1, TOP_K):
        multi = multi + hits[k].astype(F32)
    rank = jnp.dot(multi.astype(BF16), tri_ref[...], preferred_element_type=F32)
    pos = rank + base_ref[0]
    rows = [jnp.sum(jnp.where(hits[k], pos, 0.0), axis=0, keepdims=True) for k in range(TOP_K)]
    d_ref[...] = jnp.concatenate(rows, axis=0).astype(I32)


def _dest_rows(top_e, base, tm):
    n = top_e.shape[1]
    tri = jnp.asarray(np.triu(np.ones((tm, tm), np.float32), 1)).astype(BF16)
    return pl.pallas_call(
        _dest_kernel,
        grid=(n // tm,),
        in_specs=[pl.BlockSpec((TOP_K, tm), lambda i: (0, i)),
                  pl.BlockSpec((1, N_EXPERTS, 1), lambda i: (i, 0, 0)),
                  pl.BlockSpec((tm, tm), lambda i: (0, 0))],
        out_specs=pl.BlockSpec((TOP_K, tm), lambda i: (0, i)),
        out_shape=jax.ShapeDtypeStruct((TOP_K, n), I32),
        compiler_params=_cp(("arbitrary",)),
        name="dest_rows",
    )(top_e, base, tri)


def _sc_mesh():
    return plsc.VectorSubcoreMesh(core_axis_name="c", subcore_axis_name="s",
                                  num_cores=SC_CORES, num_subcores=SC_SUBCORES)


def _sc_scatter_rows(src, dest, n_rows):
    n, w = src.shape
    kk = dest.shape[0]

    @pl.kernel(out_type=jax.ShapeDtypeStruct((n_rows, w), src.dtype), mesh=_sc_mesh(), name="sc_scatter_rows")
    def scatter(x_hbm, d_hbm, o_hbm):
        def body(x_vmem, d_vmem):
            for k in range(kk):
                pltpu.sync_copy(x_vmem, o_hbm.at[d_vmem.at[k]])

        pltpu.emit_pipeline(
            body,
            grid=(n // SC_WINDOW,),
            in_specs=[pl.BlockSpec((SC_WINDOW, w), lambda i: (i, 0)),
                      pl.BlockSpec((kk, SC_WINDOW), lambda i: (0, i))],
            out_specs=[],
            core_axis_name=("c", "s"),
            dimension_semantics=(pltpu.PARALLEL,),
        )(x_hbm, d_hbm)

    return scatter(src, dest)


def _swiglu(quarters, wgu, wdn):
    ag = None
    for j, q in enumerate(quarters):
        part = jnp.dot(q, wgu[j * PACK_W:(j + 1) * PACK_W, :], preferred_element_type=F32)
        ag = part if ag is None else ag + part
    ff = ag.shape[1] // 2
    a, g = ag[:, :ff], ag[:, ff:]
    mid = (a * _sigmoid(a) * g).astype(BF16)
    return jnp.dot(mid, wdn[...], preferred_element_type=F32)


def _expert_kernel(be_ref, nv_ref, xa_ref, xb_ref, wgu_ref, wdn_ref, ya_ref, yb_ref, wgu_s, wdn_s):
    i = pl.program_id(0)
    nvalid = nv_ref[i]

    @pl.when(nvalid > 0)
    def _():
        prev = be_ref[jnp.maximum(i - 1, 0)]

        @pl.when((i == 0) | (be_ref[i] != prev))
        def _():
            wgu_s[...] = wgu_ref[0].astype(BF16)
            wdn_s[...] = wdn_ref[0].astype(BF16)

        valid = lax.broadcasted_iota(I32, xa_ref.shape, 0) < nvalid
        xa = jnp.where(valid, xa_ref[...], 0)
        xb = jnp.where(valid, xb_ref[...], 0)
        quarters = [q.astype(BF16) for q in _unpack_rows(xa, xb)]
        ya_ref[...], yb_ref[...] = _pack_rows(_swiglu(quarters, wgu_s, wdn_s))

    @pl.when(nvalid == 0)
    def _():
        ya_ref[...] = jnp.zeros(ya_ref.shape, I32)
        yb_ref[...] = jnp.zeros(yb_ref.shape, I32)


def _experts(block_e, block_nvalid, xa, xb, w_gu, w_down):
    n_rows = xa.shape[0]
    rb = ROW_BLOCK
    d, ff2 = w_gu.shape[1], w_gu.shape[2]
    row_map = lambda i, be, nv: (i, 0)
    grid_spec = pltpu.PrefetchScalarGridSpec(
        num_scalar_prefetch=2,
        grid=(n_rows // rb,),
        in_specs=[pl.BlockSpec((rb, PACK_W), row_map),
                  pl.BlockSpec((rb, PACK_W), row_map),
                  pl.BlockSpec((1, d, ff2), lambda i, be, nv: (be[i], 0, 0)),
                  pl.BlockSpec((1, ff2 // 2, d), lambda i, be, nv: (be[i], 0, 0))],
        out_specs=[pl.BlockSpec((rb, PACK_W), row_map)] * 2,
        scratch_shapes=[pltpu.VMEM((d, ff2), BF16), pltpu.VMEM((ff2 // 2, d), BF16)],
    )
    return pl.pallas_call(
        _expert_kernel,
        grid_spec=grid_spec,
        out_shape=[jax.ShapeDtypeStruct((n_rows, PACK_W), I32)] * 2,
        compiler_params=_cp(("arbitrary",)),
        name="experts",
    )(block_e, block_nvalid, xa, xb, w_gu, w_down)


def _sc_gather_rows(table, idx):
    m = idx.shape[1]
    w = table.shape[1]

    @pl.kernel(out_type=jax.ShapeDtypeStruct((m, w), table.dtype), mesh=_sc_mesh(), name="sc_gather_rows")
    def gather(t_hbm, i_hbm, o_hbm):
        def body(i_vmem, o_vmem):
            pltpu.sync_copy(t_hbm.at[i_vmem.at[0]], o_vmem)

        pltpu.emit_pipeline(
            body,
            grid=(m // SC_WINDOW,),
            in_specs=[pl.BlockSpec((1, SC_WINDOW), lambda i: (0, i))],
            out_specs=[pl.BlockSpec((SC_WINDOW, w), lambda i: (i, 0))],
            core_axis_name=("c", "s"),
            dimension_semantics=(pltpu.PARALLEL,),
        )(i_hbm, o_hbm)

    return gather(table, idx)


def _combine_kernel(gate_ref, xn_ref, hpa_ref, hpb_ref, g2_ref, wsgu_ref, wsdn_ref, fng_ref,
                    yga_ref, ygb_ref, o_ref):
    quarters = [q.astype(BF16) for q in _unpack_rows(hpa_ref[...], hpb_ref[...])]
    shared = _swiglu(quarters, wsgu_ref, wsdn_ref)
    gate = gate_ref[...]
    acc = None
    for k in range(TOP_K):
        gk = gate[:, k:k + 1]
        rows = [gk * q for q in _unpack_rows(yga_ref[k], ygb_ref[k])]
        acc = rows if acc is None else [a + r for a, r in zip(acc, rows)]
    y = jnp.concatenate(acc, axis=1) + shared
    xo = xn_ref[...] + g2_ref[0] * y
    ms = jnp.mean(xo * xo, axis=-1, keepdims=True)
    o_ref[...] = xo * lax.rsqrt(ms + RMS_EPS) * fng_ref[...]


def _combine(dest, gate_t, xn, hpa, hpb, g2, ws_gu, ws_dn, fng, ya, yb, tokens_per_batch):
    n, d = xn.shape
    tm = min(COMB_TILE, tokens_per_batch)
    nt = n // tm
    tpb = tokens_per_batch // tm
    const2 = lambda i: (0, 0)
    row_map = lambda i: (i, 0)
    idx = dest.reshape(1, TOP_K * n)
    yga = _sc_gather_rows(ya, idx).reshape(TOP_K, n, PACK_W)
    ygb = _sc_gather_rows(yb, idx).reshape(TOP_K, n, PACK_W)
    return pl.pallas_call(
        _combine_kernel,
        grid=(nt,),
        in_specs=[pl.BlockSpec((tm, TOP_K), row_map),
                  pl.BlockSpec((tm, d), row_map),
                  pl.BlockSpec((tm, PACK_W), row_map),
                  pl.BlockSpec((tm, PACK_W), row_map),
                  pl.BlockSpec((1, 1, d), lambda i: (i // tpb, 0, 0)),
                  pl.BlockSpec(ws_gu.shape, const2),
                  pl.BlockSpec(ws_dn.shape, const2),
                  pl.BlockSpec((1, d), const2),
                  pl.BlockSpec((TOP_K, tm, PACK_W), lambda i: (0, i, 0)),
                  pl.BlockSpec((TOP_K, tm, PACK_W), lambda i: (0, i, 0))],
        out_specs=pl.BlockSpec((tm, d), row_map),
        out_shape=jax.ShapeDtypeStruct((n, d), F32),
        compiler_params=_cp(("arbitrary",)),
        name="combine",
    )(gate_t, xn, hpa, hpb, g2, ws_gu, ws_dn, fng, yga, ygb)


def _rope_tables(l):
    rows = l // GRID_W
    row = np.repeat(np.arange(rows, dtype=np.float32), GRID_W)
    col = np.tile(np.arange(GRID_W, dtype=np.float32), rows)
    n_freq = AXIS_ROT // 2
    inv_freq = (np.float32(ROPE_THETA) ** (-np.arange(n_freq, dtype=np.float32) / n_freq)).astype(np.float32)
    ang_r = row[:, None] * inv_freq
    ang_c = col[:, None] * inv_freq
    ang = np.concatenate([ang_r, ang_r, ang_c, ang_c], axis=-1).astype(np.float64)
    cos, sin = np.cos(ang), np.sin(ang)
    lane = np.arange(HEAD_DIM) % AXIS_ROT
    first = lane < AXIS_ROT // 2
    sin_a = np.where(first[None, :], -sin, 0.0)
    sin_b = np.where(first[None, :], 0.0, sin)
    two = lambda a: jnp.asarray(np.concatenate([a, a], axis=1), dtype=F32)
    return two(cos), two(sin_a), two(sin_b)


def _layer(x, ctx, c, c_ctx, lw, moe_w, fng):
    norm1_g, w_ada, b_ada, w_in, w_fourier, q_norm_g, k_norm_g, w_out, norm2_g = lw
    w_router, router_bias, w_expert_gu, w_expert_down, w_shared_gu, w_shared_down = moe_w
    b, l, d = x.shape
    n = b * l

    cond = jnp.concatenate([c, c_ctx[None, :], jnp.zeros((8 - b - 1, d), F32)], axis=0)
    mod = _adaln(cond, w_ada, b_ada)
    sh1, sc1, g1, sh2, sc2, g2 = [m[:b, None, :] for m in jnp.split(mod, 6, axis=-1)]
    csh1, csc1 = [m[b:b + 1, None, :] for m in jnp.split(mod, 6, axis=-1)[:2]]

    w_in_bf = w_in.astype(BF16)
    two = lambda g, s: jnp.tile(g * s, 2).reshape(1, LANES)
    gq, gk = two(q_norm_g, ATTN_SCALE * float(np.log2(np.e))), two(k_norm_g, 1.0)
    ones_bd = jnp.asarray(np.kron(np.eye(2), np.ones((HEAD_DIM, HEAD_DIM))), dtype=F32).astype(BF16)
    n1g = norm1_g.reshape(1, d)

    kc, vct = _inproj(ctx, csc1, csh1, n1g, w_in_bf[:, KV_COL0:], gq, gk, ones_bd, None, False)
    u, qt, k, vt = _inproj(x, sc1, sh1, n1g, w_in_bf, gq, gk, ones_bd, _rope_tables(l), True)

    w_bd = jnp.zeros((FOURIER_WIDTH, FOURIER_WIDTH), F32)
    for h in range(FOURIER_HEADS):
        w_bd = lax.dynamic_update_slice(w_bd, w_fourier[h], (h * HEAD_DIM, h * HEAD_DIM))
    four = _fourier(u, w_bd.astype(BF16))
    attn = _attention(qt, k, vt, kc, vct)

    wr_t = w_router.T
    wr_hi = wr_t.astype(BF16)
    wr_lo = (wr_t - wr_hi.astype(F32)).astype(BF16)
    xn, hpa, hpb, top_e, gate, cnt = _outproj_router(
        x, four, attn, g1, w_out.astype(BF16), norm2_g.reshape(1, d), sc2, sh2,
        wr_hi, wr_lo, router_bias.reshape(N_EXPERTS, 1))

    tm = min(TOK_TILE, l)
    rb = ROW_BLOCK
    counts = cnt[:, 0, :].astype(I32)
    total = jnp.sum(counts, axis=0)
    padded = (total + rb - 1) // rb * rb
    pad_end = jnp.cumsum(padded)
    pad_start = pad_end - padded
    base = pad_start[None, :] + jnp.cumsum(counts, axis=0) - counts
    n_blocks = n * TOP_K // rb + N_EXPERTS
    n_used = pad_end[-1] // rb
    blk_row = jnp.minimum(jnp.arange(n_blocks, dtype=I32), n_used - 1) * rb
    block_e = jnp.sum((pad_end[None, :] <= blk_row[:, None]).astype(I32), axis=1)
    block_e = jnp.minimum(block_e, N_EXPERTS - 1)
    row_end = (pad_start + total)[block_e]
    block_nvalid = jnp.where(jnp.arange(n_blocks) < n_used, jnp.clip(row_end - blk_row, 0, rb), 0).astype(I32)

    dest = _dest_rows(top_e, base.astype(F32)[:, :, None], tm)
    xa = _sc_scatter_rows(hpa, dest, n_blocks * rb)
    xb = _sc_scatter_rows(hpb, dest, n_blocks * rb)
    ya, yb = _experts(block_e, block_nvalid, xa, xb, w_expert_gu, w_expert_down)
    return _combine(dest, gate.T, xn.reshape(n, d), hpa, hpb, g2, w_shared_gu.astype(BF16),
                    w_shared_down.astype(BF16), fng, ya, yb, l)


def kernel(x, c, ctx, c_ctx, norm1_g, w_ada, b_ada, w_in, w_fourier, q_norm_g, k_norm_g, w_out, norm2_g,
           w_router, router_bias, w_expert_gu, w_expert_down, w_shared_gu, w_shared_down, final_norm_g):
    depth = norm1_g.shape[0]
    assert depth == 1, "context update between layers is not implemented"
    b, l, d = x.shape
    lw = (norm1_g[0], w_ada[0], b_ada[0], w_in[0], w_fourier[0], q_norm_g[0], k_norm_g[0], w_out[0], norm2_g[0])
    moe_w = (w_router[0], router_bias[0], w_expert_gu[0], w_expert_down[0], w_shared_gu[0], w_shared_down[0])
    out = _layer(x, ctx, c, c_ctx, lw, moe_w, final_norm_g.reshape(1, d))
    return out.reshape(b, l, d)
```

```python
import functools

import numpy as np
import jax
import jax.numpy as jnp
from jax import lax
from jax.experimental import pallas as pl
from jax.experimental.pallas import tpu as pltpu
from jax.experimental.pallas import tpu_sc as plsc

F32 = jnp.float32
BF16 = jnp.bfloat16
I32 = jnp.int32

D_MODEL = 1024
GRID_W = 64
HEAD_DIM = 64
FOURIER_HEADS = 4
FOURIER_WIDTH = 256
ATTN_HEADS = 12
KV_HEADS = 4
Q_PER_KV = 3
ATTN_WIDTH = 768
KV_WIDTH = 256
KV_COL0 = 1024
IN_WIDTH = 1536
ATTN_SCALE = HEAD_DIM ** -0.5
ROPE_THETA = 10000.0
AXIS_ROT = HEAD_DIM // 2
N_EXPERTS = 256
TOP_K = 8
N_EXPERT_GROUPS = 8
TOPK_GROUPS = 4
EXPERTS_PER_GROUP = 32
EXPERT_FF = 256
ROUTED_SCALE = 2.5
RMS_EPS = 1e-6

LANES = 128
PACK_W = D_MODEL // 4
DFT_L1 = 64
VMEM_LIMIT = 48 * 1024 * 1024

TOK_TILE = 512
ATTN_BQ = 256
ATTN_BK = 1024
QK_SPLIT = 2
ONES_ROWS = 16
ROW_BLOCK = 512
COMB_TILE = 256
SC_WINDOW = 128
SC_CORES = 2
SC_SUBCORES = 16

_HI = lax.Precision.HIGHEST
_NT = (((1,), (1,)), ((), ()))


def _cp(sem, vmem=VMEM_LIMIT):
    return pltpu.CompilerParams(dimension_semantics=sem, vmem_limit_bytes=vmem)


def _sigmoid(v):
    return 1.0 / (1.0 + jnp.exp(-v))


def _pack_rows(v):
    w = v.shape[1] // 4

    def pack(lo, hi):
        lo = lax.bitcast_convert_type(lo.astype(BF16).astype(F32), I32)
        hi = lax.bitcast_convert_type(hi.astype(BF16).astype(F32), I32)
        return lax.shift_right_logical(lo, 16) | (hi & jnp.int32(-65536))

    return pack(v[:, :w], v[:, 2 * w:3 * w]), pack(v[:, w:2 * w], v[:, 3 * w:])


def _unpack_rows(a, b):
    lo = lambda t: lax.bitcast_convert_type(lax.shift_left(t, 16), F32)
    hi = lambda t: lax.bitcast_convert_type(t & jnp.int32(-65536), F32)
    return [lo(a), lo(b), hi(a), hi(b)]


def _adaln_kernel(c_ref, w_ref, b_ref, o_ref):
    c = c_ref[...]
    s = c * _sigmoid(c)
    o_ref[...] = jnp.dot(s, w_ref[...], precision=_HI, preferred_element_type=F32) + b_ref[...]


def _adaln(cond, w_ada, b_ada):
    rows, d = cond.shape
    n = w_ada.shape[1]
    tn = 1536
    return pl.pallas_call(
        _adaln_kernel,
        grid=(n // tn,),
        in_specs=[pl.BlockSpec((rows, d), lambda j: (0, 0)),
                  pl.BlockSpec((d, tn), lambda j: (0, j)),
                  pl.BlockSpec((1, tn), lambda j: (0, j))],
        out_specs=pl.BlockSpec((rows, tn), lambda j: (0, j)),
        out_shape=jax.ShapeDtypeStruct((rows, n), F32),
        compiler_params=_cp(("arbitrary",)),
        name="adaln",
    )(cond, w_ada, b_ada.reshape(1, n))


def _modulated_norm(x, g, sc, sh):
    ms = jnp.mean(x * x, axis=-1, keepdims=True)
    return x * lax.rsqrt(ms + RMS_EPS) * g * (1.0 + sc) + sh


def _head_norm(chunk, gain, ones_bd):
    sq = chunk * chunk
    hi = sq.astype(BF16)
    lo = (sq - hi.astype(F32)).astype(BF16)
    ss = (jnp.dot(hi, ones_bd, preferred_element_type=F32)
          + jnp.dot(lo, ones_bd, preferred_element_type=F32))
    return chunk * lax.rsqrt(ss * (1.0 / HEAD_DIM) + RMS_EPS) * gain


def _rope(t, cos, sin_a, sin_b):
    return (t * cos + pltpu.roll(t, LANES - AXIS_ROT // 2, 1) * sin_a
            + pltpu.roll(t, AXIS_ROT // 2, 1) * sin_b)


def _inproj_kernel(*refs, rope, with_uq):
    if rope:
        (x_ref, sc_ref, sh_ref, g_ref, w_ref, gq_ref, gk_ref, ones_ref,
         cos_ref, sa_ref, sb_ref) = refs[:11]
        outs = refs[11:]
    else:
        x_ref, sc_ref, sh_ref, g_ref, w_ref, gq_ref, gk_ref, ones_ref = refs[:8]
        outs = refs[8:]
    if with_uq:
        u_ref, qt_ref, k_ref, vt_ref = outs
    else:
        k_ref, vt_ref = outs

    h = _modulated_norm(x_ref[0], g_ref[...], sc_ref[0], sh_ref[0])
    z = jnp.dot(h.astype(BF16), w_ref[...], preferred_element_type=F32)
    ones_bd = ones_ref[...]

    def normed(chunk, gain):
        t = _head_norm(chunk, gain, ones_bd)
        if rope:
            t = _rope(t, cos_ref[...], sa_ref[...], sb_ref[...])
        return t

    col = 0
    if with_uq:
        u_ref[0] = z[:, :FOURIER_WIDTH].astype(BF16)
        col = FOURIER_WIDTH
        qs = [normed(z[:, col + j * LANES: col + (j + 1) * LANES], gq_ref[...])
              for j in range(ATTN_WIDTH // LANES)]
        qt_ref[0] = jnp.concatenate(qs, axis=1).T.astype(BF16)
        col += ATTN_WIDTH
    for j in range(KV_WIDTH // LANES):
        t = normed(z[:, col + j * LANES: col + (j + 1) * LANES], gk_ref[...])
        k_ref[0, 2 * j] = t[:, :HEAD_DIM].astype(BF16)
        k_ref[0, 2 * j + 1] = t[:, HEAD_DIM:].astype(BF16)
    col += KV_WIDTH
    vt_ref[0] = z[:, col:col + KV_WIDTH].T.astype(BF16)


def _inproj(x, sc, sh, g, w, gq, gk, ones_bd, rope_tabs, with_uq):
    b, t, d = x.shape
    tm = min(TOK_TILE, t)
    wcols = w.shape[1]
    bm = sc.shape[0]
    mod_map = (lambda i, j: (i, 0, 0)) if bm == b else (lambda i, j: (0, 0, 0))
    const2 = lambda i, j: (0, 0)
    in_specs = [pl.BlockSpec((1, tm, d), lambda i, j: (i, j, 0)),
                pl.BlockSpec((1, 1, d), mod_map),
                pl.BlockSpec((1, 1, d), mod_map),
                pl.BlockSpec((1, d), const2),
                pl.BlockSpec((d, wcols), const2),
                pl.BlockSpec((1, LANES), const2),
                pl.BlockSpec((1, LANES), const2),
                pl.BlockSpec((LANES, LANES), const2)]
    args = [x, sc, sh, g, w, gq, gk, ones_bd]
    rope = rope_tabs is not None
    if rope:
        in_specs += [pl.BlockSpec((tm, LANES), lambda i, j: (j, 0))] * 3
        args += list(rope_tabs)
    out_specs, out_shape = [], []
    if with_uq:
        out_specs += [pl.BlockSpec((1, tm, FOURIER_WIDTH), lambda i, j: (i, j, 0)),
                      pl.BlockSpec((1, ATTN_WIDTH, tm), lambda i, j: (i, 0, j))]
        out_shape += [jax.ShapeDtypeStruct((b, t, FOURIER_WIDTH), BF16),
                      jax.ShapeDtypeStruct((b, ATTN_WIDTH, t), BF16)]
    out_specs += [pl.BlockSpec((1, KV_HEADS, tm, HEAD_DIM), lambda i, j: (i, 0, j, 0)),
                  pl.BlockSpec((1, KV_WIDTH, tm), lambda i, j: (i, 0, j))]
    out_shape += [jax.ShapeDtypeStruct((b, KV_HEADS, t, HEAD_DIM), BF16),
                  jax.ShapeDtypeStruct((b, KV_WIDTH, t), BF16)]
    return pl.pallas_call(
        functools.partial(_inproj_kernel, rope=rope, with_uq=with_uq),
        grid=(b, t // tm),
        in_specs=in_specs,
        out_specs=out_specs,
        out_shape=out_shape,
        compiler_params=_cp(("arbitrary", "arbitrary")),
        name="inproj_latent" if with_uq else "inproj_ctx",
    )(*args)


def _fourier_a_kernel(u_ref, c_ref, s_ref, yr_ref, yi_ref):
    u = u_ref[0]
    yr_ref[0] = jnp.dot(c_ref[...], u, preferred_element_type=F32).astype(BF16)
    yi_ref[0] = jnp.dot(s_ref[...], u, preferred_element_type=F32).astype(BF16)


def _fourier_b_kernel(yr_ref, yi_ref, m_ref, c_ref, s_ref, w_ref, o_ref, *, kb):
    y = jnp.concatenate([yr_ref[0], yi_ref[0]], axis=1)
    x = jnp.einsum("kab,kbc->kac", m_ref[...], y, preferred_element_type=F32)
    xr = x[:, :DFT_L1].reshape(kb * DFT_L1, FOURIER_WIDTH).astype(BF16)
    xi = x[:, DFT_L1:].reshape(kb * DFT_L1, FOURIER_WIDTH).astype(BF16)
    spec = (jnp.dot(xr, c_ref[...], preferred_element_type=F32)
            + jnp.dot(xi, s_ref[...], preferred_element_type=F32))
    o = jnp.dot(spec.astype(BF16), w_ref[...], preferred_element_type=F32)
    for j in range(kb):
        o_ref[0, :, j, :] = o[j * DFT_L1:(j + 1) * DFT_L1]


def _dft_tables(l):
    l2 = l // DFT_L1
    k2 = np.arange(l2)
    ang2 = 2.0 * np.pi * ((k2[:, None] * k2[None, :]) % l2) / l2
    c2 = np.cos(ang2)
    s2n = -np.sin(ang2)
    n1 = np.arange(DFT_L1)
    k = l2 * n1[None, :, None] + k2[:, None, None]
    ang = 2.0 * np.pi * ((k * n1[None, None, :]) % l) / l
    mr, mi = np.cos(ang), -np.sin(ang)
    m = np.concatenate([np.concatenate([mr, -mi], axis=2),
                        np.concatenate([mi, mr], axis=2)], axis=1)
    d = np.arange(HEAD_DIM)
    angc = 2.0 * np.pi * ((d[:, None] * d[None, :]) % HEAD_DIM) / HEAD_DIM
    scale = 1.0 / np.sqrt(float(l) * HEAD_DIM)
    eye = np.eye(FOURIER_HEADS)
    cbd = np.kron(eye, np.cos(angc) * scale)
    sbd = np.kron(eye, np.sin(angc) * scale)
    as_bf = lambda a: jnp.asarray(a, dtype=F32).astype(BF16)
    return as_bf(c2), as_bf(s2n), as_bf(m), as_bf(cbd), as_bf(sbd)


def _fourier(u, w_bd):
    b, l, fw = u.shape
    l2 = l // DFT_L1
    ncol = DFT_L1 * fw
    tn = min(4096, ncol)
    kb = min(8, l2)
    c2, s2n, m, cbd, sbd = _dft_tables(l)
    yr, yi = pl.pallas_call(
        _fourier_a_kernel,
        grid=(b, ncol // tn),
        in_specs=[pl.BlockSpec((1, l2, tn), lambda i, j: (i, 0, j)),
                  pl.BlockSpec((l2, l2), lambda i, j: (0, 0)),
                  pl.BlockSpec((l2, l2), lambda i, j: (0, 0))],
        out_specs=[pl.BlockSpec((1, l2, tn), lambda i, j: (i, 0, j))] * 2,
        out_shape=[jax.ShapeDtypeStruct((b, l2, ncol), BF16)] * 2,
        compiler_params=_cp(("arbitrary", "arbitrary")),
        name="fourier_a",
    )(u.reshape(b, l2, ncol), c2, s2n)
    yr = yr.reshape(b, l2, DFT_L1, fw)
    yi = yi.reshape(b, l2, DFT_L1, fw)
    out = pl.pallas_call(
        functools.partial(_fourier_b_kernel, kb=kb),
        grid=(b, l2 // kb),
        in_specs=[pl.BlockSpec((1, kb, DFT_L1, fw), lambda i, j: (i, j, 0, 0)),
                  pl.BlockSpec((1, kb, DFT_L1, fw), lambda i, j: (i, j, 0, 0)),
                  pl.BlockSpec((kb, 2 * DFT_L1, 2 * DFT_L1), lambda i, j: (j, 0, 0)),
                  pl.BlockSpec((fw, fw), lambda i, j: (0, 0)),
                  pl.BlockSpec((fw, fw), lambda i, j: (0, 0)),
                  pl.BlockSpec((fw, fw), lambda i, j: (0, 0))],
        out_specs=pl.BlockSpec((1, DFT_L1, kb, fw), lambda i, j: (i, 0, j, 0)),
        out_shape=jax.ShapeDtypeStruct((b, DFT_L1, l2, fw), F32),
        compiler_params=_cp(("arbitrary", "arbitrary")),
        name="fourier_b",
    )(yr, yi, m, cbd, sbd, w_bd)
    return out.reshape(b, l, fw)


def _attn_kernel(qt_ref, k_ref, vt_ref, kc_ref, vct_ref, o_ref, m_scr, acc_scr, *s_scrs, bk, nchunks):
    bq = qt_ref.shape[2]
    m_scr[...] = jnp.full(m_scr.shape, -jnp.inf, F32)
    acc_scr[...] = jnp.zeros(acc_scr.shape, F32)

    def scores(k, buf, h):
        nk = k.shape[0]
        qt = qt_ref[0, h * HEAD_DIM:(h + 1) * HEAD_DIM, :]
        part = nk // QK_SPLIT
        for j in range(QK_SPLIT):
            s_scrs[buf][j * part:(j + 1) * part, :] = jnp.dot(k[j * part:(j + 1) * part], qt,
                                                              preferred_element_type=F32)

    def softmax_pv(buf, h, vt):
        nk = vt.shape[1]
        s = s_scrs[buf][:nk, :]
        m_old = m_scr[h]
        m_new = jnp.maximum(m_old, jnp.max(s, axis=0, keepdims=True))
        alpha = jnp.exp2(m_old - m_new)
        p = jnp.exp2((s - m_new).astype(BF16))
        vt_ones = jnp.concatenate([vt, jnp.ones((ONES_ROWS, nk), BF16)], axis=0)
        acc_scr[h] = alpha * acc_scr[h] + jnp.dot(vt_ones, p, preferred_element_type=F32)
        m_scr[h] = m_new

    def keys(c):
        if isinstance(c, int) and c == nchunks:
            return kc_ref[0, 0]
        return k_ref[0, 0, pl.ds(pl.multiple_of(c * bk, bk), bk), :]

    def values(c):
        if isinstance(c, int) and c == nchunks:
            return vct_ref[0]
        return vt_ref[0, :, pl.ds(pl.multiple_of(c * bk, bk), bk)]

    nbuf = len(s_scrs)
    pair = 1
    ahead = nbuf - pair
    n_units = Q_PER_KV * (nchunks + 1)

    def group(us, chunk0=0):
        for u in us:
            ua = u + ahead
            if not (isinstance(chunk0, int) and ua >= n_units):
                scores(keys(chunk0 + ua // Q_PER_KV), ua % nbuf, ua % Q_PER_KV)
        for u in us:
            softmax_pv(u % nbuf, u % Q_PER_KV, values(chunk0 + u // Q_PER_KV))

    for u in range(min(ahead, n_units)):
        scores(keys(u // Q_PER_KV), u % nbuf, u % Q_PER_KV)

    chunks_per_iter = nbuf // Q_PER_KV
    n_iter = max(0, (Q_PER_KV * nchunks - ahead) // nbuf)

    def body(j, carry):
        for t in range(0, nbuf, pair):
            group(range(t, t + pair), j * chunks_per_iter)
        return carry

    lax.fori_loop(0, n_iter, body, 0)
    for t in range(n_iter * nbuf, n_units, pair):
        group(range(t, min(t + pair, n_units)))
    outs = [acc_scr[h, :HEAD_DIM, :] / acc_scr[h, HEAD_DIM:HEAD_DIM + 1, :] for h in range(Q_PER_KV)]
    o_t = jnp.concatenate(outs + [jnp.zeros((HEAD_DIM, bq), F32)], axis=0)
    o_ref[0, 0] = o_t.T[:, :Q_PER_KV * HEAD_DIM].astype(BF16)


def _attention(qt, k, vt, kc, vct):
    b, _, l = qt.shape
    c = kc.shape[2]
    bq = min(ATTN_BQ, l)
    bk = min(ATTN_BK, l)
    gw = Q_PER_KV * HEAD_DIM
    return pl.pallas_call(
        functools.partial(_attn_kernel, bk=bk, nchunks=l // bk),
        grid=(b, KV_HEADS, l // bq),
        in_specs=[pl.BlockSpec((1, gw, bq), lambda i, g, j: (i, g, j)),
                  pl.BlockSpec((1, 1, l, HEAD_DIM), lambda i, g, j: (i, g, 0, 0)),
                  pl.BlockSpec((1, HEAD_DIM, l), lambda i, g, j: (i, g, 0)),
                  pl.BlockSpec((1, 1, c, HEAD_DIM), lambda i, g, j: (i, g, 0, 0)),
                  pl.BlockSpec((1, HEAD_DIM, c), lambda i, g, j: (i, g, 0))],
        out_specs=pl.BlockSpec((1, 1, bq, gw), lambda i, g, j: (i, g, j, 0)),
        out_shape=jax.ShapeDtypeStruct((b, KV_HEADS, l, gw), BF16),
        scratch_shapes=[pltpu.VMEM((Q_PER_KV, 1, bq), F32),
                        pltpu.VMEM((Q_PER_KV, HEAD_DIM + ONES_ROWS, bq), F32),
                        *[pltpu.VMEM((max(bk, c), bq), F32)] * (2 * Q_PER_KV)],
        compiler_params=_cp(("arbitrary", "arbitrary", "arbitrary")),
        name="attention",
    )(qt, k, vt, kc, vct)


def _outproj_router_kernel(x_ref, f_ref, a_ref, g1_ref, wo_ref, n2g_ref, sc_ref, sh_ref,
                           wrh_ref, wrl_ref, bias_ref, xn_ref, hpa_ref, hpb_ref, te_ref, gt_ref, cnt_ref):
    tm = x_ref.shape[1]
    mix = jnp.dot(f_ref[0].astype(BF16), wo_ref[:FOURIER_WIDTH, :], preferred_element_type=F32)
    gw = Q_PER_KV * HEAD_DIM
    for g in range(KV_HEADS):
        r0 = FOURIER_WIDTH + g * gw
        mix += jnp.dot(a_ref[0, g], wo_ref[r0:r0 + gw, :], preferred_element_type=F32)
    xn = x_ref[0] + g1_ref[0] * mix
    xn_ref[0] = xn
    h2 = _modulated_norm(xn, n2g_ref[...], sc_ref[0], sh_ref[0])
    h2_hi = h2.astype(BF16)
    h2_lo = (h2 - h2_hi.astype(F32)).astype(BF16)
    hpa_ref[...], hpb_ref[...] = _pack_rows(h2)

    wrh = wrh_ref[...]
    logits = (lax.dot_general(wrh, h2_hi, _NT, preferred_element_type=F32)
              + lax.dot_general(wrh, h2_lo, _NT, preferred_element_type=F32)
              + lax.dot_general(wrl_ref[...], h2_hi, _NT, preferred_element_type=F32))
    scores = _sigmoid(logits)
    sel = scores + bias_ref[...]

    neg = jnp.float32(-jnp.inf)
    s3 = sel.reshape(N_EXPERT_GROUPS, EXPERTS_PER_GROUP, tm)
    i3 = lax.broadcasted_iota(I32, s3.shape, 1)
    m1 = jnp.max(s3, axis=1, keepdims=True)
    i1 = jnp.min(jnp.where(s3 == m1, i3, EXPERTS_PER_GROUP), axis=1, keepdims=True)
    m2 = jnp.max(jnp.where(i3 == i1, neg, s3), axis=1)
    gs = m1[:, 0, :] + m2
    gi = lax.broadcasted_iota(I32, gs.shape, 0)
    keep = jnp.zeros(gs.shape, jnp.bool_)
    for _ in range(TOPK_GROUPS):
        m = jnp.max(gs, axis=0, keepdims=True)
        idx = jnp.min(jnp.where(gs == m, gi, N_EXPERT_GROUPS), axis=0, keepdims=True)
        hit = gi == idx
        keep = keep | hit
        gs = jnp.where(hit, neg, gs)
    keep3 = jnp.broadcast_to(keep[:, None, :], s3.shape)
    selm = jnp.where(keep3, s3, neg).reshape(N_EXPERTS, tm)

    ei = lax.broadcasted_iota(I32, selm.shape, 0)
    multi = jnp.zeros(selm.shape, F32)
    idxs, gates = [], []
    for _ in range(TOP_K):
        m = jnp.max(selm, axis=0, keepdims=True)
        idx = jnp.min(jnp.where(selm == m, ei, N_EXPERTS), axis=0, keepdims=True)
        hit = ei == idx
        gates.append(jnp.sum(jnp.where(hit, scores, 0.0), axis=0, keepdims=True))
        idxs.append(idx)
        selm = jnp.where(hit, neg, selm)
        multi = multi + hit.astype(F32)
    gate = jnp.concatenate(gates, axis=0)
    gate = gate / jnp.sum(gate, axis=0, keepdims=True) * ROUTED_SCALE
    te_ref[...] = jnp.concatenate(idxs, axis=0)
    gt_ref[...] = gate
    ones = jnp.ones((8, tm), BF16)
    cnt_ref[0] = lax.dot_general(ones, multi.astype(BF16), _NT, preferred_element_type=F32)


def _outproj_router(x, four, attn, g1, w_out, n2g, sc2, sh2, wr_hi, wr_lo, bias):
    b, l, d = x.shape
    tm = min(TOK_TILE, l)
    tpb = l // tm
    n = b * l
    const2 = lambda i, j: (0, 0)
    mod_map = lambda i, j: (i, 0, 0)
    tok_map = lambda i, j: (0, i * tpb + j)
    return pl.pallas_call(
        _outproj_router_kernel,
        grid=(b, tpb),
        in_specs=[pl.BlockSpec((1, tm, d), lambda i, j: (i, j, 0)),
                  pl.BlockSpec((1, tm, FOURIER_WIDTH), lambda i, j: (i, j, 0)),
                  pl.BlockSpec((1, KV_HEADS, tm, Q_PER_KV * HEAD_DIM), lambda i, j: (i, 0, j, 0)),
                  pl.BlockSpec((1, 1, d), mod_map),
                  pl.BlockSpec((d, d), const2),
                  pl.BlockSpec((1, d), const2),
                  pl.BlockSpec((1, 1, d), mod_map),
                  pl.BlockSpec((1, 1, d), mod_map),
                  pl.BlockSpec((N_EXPERTS, d), const2),
                  pl.BlockSpec((N_EXPERTS, d), const2),
                  pl.BlockSpec((N_EXPERTS, 1), const2)],
        out_specs=[pl.BlockSpec((1, tm, d), lambda i, j: (i, j, 0)),
                   pl.BlockSpec((tm, PACK_W), lambda i, j: (i * tpb + j, 0)),
                   pl.BlockSpec((tm, PACK_W), lambda i, j: (i * tpb + j, 0)),
                   pl.BlockSpec((TOP_K, tm), tok_map),
                   pl.BlockSpec((TOP_K, tm), tok_map),
                   pl.BlockSpec((1, 8, N_EXPERTS), lambda i, j: (i * tpb + j, 0, 0))],
        out_shape=[jax.ShapeDtypeStruct((b, l, d), F32),
                   jax.ShapeDtypeStruct((n, PACK_W), I32),
                   jax.ShapeDtypeStruct((n, PACK_W), I32),
                   jax.ShapeDtypeStruct((TOP_K, n), I32),
                   jax.ShapeDtypeStruct((TOP_K, n), F32),
                   jax.ShapeDtypeStruct((n // tm, 8, N_EXPERTS), F32)],
        compiler_params=_cp(("arbitrary", "arbitrary")),
        name="outproj_router",
    )(x, four, attn, g1, w_out, n2g, sc2, sh2, wr_hi, wr_lo, bias)


def _dest_kernel(te_ref, base_ref, tri_ref, d_ref):
    te = te_ref[...]
    tm = te.shape[1]
    ei = lax.broadcasted_iota(I32, (N_EXPERTS, tm), 0)
    hits = [ei == te[k:k + 1, :] for k in range(TOP_K)]
    multi = hits[0].astype(F32)
    for k in range(1, TOP_K):
        multi = multi + hits[k].astype(F32)
    rank = jnp.dot(multi.astype(BF16), tri_ref[...], preferred_element_type=F32)
    pos = rank + base_ref[0]
    rows = [jnp.sum(jnp.where(hits[k], pos, 0.0), axis=0, keepdims=True) for k in range(TOP_K)]
    d_ref[...] = jnp.concatenate(rows, axis=0).astype(I32)


def _dest_rows(top_e, base, tm):
    n = top_e.shape[1]
    tri = jnp.asarray(np.triu(np.ones((tm, tm), np.float32), 1)).astype(BF16)
    return pl.pallas_call(
        _dest_kernel,
        grid=(n // tm,),
        in_specs=[pl.BlockSpec((TOP_K, tm), lambda i: (0, i)),
                  pl.BlockSpec((1, N_EXPERTS, 1), lambda i: (i, 0, 0)),
                  pl.BlockSpec((tm, tm), lambda i: (0, 0))],
        out_specs=pl.BlockSpec((TOP_K, tm), lambda i: (0, i)),
        out_shape=jax.ShapeDtypeStruct((TOP_K, n), I32),
        compiler_params=_cp(("arbitrary",)),
        name="dest_rows",
    )(top_e, base, tri)


def _sc_mesh():
    return plsc.VectorSubcoreMesh(core_axis_name="c", subcore_axis_name="s",
                                  num_cores=SC_CORES, num_subcores=SC_SUBCORES)


def _sc_scatter_rows(src, dest, n_rows):
    n, w = src.shape
    kk = dest.shape[0]

    @pl.kernel(out_type=jax.ShapeDtypeStruct((n_rows, w), src.dtype), mesh=_sc_mesh(), name="sc_scatter_rows")
    def scatter(x_hbm, d_hbm, o_hbm):
        def body(x_vmem, d_vmem):
            for k in range(kk):
                pltpu.sync_copy(x_vmem, o_hbm.at[d_vmem.at[k]])

        pltpu.emit_pipeline(
            body,
            grid=(n // SC_WINDOW,),
            in_specs=[pl.BlockSpec((SC_WINDOW, w), lambda i: (i, 0)),
                      pl.BlockSpec((kk, SC_WINDOW), lambda i: (0, i))],
            out_specs=[],
            core_axis_name=("c", "s"),
            dimension_semantics=(pltpu.PARALLEL,),
        )(x_hbm, d_hbm)

    return scatter(src, dest)


def _swiglu(quarters, wgu, wdn):
    ag = None
    for j, q in enumerate(quarters):
        part = jnp.dot(q, wgu[j * PACK_W:(j + 1) * PACK_W, :], preferred_element_type=F32)
        ag = part if ag is None else ag + part
    ff = ag.shape[1] // 2
    a, g = ag[:, :ff], ag[:, ff:]
    mid = (a * _sigmoid(a) * g).astype(BF16)
    return jnp.dot(mid, wdn[...], preferred_element_type=F32)


def _expert_kernel(be_ref, nv_ref, nu_ref, xa_ref, xb_ref, wgu_ref, wdn_ref, ya_ref, yb_ref, wgu_s, wdn_s):
    del nu_ref
    i = pl.program_id(0)
    nvalid = nv_ref[i]

    @pl.when(nvalid > 0)
    def _():
        prev = be_ref[jnp.maximum(i - 1, 0)]

        @pl.when((i == 0) | (be_ref[i] != prev))
        def _():
            wgu_s[...] = wgu_ref[0].astype(BF16)
            wdn_s[...] = wdn_ref[0].astype(BF16)

        valid = lax.broadcasted_iota(I32, xa_ref.shape, 0) < nvalid
        xa = jnp.where(valid, xa_ref[...], 0)
        xb = jnp.where(valid, xb_ref[...], 0)
        quarters = [q.astype(BF16) for q in _unpack_rows(xa, xb)]
        ya_ref[...], yb_ref[...] = _pack_rows(_swiglu(quarters, wgu_s, wdn_s))

    @pl.when(nvalid == 0)
    def _():
        ya_ref[...] = jnp.zeros(ya_ref.shape, I32)
        yb_ref[...] = jnp.zeros(yb_ref.shape, I32)


def _experts(block_e, block_nvalid, n_used, xa, xb, w_gu, w_down):
    n_rows = xa.shape[0]
    rb = ROW_BLOCK
    d, ff2 = w_gu.shape[1], w_gu.shape[2]
    out_map = lambda i, be, nv, nu: (i, 0)
    in_map = lambda i, be, nv, nu: (jnp.minimum(i, nu[0] - 1), 0)
    w_map = lambda i, be, nv, nu: (be[i], 0, 0)
    grid_spec = pltpu.PrefetchScalarGridSpec(
        num_scalar_prefetch=3,
        grid=(n_rows // rb,),
        in_specs=[pl.BlockSpec((rb, PACK_W), in_map),
                  pl.BlockSpec((rb, PACK_W), in_map),
                  pl.BlockSpec((1, d, ff2), w_map),
                  pl.BlockSpec((1, ff2 // 2, d), w_map)],
        out_specs=[pl.BlockSpec((rb, PACK_W), out_map)] * 2,
        scratch_shapes=[pltpu.VMEM((d, ff2), BF16), pltpu.VMEM((ff2 // 2, d), BF16)],
    )
    return pl.pallas_call(
        _expert_kernel,
        grid_spec=grid_spec,
        out_shape=[jax.ShapeDtypeStruct((n_rows, PACK_W), I32)] * 2,
        compiler_params=_cp(("arbitrary",)),
        name="experts",
    )(block_e, block_nvalid, n_used, xa, xb, w_gu, w_down)


def _sc_gather_rows(table, idx):
    m = idx.shape[1]
    w = table.shape[1]

    @pl.kernel(out_type=jax.ShapeDtypeStruct((m, w), table.dtype), mesh=_sc_mesh(), name="sc_gather_rows")
    def gather(t_hbm, i_hbm, o_hbm):
        def body(i_vmem, o_vmem):
            pltpu.sync_copy(t_hbm.at[i_vmem.at[0]], o_vmem)

        pltpu.emit_pipeline(
            body,
            grid=(m // SC_WINDOW,),
            in_specs=[pl.BlockSpec((1, SC_WINDOW), lambda i: (0, i))],
            out_specs=[pl.BlockSpec((SC_WINDOW, w), lambda i: (i, 0))],
            core_axis_name=("c", "s"),
            dimension_semantics=(pltpu.PARALLEL,),
        )(i_hbm, o_hbm)

    return gather(table, idx)


def _combine_kernel(gate_ref, xn_ref, hpa_ref, hpb_ref, g2_ref, wsgu_ref, wsdn_ref, fng_ref,
                    yga_ref, ygb_ref, o_ref):
    quarters = [q.astype(BF16) for q in _unpack_rows(hpa_ref[...], hpb_ref[...])]
    shared = _swiglu(quarters, wsgu_ref, wsdn_ref)
    gate = gate_ref[...]
    acc = None
    for k in range(TOP_K):
        gk = gate[:, k:k + 1]
        rows = [gk * q for q in _unpack_rows(yga_ref[k], ygb_ref[k])]
        acc = rows if acc is None else [a + r for a, r in zip(acc, rows)]
    y = jnp.concatenate(acc, axis=1) + shared
    xo = xn_ref[...] + g2_ref[0] * y
    ms = jnp.mean(xo * xo, axis=-1, keepdims=True)
    o_ref[...] = xo * lax.rsqrt(ms + RMS_EPS) * fng_ref[...]


def _combine(dest, gate_t, xn, hpa, hpb, g2, ws_gu, ws_dn, fng, ya, yb, tokens_per_batch):
    n, d = xn.shape
    tm = min(COMB_TILE, tokens_per_batch)
    nt = n // tm
    tpb = tokens_per_batch // tm
    const2 = lambda i: (0, 0)
    row_map = lambda i: (i, 0)
    idx = dest.reshape(1, TOP_K * n)
    yga = _sc_gather_rows(ya, idx).reshape(TOP_K, n, PACK_W)
    ygb = _sc_gather_rows(yb, idx).reshape(TOP_K, n, PACK_W)
    return pl.pallas_call(
        _combine_kernel,
        grid=(nt,),
        in_specs=[pl.BlockSpec((tm, TOP_K), row_map),
                  pl.BlockSpec((tm, d), row_map),
                  pl.BlockSpec((tm, PACK_W), row_map),
                  pl.BlockSpec((tm, PACK_W), row_map),
                  pl.BlockSpec((1, 1, d), lambda i: (i // tpb, 0, 0)),
                  pl.BlockSpec(ws_gu.shape, const2),
                  pl.BlockSpec(ws_dn.shape, const2),
                  pl.BlockSpec((1, d), const2),
                  pl.BlockSpec((TOP_K, tm, PACK_W), lambda i: (0, i, 0)),
                  pl.BlockSpec((TOP_K, tm, PACK_W), lambda i: (0, i, 0))],
        out_specs=pl.BlockSpec((tm, d), row_map),
        out_shape=jax.ShapeDtypeStruct((n, d), F32),
        compiler_params=_cp(("arbitrary",)),
        name="combine",
    )(gate_t, xn, hpa, hpb, g2, ws_gu, ws_dn, fng, yga, ygb)


def _rope_tables(l):
    rows = l // GRID_W
    row = np.repeat(np.arange(rows, dtype=np.float32), GRID_W)
    col = np.tile(np.arange(GRID_W, dtype=np.float32), rows)
    n_freq = AXIS_ROT // 2
    inv_freq = (np.float32(ROPE_THETA) ** (-np.arange(n_freq, dtype=np.float32) / n_freq)).astype(np.float32)
    ang_r = row[:, None] * inv_freq
    ang_c = col[:, None] * inv_freq
    ang = np.concatenate([ang_r, ang_r, ang_c, ang_c], axis=-1).astype(np.float64)
    cos, sin = np.cos(ang), np.sin(ang)
    lane = np.arange(HEAD_DIM) % AXIS_ROT
    first = lane < AXIS_ROT // 2
    sin_a = np.where(first[None, :], -sin, 0.0)
    sin_b = np.where(first[None, :], 0.0, sin)
    two = lambda a: jnp.asarray(np.concatenate([a, a], axis=1), dtype=F32)
    return two(cos), two(sin_a), two(sin_b)


def _layer(x, ctx, c, c_ctx, lw, moe_w, fng):
    norm1_g, w_ada, b_ada, w_in, w_fourier, q_norm_g, k_norm_g, w_out, norm2_g = lw
    w_router, router_bias, w_expert_gu, w_expert_down, w_shared_gu, w_shared_down = moe_w
    b, l, d = x.shape
    n = b * l

    cond = jnp.concatenate([c, c_ctx[None, :], jnp.zeros((8 - b - 1, d), F32)], axis=0)
    mod = _adaln(cond, w_ada, b_ada)
    sh1, sc1, g1, sh2, sc2, g2 = [m[:b, None, :] for m in jnp.split(mod, 6, axis=-1)]
    csh1, csc1 = [m[b:b + 1, None, :] for m in jnp.split(mod, 6, axis=-1)[:2]]

    w_in_bf = w_in.astype(BF16)
    two = lambda g, s: jnp.tile(g * s, 2).reshape(1, LANES)
    gq, gk = two(q_norm_g, ATTN_SCALE * float(np.log2(np.e))), two(k_norm_g, 1.0)
    ones_bd = jnp.asarray(np.kron(np.eye(2), np.ones((HEAD_DIM, HEAD_DIM))), dtype=F32).astype(BF16)
    n1g = norm1_g.reshape(1, d)

    kc, vct = _inproj(ctx, csc1, csh1, n1g, w_in_bf[:, KV_COL0:], gq, gk, ones_bd, None, False)
    u, qt, k, vt = _inproj(x, sc1, sh1, n1g, w_in_bf, gq, gk, ones_bd, _rope_tables(l), True)

    w_bd = jnp.zeros((FOURIER_WIDTH, FOURIER_WIDTH), F32)
    for h in range(FOURIER_HEADS):
        w_bd = lax.dynamic_update_slice(w_bd, w_fourier[h], (h * HEAD_DIM, h * HEAD_DIM))
    four = _fourier(u, w_bd.astype(BF16))
    attn = _attention(qt, k, vt, kc, vct)

    wr_t = w_router.T
    wr_hi = wr_t.astype(BF16)
    wr_lo = (wr_t - wr_hi.astype(F32)).astype(BF16)
    xn, hpa, hpb, top_e, gate, cnt = _outproj_router(
        x, four, attn, g1, w_out.astype(BF16), norm2_g.reshape(1, d), sc2, sh2,
        wr_hi, wr_lo, router_bias.reshape(N_EXPERTS, 1))

    tm = min(TOK_TILE, l)
    rb = ROW_BLOCK
    counts = cnt[:, 0, :].astype(I32)
    total = jnp.sum(counts, axis=0)
    padded = (total + rb - 1) // rb * rb
    pad_end = jnp.cumsum(padded)
    pad_start = pad_end - padded
    base = pad_start[None, :] + jnp.cumsum(counts, axis=0) - counts
    n_blocks = n * TOP_K // rb + N_EXPERTS
    n_used = pad_end[-1] // rb
    blk_row = jnp.minimum(jnp.arange(n_blocks, dtype=I32), n_used - 1) * rb
    block_e = jnp.sum((pad_end[None, :] <= blk_row[:, None]).astype(I32), axis=1)
    block_e = jnp.minimum(block_e, N_EXPERTS - 1)
    row_end = (pad_start + total)[block_e]
    block_nvalid = jnp.where(jnp.arange(n_blocks) < n_used, jnp.clip(row_end - blk_row, 0, rb), 0).astype(I32)

    dest = _dest_rows(top_e, base.astype(F32)[:, :, None], tm)
    xa = _sc_scatter_rows(hpa, dest, n_blocks * rb)
    xb = _sc_scatter_rows(hpb, dest, n_blocks * rb)
    ya, yb = _experts(block_e, block_nvalid, n_used.astype(I32).reshape(1), xa, xb, w_expert_gu, w_expert_down)
    return _combine(dest, gate.T, xn.reshape(n, d), hpa, hpb, g2, w_shared_gu.astype(BF16),
                    w_shared_down.astype(BF16), fng, ya, yb, l)


def kernel(x, c, ctx, c_ctx, norm1_g, w_ada, b_ada, w_in, w_fourier, q_norm_g, k_norm_g, w_out, norm2_g,
           w_router, router_bias, w_expert_gu, w_expert_down, w_shared_gu, w_shared_down, final_norm_g):
    depth = norm1_g.shape[0]
    assert depth == 1, "context update between layers is not implemented"
    b, l, d = x.shape
    lw = (norm1_g[0], w_ada[0], b_ada[0], w_in[0], w_fourier[0], q_norm_g[0], k_norm_g[0], w_out[0], norm2_g[0])
    moe_w = (w_router[0], router_bias[0], w_expert_gu[0], w_expert_down[0], w_shared_gu[0], w_shared_down[0])
    out = _layer(x, ctx, c, c_ctx, lw, moe_w, final_norm_g.reshape(1, d))
    return out.reshape(b, l, d)
```

```python
import functools

import numpy as np
import jax
import jax.numpy as jnp
from jax import lax
from jax.experimental import pallas as pl
from jax.experimental.pallas import tpu as pltpu
from jax.experimental.pallas import tpu_sc as plsc

F32 = jnp.float32
BF16 = jnp.bfloat16
I32 = jnp.int32

D_MODEL = 1024
GRID_W = 64
HEAD_DIM = 64
FOURIER_HEADS = 4
FOURIER_WIDTH = 256
ATTN_HEADS = 12
KV_HEADS = 4
Q_PER_KV = 3
ATTN_WIDTH = 768
KV_WIDTH = 256
KV_COL0 = 1024
IN_WIDTH = 1536
ATTN_SCALE = HEAD_DIM ** -0.5
ROPE_THETA = 10000.0
AXIS_ROT = HEAD_DIM // 2
N_EXPERTS = 256
TOP_K = 8
N_EXPERT_GROUPS = 8
TOPK_GROUPS = 4
EXPERTS_PER_GROUP = 32
EXPERT_FF = 256
ROUTED_SCALE = 2.5
RMS_EPS = 1e-6

LANES = 128
PACK_W = D_MODEL // 4
DFT_L1 = 64
VMEM_LIMIT = 48 * 1024 * 1024

TOK_TILE = 512
ATTN_BQ = 256
ATTN_BK = 1024
QK_SPLIT = 2
ONES_ROWS = 16
ROW_BLOCK = 512
COMB_TILE = 256
SC_WINDOW = 128
SC_CORES = 2
SC_SUBCORES = 16

_HI = lax.Precision.HIGHEST
_NT = (((1,), (1,)), ((), ()))


def _cp(sem, vmem=VMEM_LIMIT):
    return pltpu.CompilerParams(dimension_semantics=sem, vmem_limit_bytes=vmem)


def _sigmoid(v):
    return 1.0 / (1.0 + jnp.exp(-v))


def _pack_rows(v):
    w = v.shape[1] // 4

    def pack(lo, hi):
        lo = lax.bitcast_convert_type(lo.astype(BF16).astype(F32), I32)
        hi = lax.bitcast_convert_type(hi.astype(BF16).astype(F32), I32)
        return lax.shift_right_logical(lo, 16) | (hi & jnp.int32(-65536))

    return pack(v[:, :w], v[:, 2 * w:3 * w]), pack(v[:, w:2 * w], v[:, 3 * w:])


def _unpack_rows(a, b):
    lo = lambda t: lax.bitcast_convert_type(lax.shift_left(t, 16), F32)
    hi = lambda t: lax.bitcast_convert_type(t & jnp.int32(-65536), F32)
    return [lo(a), lo(b), hi(a), hi(b)]


def _adaln_kernel(c_ref, w_ref, b_ref, o_ref):
    c = c_ref[...]
    s = c * _sigmoid(c)
    o_ref[...] = jnp.dot(s, w_ref[...], precision=_HI, preferred_element_type=F32) + b_ref[...]


def _adaln(cond, w_ada, b_ada):
    rows, d = cond.shape
    n = w_ada.shape[1]
    tn = 1536
    return pl.pallas_call(
        _adaln_kernel,
        grid=(n // tn,),
        in_specs=[pl.BlockSpec((rows, d), lambda j: (0, 0)),
                  pl.BlockSpec((d, tn), lambda j: (0, j)),
                  pl.BlockSpec((1, tn), lambda j: (0, j))],
        out_specs=pl.BlockSpec((rows, tn), lambda j: (0, j)),
        out_shape=jax.ShapeDtypeStruct((rows, n), F32),
        compiler_params=_cp(("arbitrary",)),
        name="adaln",
    )(cond, w_ada, b_ada.reshape(1, n))


def _modulated_norm(x, g, sc, sh):
    ms = jnp.mean(x * x, axis=-1, keepdims=True)
    return x * lax.rsqrt(ms + RMS_EPS) * g * (1.0 + sc) + sh


def _head_norm(chunk, gain, ones_bd):
    sq = chunk * chunk
    hi = sq.astype(BF16)
    lo = (sq - hi.astype(F32)).astype(BF16)
    ss = (jnp.dot(hi, ones_bd, preferred_element_type=F32)
          + jnp.dot(lo, ones_bd, preferred_element_type=F32))
    return chunk * lax.rsqrt(ss * (1.0 / HEAD_DIM) + RMS_EPS) * gain


def _rope(t, cos, sin_a, sin_b):
    return (t * cos + pltpu.roll(t, LANES - AXIS_ROT // 2, 1) * sin_a
            + pltpu.roll(t, AXIS_ROT // 2, 1) * sin_b)


def _inproj_kernel(*refs, rope, with_uq):
    if rope:
        (x_ref, sc_ref, sh_ref, g_ref, w_ref, gq_ref, gk_ref, ones_ref,
         cos_ref, sa_ref, sb_ref) = refs[:11]
        outs = refs[11:]
    else:
        x_ref, sc_ref, sh_ref, g_ref, w_ref, gq_ref, gk_ref, ones_ref = refs[:8]
        outs = refs[8:]
    if with_uq:
        u_ref, qt_ref, k_ref, vt_ref = outs
    else:
        k_ref, vt_ref = outs

    h = _modulated_norm(x_ref[0], g_ref[...], sc_ref[0], sh_ref[0])
    z = jnp.dot(h.astype(BF16), w_ref[...], preferred_element_type=F32)
    ones_bd = ones_ref[...]

    def normed(chunk, gain):
        t = _head_norm(chunk, gain, ones_bd)
        if rope:
            t = _rope(t, cos_ref[...], sa_ref[...], sb_ref[...])
        return t

    col = 0
    if with_uq:
        u_ref[0] = z[:, :FOURIER_WIDTH].astype(BF16)
        col = FOURIER_WIDTH
        qs = [normed(z[:, col + j * LANES: col + (j + 1) * LANES], gq_ref[...])
              for j in range(ATTN_WIDTH // LANES)]
        qt_ref[0] = jnp.concatenate(qs, axis=1).T.astype(BF16)
        col += ATTN_WIDTH
    for j in range(KV_WIDTH // LANES):
        t = normed(z[:, col + j * LANES: col + (j + 1) * LANES], gk_ref[...])
        k_ref[0, 2 * j] = t[:, :HEAD_DIM].astype(BF16)
        k_ref[0, 2 * j + 1] = t[:, HEAD_DIM:].astype(BF16)
    col += KV_WIDTH
    vt_ref[0] = z[:, col:col + KV_WIDTH].T.astype(BF16)


def _inproj(x, sc, sh, g, w, gq, gk, ones_bd, rope_tabs, with_uq):
    b, t, d = x.shape
    tm = min(TOK_TILE, t)
    wcols = w.shape[1]
    bm = sc.shape[0]
    mod_map = (lambda i, j: (i, 0, 0)) if bm == b else (lambda i, j: (0, 0, 0))
    const2 = lambda i, j: (0, 0)
    in_specs = [pl.BlockSpec((1, tm, d), lambda i, j: (i, j, 0)),
                pl.BlockSpec((1, 1, d), mod_map),
                pl.BlockSpec((1, 1, d), mod_map),
                pl.BlockSpec((1, d), const2),
                pl.BlockSpec((d, wcols), const2),
                pl.BlockSpec((1, LANES), const2),
                pl.BlockSpec((1, LANES), const2),
                pl.BlockSpec((LANES, LANES), const2)]
    args = [x, sc, sh, g, w, gq, gk, ones_bd]
    rope = rope_tabs is not None
    if rope:
        in_specs += [pl.BlockSpec((tm, LANES), lambda i, j: (j, 0))] * 3
        args += list(rope_tabs)
    out_specs, out_shape = [], []
    if with_uq:
        out_specs += [pl.BlockSpec((1, tm, FOURIER_WIDTH), lambda i, j: (i, j, 0)),
                      pl.BlockSpec((1, ATTN_WIDTH, tm), lambda i, j: (i, 0, j))]
        out_shape += [jax.ShapeDtypeStruct((b, t, FOURIER_WIDTH), BF16),
                      jax.ShapeDtypeStruct((b, ATTN_WIDTH, t), BF16)]
    out_specs += [pl.BlockSpec((1, KV_HEADS, tm, HEAD_DIM), lambda i, j: (i, 0, j, 0)),
                  pl.BlockSpec((1, KV_WIDTH, tm), lambda i, j: (i, 0, j))]
    out_shape += [jax.ShapeDtypeStruct((b, KV_HEADS, t, HEAD_DIM), BF16),
                  jax.ShapeDtypeStruct((b, KV_WIDTH, t), BF16)]
    return pl.pallas_call(
        functools.partial(_inproj_kernel, rope=rope, with_uq=with_uq),
        grid=(b, t // tm),
        in_specs=in_specs,
        out_specs=out_specs,
        out_shape=out_shape,
        compiler_params=_cp(("arbitrary", "arbitrary")),
        name="inproj_latent" if with_uq else "inproj_ctx",
    )(*args)


def _fourier_a_kernel(u_ref, c_ref, s_ref, yr_ref, yi_ref):
    u = u_ref[0]
    yr_ref[0] = jnp.dot(c_ref[...], u, preferred_element_type=F32).astype(BF16)
    yi_ref[0] = jnp.dot(s_ref[...], u, preferred_element_type=F32).astype(BF16)


def _fourier_b_kernel(yr_ref, yi_ref, m_ref, c_ref, s_ref, w_ref, o_ref, *, kb):
    y = jnp.concatenate([yr_ref[0], yi_ref[0]], axis=1)
    x = jnp.einsum("kab,kbc->kac", m_ref[...], y, preferred_element_type=F32)
    xr = x[:, :DFT_L1].reshape(kb * DFT_L1, FOURIER_WIDTH).astype(BF16)
    xi = x[:, DFT_L1:].reshape(kb * DFT_L1, FOURIER_WIDTH).astype(BF16)
    spec = (jnp.dot(xr, c_ref[...], preferred_element_type=F32)
            + jnp.dot(xi, s_ref[...], preferred_element_type=F32))
    o = jnp.dot(spec.astype(BF16), w_ref[...], preferred_element_type=F32)
    for j in range(kb):
        o_ref[0, :, j, :] = o[j * DFT_L1:(j + 1) * DFT_L1]


def _dft_tables(l):
    l2 = l // DFT_L1
    k2 = np.arange(l2)
    ang2 = 2.0 * np.pi * ((k2[:, None] * k2[None, :]) % l2) / l2
    c2 = np.cos(ang2)
    s2n = -np.sin(ang2)
    n1 = np.arange(DFT_L1)
    k = l2 * n1[None, :, None] + k2[:, None, None]
    ang = 2.0 * np.pi * ((k * n1[None, None, :]) % l) / l
    mr, mi = np.cos(ang), -np.sin(ang)
    m = np.concatenate([np.concatenate([mr, -mi], axis=2),
                        np.concatenate([mi, mr], axis=2)], axis=1)
    d = np.arange(HEAD_DIM)
    angc = 2.0 * np.pi * ((d[:, None] * d[None, :]) % HEAD_DIM) / HEAD_DIM
    scale = 1.0 / np.sqrt(float(l) * HEAD_DIM)
    eye = np.eye(FOURIER_HEADS)
    cbd = np.kron(eye, np.cos(angc) * scale)
    sbd = np.kron(eye, np.sin(angc) * scale)
    as_bf = lambda a: jnp.asarray(a, dtype=F32).astype(BF16)
    return as_bf(c2), as_bf(s2n), as_bf(m), as_bf(cbd), as_bf(sbd)


def _fourier(u, w_bd):
    b, l, fw = u.shape
    l2 = l // DFT_L1
    ncol = DFT_L1 * fw
    tn = min(4096, ncol)
    kb = min(8, l2)
    c2, s2n, m, cbd, sbd = _dft_tables(l)
    yr, yi = pl.pallas_call(
        _fourier_a_kernel,
        grid=(b, ncol // tn),
        in_specs=[pl.BlockSpec((1, l2, tn), lambda i, j: (i, 0, j)),
                  pl.BlockSpec((l2, l2), lambda i, j: (0, 0)),
                  pl.BlockSpec((l2, l2), lambda i, j: (0, 0))],
        out_specs=[pl.BlockSpec((1, l2, tn), lambda i, j: (i, 0, j))] * 2,
        out_shape=[jax.ShapeDtypeStruct((b, l2, ncol), BF16)] * 2,
        compiler_params=_cp(("arbitrary", "arbitrary")),
        name="fourier_a",
    )(u.reshape(b, l2, ncol), c2, s2n)
    yr = yr.reshape(b, l2, DFT_L1, fw)
    yi = yi.reshape(b, l2, DFT_L1, fw)
    out = pl.pallas_call(
        functools.partial(_fourier_b_kernel, kb=kb),
        grid=(b, l2 // kb),
        in_specs=[pl.BlockSpec((1, kb, DFT_L1, fw), lambda i, j: (i, j, 0, 0)),
                  pl.BlockSpec((1, kb, DFT_L1, fw), lambda i, j: (i, j, 0, 0)),
                  pl.BlockSpec((kb, 2 * DFT_L1, 2 * DFT_L1), lambda i, j: (j, 0, 0)),
                  pl.BlockSpec((fw, fw), lambda i, j: (0, 0)),
                  pl.BlockSpec((fw, fw), lambda i, j: (0, 0)),
                  pl.BlockSpec((fw, fw), lambda i, j: (0, 0))],
        out_specs=pl.BlockSpec((1, DFT_L1, kb, fw), lambda i, j: (i, 0, j, 0)),
        out_shape=jax.ShapeDtypeStruct((b, DFT_L1, l2, fw), F32),
        compiler_params=_cp(("arbitrary", "arbitrary")),
        name="fourier_b",
    )(yr, yi, m, cbd, sbd, w_bd)
    return out.reshape(b, l, fw)


def _attn_kernel(qt_ref, k_ref, vt_ref, kc_ref, vct_ref, o_ref, m_scr, acc_scr, *s_scrs, bk, nchunks):
    bq = qt_ref.shape[2]
    m_scr[...] = jnp.full(m_scr.shape, -jnp.inf, F32)
    acc_scr[...] = jnp.zeros(acc_scr.shape, F32)

    def scores(k, buf, h):
        nk = k.shape[0]
        qt = qt_ref[0, h * HEAD_DIM:(h + 1) * HEAD_DIM, :]
        part = nk // QK_SPLIT
        for j in range(QK_SPLIT):
            s_scrs[buf][j * part:(j + 1) * part, :] = jnp.dot(k[j * part:(j + 1) * part], qt,
                                                              preferred_element_type=F32)

    def softmax_pv(buf, h, vt):
        nk = vt.shape[1]
        s = s_scrs[buf][:nk, :]
        m_old = m_scr[h]
        m_new = jnp.maximum(m_old, jnp.max(s, axis=0, keepdims=True))
        alpha = jnp.exp2(m_old - m_new)
        p = jnp.exp2((s - m_new).astype(BF16))
        vt_ones = jnp.concatenate([vt, jnp.ones((ONES_ROWS, nk), BF16)], axis=0)
        acc_scr[h] = alpha * acc_scr[h] + jnp.dot(vt_ones, p, preferred_element_type=F32)
        m_scr[h] = m_new

    def keys(c):
        if isinstance(c, int) and c == nchunks:
            return kc_ref[0, 0]
        return k_ref[0, 0, pl.ds(pl.multiple_of(c * bk, bk), bk), :]

    def values(c):
        if isinstance(c, int) and c == nchunks:
            return vct_ref[0]
        return vt_ref[0, :, pl.ds(pl.multiple_of(c * bk, bk), bk)]

    nbuf = len(s_scrs)
    pair = 1
    ahead = nbuf - pair
    n_units = Q_PER_KV * (nchunks + 1)

    def group(us, chunk0=0):
        for u in us:
            ua = u + ahead
            if not (isinstance(chunk0, int) and ua >= n_units):
                scores(keys(chunk0 + ua // Q_PER_KV), ua % nbuf, ua % Q_PER_KV)
        for u in us:
            softmax_pv(u % nbuf, u % Q_PER_KV, values(chunk0 + u // Q_PER_KV))

    for u in range(min(ahead, n_units)):
        scores(keys(u // Q_PER_KV), u % nbuf, u % Q_PER_KV)

    chunks_per_iter = nbuf // Q_PER_KV
    n_iter = max(0, (Q_PER_KV * nchunks - ahead) // nbuf)

    def body(j, carry):
        for t in range(0, nbuf, pair):
            group(range(t, t + pair), j * chunks_per_iter)
        return carry

    lax.fori_loop(0, n_iter, body, 0)
    for t in range(n_iter * nbuf, n_units, pair):
        group(range(t, min(t + pair, n_units)))
    outs = [acc_scr[h, :HEAD_DIM, :] / acc_scr[h, HEAD_DIM:HEAD_DIM + 1, :] for h in range(Q_PER_KV)]
    o_t = jnp.concatenate(outs + [jnp.zeros((HEAD_DIM, bq), F32)], axis=0)
    o_ref[0, 0] = o_t.T[:, :Q_PER_KV * HEAD_DIM].astype(BF16)


def _attention(qt, k, vt, kc, vct):
    b, _, l = qt.shape
    c = kc.shape[2]
    bq = min(ATTN_BQ, l)
    bk = min(ATTN_BK, l)
    gw = Q_PER_KV * HEAD_DIM
    return pl.pallas_call(
        functools.partial(_attn_kernel, bk=bk, nchunks=l // bk),
        grid=(b, KV_HEADS, l // bq),
        in_specs=[pl.BlockSpec((1, gw, bq), lambda i, g, j: (i, g, j)),
                  pl.BlockSpec((1, 1, l, HEAD_DIM), lambda i, g, j: (i, g, 0, 0)),
                  pl.BlockSpec((1, HEAD_DIM, l), lambda i, g, j: (i, g, 0)),
                  pl.BlockSpec((1, 1, c, HEAD_DIM), lambda i, g, j: (i, g, 0, 0)),
                  pl.BlockSpec((1, HEAD_DIM, c), lambda i, g, j: (i, g, 0))],
        out_specs=pl.BlockSpec((1, 1, bq, gw), lambda i, g, j: (i, g, j, 0)),
        out_shape=jax.ShapeDtypeStruct((b, KV_HEADS, l, gw), BF16),
        scratch_shapes=[pltpu.VMEM((Q_PER_KV, 1, bq), F32),
                        pltpu.VMEM((Q_PER_KV, HEAD_DIM + ONES_ROWS, bq), F32),
                        *[pltpu.VMEM((max(bk, c), bq), F32)] * (2 * Q_PER_KV)],
        compiler_params=_cp(("arbitrary", "arbitrary", "arbitrary")),
        name="attention",
    )(qt, k, vt, kc, vct)


def _outproj_router_kernel(x_ref, f_ref, a_ref, g1_ref, wo_ref, n2g_ref, sc_ref, sh_ref,
                           wrh_ref, wrl_ref, bias_ref, xn_ref, hpa_ref, hpb_ref, te_ref, gt_ref, cnt_ref):
    tm = x_ref.shape[1]
    mix = jnp.dot(f_ref[0].astype(BF16), wo_ref[:FOURIER_WIDTH, :], preferred_element_type=F32)
    gw = Q_PER_KV * HEAD_DIM
    for g in range(KV_HEADS):
        r0 = FOURIER_WIDTH + g * gw
        mix += jnp.dot(a_ref[0, g], wo_ref[r0:r0 + gw, :], preferred_element_type=F32)
    xn = x_ref[0] + g1_ref[0] * mix
    xn_ref[0] = xn
    h2 = _modulated_norm(xn, n2g_ref[...], sc_ref[0], sh_ref[0])
    h2_hi = h2.astype(BF16)
    h2_lo = (h2 - h2_hi.astype(F32)).astype(BF16)
    hpa_ref[...], hpb_ref[...] = _pack_rows(h2)

    wrh = wrh_ref[...]
    logits = (lax.dot_general(wrh, h2_hi, _NT, preferred_element_type=F32)
              + lax.dot_general(wrh, h2_lo, _NT, preferred_element_type=F32)
              + lax.dot_general(wrl_ref[...], h2_hi, _NT, preferred_element_type=F32))
    scores = _sigmoid(logits)
    sel = scores + bias_ref[...]

    neg = jnp.float32(-jnp.inf)
    s3 = sel.reshape(N_EXPERT_GROUPS, EXPERTS_PER_GROUP, tm)
    i3 = lax.broadcasted_iota(I32, s3.shape, 1)
    m1 = jnp.max(s3, axis=1, keepdims=True)
    i1 = jnp.min(jnp.where(s3 == m1, i3, EXPERTS_PER_GROUP), axis=1, keepdims=True)
    m2 = jnp.max(jnp.where(i3 == i1, neg, s3), axis=1)
    gs = m1[:, 0, :] + m2
    gi = lax.broadcasted_iota(I32, gs.shape, 0)
    keep = jnp.zeros(gs.shape, jnp.bool_)
    for _ in range(TOPK_GROUPS):
        m = jnp.max(gs, axis=0, keepdims=True)
        idx = jnp.min(jnp.where(gs == m, gi, N_EXPERT_GROUPS), axis=0, keepdims=True)
        hit = gi == idx
        keep = keep | hit
        gs = jnp.where(hit, neg, gs)
    keep3 = jnp.broadcast_to(keep[:, None, :], s3.shape)
    selm = jnp.where(keep3, s3, neg).reshape(N_EXPERTS, tm)

    ei = lax.broadcasted_iota(I32, selm.shape, 0)
    multi = jnp.zeros(selm.shape, F32)
    idxs, gates = [], []
    for _ in range(TOP_K):
        m = jnp.max(selm, axis=0, keepdims=True)
        idx = jnp.min(jnp.where(selm == m, ei, N_EXPERTS), axis=0, keepdims=True)
        hit = ei == idx
        gates.append(jnp.sum(jnp.where(hit, scores, 0.0), axis=0, keepdims=True))
        idxs.append(idx)
        selm = jnp.where(hit, neg, selm)
        multi = multi + hit.astype(F32)
    gate = jnp.concatenate(gates, axis=0)
    gate = gate / jnp.sum(gate, axis=0, keepdims=True) * ROUTED_SCALE
    te_ref[...] = jnp.concatenate(idxs, axis=0)
    gt_ref[...] = gate
    ones = jnp.ones((8, tm), BF16)
    cnt_ref[0] = lax.dot_general(ones, multi.astype(BF16), _NT, preferred_element_type=F32)


def _outproj_router(x, four, attn, g1, w_out, n2g, sc2, sh2, wr_hi, wr_lo, bias):
    b, l, d = x.shape
    tm = min(TOK_TILE, l)
    tpb = l // tm
    n = b * l
    const2 = lambda i, j: (0, 0)
    mod_map = lambda i, j: (i, 0, 0)
    tok_map = lambda i, j: (0, i * tpb + j)
    return pl.pallas_call(
        _outproj_router_kernel,
        grid=(b, tpb),
        in_specs=[pl.BlockSpec((1, tm, d), lambda i, j: (i, j, 0)),
                  pl.BlockSpec((1, tm, FOURIER_WIDTH), lambda i, j: (i, j, 0)),
                  pl.BlockSpec((1, KV_HEADS, tm, Q_PER_KV * HEAD_DIM), lambda i, j: (i, 0, j, 0)),
                  pl.BlockSpec((1, 1, d), mod_map),
                  pl.BlockSpec((d, d), const2),
                  pl.BlockSpec((1, d), const2),
                  pl.BlockSpec((1, 1, d), mod_map),
                  pl.BlockSpec((1, 1, d), mod_map),
                  pl.BlockSpec((N_EXPERTS, d), const2),
                  pl.BlockSpec((N_EXPERTS, d), const2),
                  pl.BlockSpec((N_EXPERTS, 1), const2)],
        out_specs=[pl.BlockSpec((1, tm, d), lambda i, j: (i, j, 0)),
                   pl.BlockSpec((tm, PACK_W), lambda i, j: (i * tpb + j, 0)),
                   pl.BlockSpec((tm, PACK_W), lambda i, j: (i * tpb + j, 0)),
                   pl.BlockSpec((TOP_K, tm), tok_map),
                   pl.BlockSpec((TOP_K, tm), tok_map),
                   pl.BlockSpec((1, 8, N_EXPERTS), lambda i, j: (i * tpb + j, 0, 0))],
        out_shape=[jax.ShapeDtypeStruct((b, l, d), F32),
                   jax.ShapeDtypeStruct((n, PACK_W), I32),
                   jax.ShapeDtypeStruct((n, PACK_W), I32),
                   jax.ShapeDtypeStruct((TOP_K, n), I32),
                   jax.ShapeDtypeStruct((TOP_K, n), F32),
                   jax.ShapeDtypeStruct((n // tm, 8, N_EXPERTS), F32)],
        compiler_params=_cp(("arbitrary", "arbitrary")),
        name="outproj_router",
    )(x, four, attn, g1, w_out, n2g, sc2, sh2, wr_hi, wr_lo, bias)


def _dest_kernel(te_ref, base_ref, tri_ref, d_ref):
    te = te_ref[...]
    tm = te.shape[1]
    ei = lax.broadcasted_iota(I32, (N_EXPERTS, tm), 0)
    hits = [ei == te[k:k + 1, :] for k in range(TOP_K)]
    multi = hits[0].astype(F32)
    for k in range(1, TOP_K):
        multi = multi + hits[k].astype(F32)
    rank = jnp.dot(multi.astype(BF16), tri_ref[...], preferred_element_type=F32)
    pos = rank + base_ref[0]
    rows = [jnp.sum(jnp.where(hits[k], pos, 0.0), axis=0, keepdims=True) for k in range(TOP_K)]
    d_ref[...] = jnp.concatenate(rows, axis=0).astype(I32)


def _dest_rows(top_e, base, tm):
    n = top_e.shape[1]
    tri = jnp.asarray(np.triu(np.ones((tm, tm), np.float32), 1)).astype(BF16)
    return pl.pallas_call(
        _dest_kernel,
        grid=(n // tm,),
        in_specs=[pl.BlockSpec((TOP_K, tm), lambda i: (0, i)),
                  pl.BlockSpec((1, N_EXPERTS, 1), lambda i: (i, 0, 0)),
                  pl.BlockSpec((tm, tm), lambda i: (0, 0))],
        out_specs=pl.BlockSpec((TOP_K, tm), lambda i: (0, i)),
        out_shape=jax.ShapeDtypeStruct((TOP_K, n), I32),
        compiler_params=_cp(("arbitrary",)),
        name="dest_rows",
    )(top_e, base, tri)


def _sc_mesh():
    return plsc.VectorSubcoreMesh(core_axis_name="c", subcore_axis_name="s",
                                  num_cores=SC_CORES, num_subcores=SC_SUBCORES)


def _sc_scatter_rows(src, dest, n_rows):
    n, w = src.shape
    kk = dest.shape[0]

    @pl.kernel(out_type=jax.ShapeDtypeStruct((n_rows, w), src.dtype), mesh=_sc_mesh(), name="sc_scatter_rows")
    def scatter(x_hbm, d_hbm, o_hbm):
        def body(x_vmem, d_vmem):
            for k in range(kk):
                pltpu.sync_copy(x_vmem, o_hbm.at[d_vmem.at[k]])

        pltpu.emit_pipeline(
            body,
            grid=(n // SC_WINDOW,),
            in_specs=[pl.BlockSpec((SC_WINDOW, w), lambda i: (i, 0)),
                      pl.BlockSpec((kk, SC_WINDOW), lambda i: (0, i))],
            out_specs=[],
            core_axis_name=("c", "s"),
            dimension_semantics=(pltpu.PARALLEL,),
        )(x_hbm, d_hbm)

    return scatter(src, dest)


def _swiglu(quarters, wgu, wdn):
    ag = None
    for j, q in enumerate(quarters):
        part = jnp.dot(q, wgu[j * PACK_W:(j + 1) * PACK_W, :], preferred_element_type=F32)
        ag = part if ag is None else ag + part
    ff = ag.shape[1] // 2
    a, g = ag[:, :ff], ag[:, ff:]
    mid = (a * _sigmoid(a) * g).astype(BF16)
    return jnp.dot(mid, wdn[...], preferred_element_type=F32)


def _expert_kernel(blk0_ref, nblk_ref, cnt_ref, xa_hbm, xb_hbm, wgu_ref, wdn_ref, ya_hbm, yb_hbm,
                   xbuf, ybuf, wgu_s, wdn_s, in_sem, out_sem, pending):
    e = pl.program_id(0)
    rb = xbuf.shape[2]
    nb = nblk_ref[e]
    blk0 = blk0_ref[e]
    x_hbm = (xa_hbm, xb_hbm)
    y_hbm = (ya_hbm, yb_hbm)

    def x_copy(j, slot, t):
        return pltpu.make_async_copy(x_hbm[t].at[pl.ds((blk0 + j) * rb, rb)], xbuf.at[slot, t],
                                     in_sem.at[slot, t])

    def y_copy(j, slot, t):
        return pltpu.make_async_copy(ybuf.at[slot, t], y_hbm[t].at[pl.ds((blk0 + j) * rb, rb)],
                                     out_sem.at[slot, t])

    def drain(slot):
        @pl.when(pending[slot] == 1)
        def _():
            for t in range(2):
                y_copy(0, slot, t).wait()
            pending[slot] = 0

    @pl.when(e == 0)
    def _():
        pending[0] = 0
        pending[1] = 0

    @pl.when(nb > 0)
    def _():
        for t in range(2):
            x_copy(0, 0, t).start()
        wgu_s[...] = wgu_ref[0].astype(BF16)
        wdn_s[...] = wdn_ref[0].astype(BF16)

        def body(j, carry):
            slot = j % 2
            for t in range(2):
                x_copy(j, slot, t).wait()

            @pl.when(j + 1 < nb)
            def _():
                for t in range(2):
                    x_copy(j + 1, 1 - slot, t).start()

            drain(slot)
            valid = lax.broadcasted_iota(I32, (rb, PACK_W), 0) < cnt_ref[e] - j * rb
            xa = jnp.where(valid, xbuf[slot, 0], 0)
            xb = jnp.where(valid, xbuf[slot, 1], 0)
            quarters = [q.astype(BF16) for q in _unpack_rows(xa, xb)]
            ya, yb = _pack_rows(_swiglu(quarters, wgu_s, wdn_s))
            ybuf[slot, 0] = ya
            ybuf[slot, 1] = yb
            for t in range(2):
                y_copy(j, slot, t).start()
            pending[slot] = 1
            return carry

        lax.fori_loop(0, nb, body, 0)

    @pl.when(e == pl.num_programs(0) - 1)
    def _():
        drain(0)
        drain(1)
        n_total = ya_hbm.shape[0] // rb
        n_used = blk0 + nb
        ybuf[0, 0] = jnp.zeros((rb, PACK_W), I32)

        def zero_copy(b, t):
            return pltpu.make_async_copy(ybuf.at[0, 0], y_hbm[t].at[pl.ds(b * rb, rb)], out_sem.at[0, t])

        def start_zero(b, carry):
            for t in range(2):
                zero_copy(b, t).start()
            return carry

        def wait_zero(b, carry):
            for t in range(2):
                zero_copy(b, t).wait()
            return carry

        lax.fori_loop(n_used, n_total, start_zero, 0)
        lax.fori_loop(n_used, n_total, wait_zero, 0)


def _experts(blk0, nblk, cnt, xa, xb, w_gu, w_down):
    n_rows = xa.shape[0]
    rb = ROW_BLOCK
    n_experts, d, ff2 = w_gu.shape
    w_map = lambda e, b0, nb, ct: (e, 0, 0)
    grid_spec = pltpu.PrefetchScalarGridSpec(
        num_scalar_prefetch=3,
        grid=(n_experts,),
        in_specs=[pl.BlockSpec(memory_space=pl.ANY),
                  pl.BlockSpec(memory_space=pl.ANY),
                  pl.BlockSpec((1, d, ff2), w_map),
                  pl.BlockSpec((1, ff2 // 2, d), w_map)],
        out_specs=[pl.BlockSpec(memory_space=pl.ANY)] * 2,
        scratch_shapes=[pltpu.VMEM((2, 2, rb, PACK_W), I32), pltpu.VMEM((2, 2, rb, PACK_W), I32),
                        pltpu.VMEM((d, ff2), BF16), pltpu.VMEM((ff2 // 2, d), BF16),
                        pltpu.SemaphoreType.DMA((2, 2)), pltpu.SemaphoreType.DMA((2, 2)),
                        pltpu.SMEM((2,), I32)],
    )
    return pl.pallas_call(
        _expert_kernel,
        grid_spec=grid_spec,
        out_shape=[jax.ShapeDtypeStruct((n_rows, PACK_W), I32)] * 2,
        compiler_params=_cp(("arbitrary",)),
        name="experts",
    )(blk0, nblk, cnt, xa, xb, w_gu, w_down)


def _sc_gather_rows(table, idx):
    m = idx.shape[1]
    w = table.shape[1]

    @pl.kernel(out_type=jax.ShapeDtypeStruct((m, w), table.dtype), mesh=_sc_mesh(), name="sc_gather_rows")
    def gather(t_hbm, i_hbm, o_hbm):
        def body(i_vmem, o_vmem):
            pltpu.sync_copy(t_hbm.at[i_vmem.at[0]], o_vmem)

        pltpu.emit_pipeline(
            body,
            grid=(m // SC_WINDOW,),
            in_specs=[pl.BlockSpec((1, SC_WINDOW), lambda i: (0, i))],
            out_specs=[pl.BlockSpec((SC_WINDOW, w), lambda i: (i, 0))],
            core_axis_name=("c", "s"),
            dimension_semantics=(pltpu.PARALLEL,),
        )(i_hbm, o_hbm)

    return gather(table, idx)


def _combine_kernel(gate_ref, xn_ref, hpa_ref, hpb_ref, g2_ref, wsgu_ref, wsdn_ref, fng_ref,
                    yga_ref, ygb_ref, o_ref):
    quarters = [q.astype(BF16) for q in _unpack_rows(hpa_ref[...], hpb_ref[...])]
    shared = _swiglu(quarters, wsgu_ref, wsdn_ref)
    gate = gate_ref[...]
    acc = None
    for k in range(TOP_K):
        gk = gate[:, k:k + 1]
        rows = [gk * q for q in _unpack_rows(yga_ref[k], ygb_ref[k])]
        acc = rows if acc is None else [a + r for a, r in zip(acc, rows)]
    y = jnp.concatenate(acc, axis=1) + shared
    xo = xn_ref[...] + g2_ref[0] * y
    ms = jnp.mean(xo * xo, axis=-1, keepdims=True)
    o_ref[...] = xo * lax.rsqrt(ms + RMS_EPS) * fng_ref[...]


def _combine(dest, gate_t, xn, hpa, hpb, g2, ws_gu, ws_dn, fng, ya, yb, tokens_per_batch):
    n, d = xn.shape
    tm = min(COMB_TILE, tokens_per_batch)
    nt = n // tm
    tpb = tokens_per_batch // tm
    const2 = lambda i: (0, 0)
    row_map = lambda i: (i, 0)
    idx = dest.reshape(1, TOP_K * n)
    yga = _sc_gather_rows(ya, idx).reshape(TOP_K, n, PACK_W)
    ygb = _sc_gather_rows(yb, idx).reshape(TOP_K, n, PACK_W)
    return pl.pallas_call(
        _combine_kernel,
        grid=(nt,),
        in_specs=[pl.BlockSpec((tm, TOP_K), row_map),
                  pl.BlockSpec((tm, d), row_map),
                  pl.BlockSpec((tm, PACK_W), row_map),
                  pl.BlockSpec((tm, PACK_W), row_map),
                  pl.BlockSpec((1, 1, d), lambda i: (i // tpb, 0, 0)),
                  pl.BlockSpec(ws_gu.shape, const2),
                  pl.BlockSpec(ws_dn.shape, const2),
                  pl.BlockSpec((1, d), const2),
                  pl.BlockSpec((TOP_K, tm, PACK_W), lambda i: (0, i, 0)),
                  pl.BlockSpec((TOP_K, tm, PACK_W), lambda i: (0, i, 0))],
        out_specs=pl.BlockSpec((tm, d), row_map),
        out_shape=jax.ShapeDtypeStruct((n, d), F32),
        compiler_params=_cp(("arbitrary",)),
        name="combine",
    )(gate_t, xn, hpa, hpb, g2, ws_gu, ws_dn, fng, yga, ygb)


def _rope_tables(l):
    rows = l // GRID_W
    row = np.repeat(np.arange(rows, dtype=np.float32), GRID_W)
    col = np.tile(np.arange(GRID_W, dtype=np.float32), rows)
    n_freq = AXIS_ROT // 2
    inv_freq = (np.float32(ROPE_THETA) ** (-np.arange(n_freq, dtype=np.float32) / n_freq)).astype(np.float32)
    ang_r = row[:, None] * inv_freq
    ang_c = col[:, None] * inv_freq
    ang = np.concatenate([ang_r, ang_r, ang_c, ang_c], axis=-1).astype(np.float64)
    cos, sin = np.cos(ang), np.sin(ang)
    lane = np.arange(HEAD_DIM) % AXIS_ROT
    first = lane < AXIS_ROT // 2
    sin_a = np.where(first[None, :], -sin, 0.0)
    sin_b = np.where(first[None, :], 0.0, sin)
    two = lambda a: jnp.asarray(np.concatenate([a, a], axis=1), dtype=F32)
    return two(cos), two(sin_a), two(sin_b)


def _layer(x, ctx, c, c_ctx, lw, moe_w, fng):
    norm1_g, w_ada, b_ada, w_in, w_fourier, q_norm_g, k_norm_g, w_out, norm2_g = lw
    w_router, router_bias, w_expert_gu, w_expert_down, w_shared_gu, w_shared_down = moe_w
    b, l, d = x.shape
    n = b * l

    cond = jnp.concatenate([c, c_ctx[None, :], jnp.zeros((8 - b - 1, d), F32)], axis=0)
    mod = _adaln(cond, w_ada, b_ada)
    sh1, sc1, g1, sh2, sc2, g2 = [m[:b, None, :] for m in jnp.split(mod, 6, axis=-1)]
    csh1, csc1 = [m[b:b + 1, None, :] for m in jnp.split(mod, 6, axis=-1)[:2]]

    w_in_bf = w_in.astype(BF16)
    two = lambda g, s: jnp.tile(g * s, 2).reshape(1, LANES)
    gq, gk = two(q_norm_g, ATTN_SCALE * float(np.log2(np.e))), two(k_norm_g, 1.0)
    ones_bd = jnp.asarray(np.kron(np.eye(2), np.ones((HEAD_DIM, HEAD_DIM))), dtype=F32).astype(BF16)
    n1g = norm1_g.reshape(1, d)

    kc, vct = _inproj(ctx, csc1, csh1, n1g, w_in_bf[:, KV_COL0:], gq, gk, ones_bd, None, False)
    u, qt, k, vt = _inproj(x, sc1, sh1, n1g, w_in_bf, gq, gk, ones_bd, _rope_tables(l), True)

    w_bd = jnp.zeros((FOURIER_WIDTH, FOURIER_WIDTH), F32)
    for h in range(FOURIER_HEADS):
        w_bd = lax.dynamic_update_slice(w_bd, w_fourier[h], (h * HEAD_DIM, h * HEAD_DIM))
    four = _fourier(u, w_bd.astype(BF16))
    attn = _attention(qt, k, vt, kc, vct)

    wr_t = w_router.T
    wr_hi = wr_t.astype(BF16)
    wr_lo = (wr_t - wr_hi.astype(F32)).astype(BF16)
    xn, hpa, hpb, top_e, gate, cnt = _outproj_router(
        x, four, attn, g1, w_out.astype(BF16), norm2_g.reshape(1, d), sc2, sh2,
        wr_hi, wr_lo, router_bias.reshape(N_EXPERTS, 1))

    tm = min(TOK_TILE, l)
    rb = ROW_BLOCK
    counts = cnt[:, 0, :].astype(I32)
    total = jnp.sum(counts, axis=0)
    padded = (total + rb - 1) // rb * rb
    pad_end = jnp.cumsum(padded)
    pad_start = pad_end - padded
    base = pad_start[None, :] + jnp.cumsum(counts, axis=0) - counts
    n_blocks = n * TOP_K // rb + N_EXPERTS

    dest = _dest_rows(top_e, base.astype(F32)[:, :, None], tm)
    xa = _sc_scatter_rows(hpa, dest, n_blocks * rb)
    xb = _sc_scatter_rows(hpb, dest, n_blocks * rb)
    ya, yb = _experts(pad_start // rb, padded // rb, total, xa, xb, w_expert_gu, w_expert_down)
    return _combine(dest, gate.T, xn.reshape(n, d), hpa, hpb, g2, w_shared_gu.astype(BF16),
                    w_shared_down.astype(BF16), fng, ya, yb, l)


def kernel(x, c, ctx, c_ctx, norm1_g, w_ada, b_ada, w_in, w_fourier, q_norm_g, k_norm_g, w_out, norm2_g,
           w_router, router_bias, w_expert_gu, w_expert_down, w_shared_gu, w_shared_down, final_norm_g):
    depth = norm1_g.shape[0]
    assert depth == 1, "context update between layers is not implemented"
    b, l, d = x.shape
    lw = (norm1_g[0], w_ada[0], b_ada[0], w_in[0], w_fourier[0], q_norm_g[0], k_norm_g[0], w_out[0], norm2_g[0])
    moe_w = (w_router[0], router_bias[0], w_expert_gu[0], w_expert_down[0], w_shared_gu[0], w_shared_down[0])
    out = _layer(x, ctx, c, c_ctx, lw, moe_w, final_norm_g.reshape(1, d))
    return out.reshape(b, l, d)
```

```python
import functools

import numpy as np
import jax
import jax.numpy as jnp
from jax import lax
from jax.experimental import pallas as pl
from jax.experimental.pallas import tpu as pltpu
from jax.experimental.pallas import tpu_sc as plsc

F32 = jnp.float32
BF16 = jnp.bfloat16
I32 = jnp.int32

D_MODEL = 1024
GRID_W = 64
HEAD_DIM = 64
FOURIER_HEADS = 4
FOURIER_WIDTH = 256
ATTN_HEADS = 12
KV_HEADS = 4
Q_PER_KV = 3
ATTN_WIDTH = 768
KV_WIDTH = 256
KV_COL0 = 1024
IN_WIDTH = 1536
ATTN_SCALE = HEAD_DIM ** -0.5
ROPE_THETA = 10000.0
AXIS_ROT = HEAD_DIM // 2
N_EXPERTS = 256
TOP_K = 8
N_EXPERT_GROUPS = 8
TOPK_GROUPS = 4
EXPERTS_PER_GROUP = 32
EXPERT_FF = 256
ROUTED_SCALE = 2.5
RMS_EPS = 1e-6

LANES = 128
PACK_W = D_MODEL // 4
DFT_L1 = 64
VMEM_LIMIT = 48 * 1024 * 1024

TOK_TILE = 512
ATTN_BQ = 256
ATTN_BK = 1024
QK_SPLIT = 2
ONES_ROWS = 16
ROW_BLOCK = 512
X_SLOTS = 3
COMB_TILE = 256
SC_WINDOW = 128
SC_CORES = 2
SC_SUBCORES = 16

_HI = lax.Precision.HIGHEST
_NT = (((1,), (1,)), ((), ()))


def _cp(sem, vmem=VMEM_LIMIT):
    return pltpu.CompilerParams(dimension_semantics=sem, vmem_limit_bytes=vmem)


def _sigmoid(v):
    return 1.0 / (1.0 + jnp.exp(-v))


def _pack_rows(v):
    w = v.shape[1] // 4

    def pack(lo, hi):
        lo = lax.bitcast_convert_type(lo.astype(BF16).astype(F32), I32)
        hi = lax.bitcast_convert_type(hi.astype(BF16).astype(F32), I32)
        return lax.shift_right_logical(lo, 16) | (hi & jnp.int32(-65536))

    return pack(v[:, :w], v[:, 2 * w:3 * w]), pack(v[:, w:2 * w], v[:, 3 * w:])


def _unpack_rows(a, b):
    lo = lambda t: lax.bitcast_convert_type(lax.shift_left(t, 16), F32)
    hi = lambda t: lax.bitcast_convert_type(t & jnp.int32(-65536), F32)
    return [lo(a), lo(b), hi(a), hi(b)]


def _adaln_kernel(c_ref, w_ref, b_ref, o_ref):
    c = c_ref[...]
    s = c * _sigmoid(c)
    o_ref[...] = jnp.dot(s, w_ref[...], precision=_HI, preferred_element_type=F32) + b_ref[...]


def _adaln(cond, w_ada, b_ada):
    rows, d = cond.shape
    n = w_ada.shape[1]
    tn = 1536
    return pl.pallas_call(
        _adaln_kernel,
        grid=(n // tn,),
        in_specs=[pl.BlockSpec((rows, d), lambda j: (0, 0)),
                  pl.BlockSpec((d, tn), lambda j: (0, j)),
                  pl.BlockSpec((1, tn), lambda j: (0, j))],
        out_specs=pl.BlockSpec((rows, tn), lambda j: (0, j)),
        out_shape=jax.ShapeDtypeStruct((rows, n), F32),
        compiler_params=_cp(("arbitrary",)),
        name="adaln",
    )(cond, w_ada, b_ada.reshape(1, n))


def _modulated_norm(x, g, sc, sh):
    ms = jnp.mean(x * x, axis=-1, keepdims=True)
    return x * lax.rsqrt(ms + RMS_EPS) * g * (1.0 + sc) + sh


def _head_norm(chunk, gain, ones_bd):
    sq = chunk * chunk
    hi = sq.astype(BF16)
    lo = (sq - hi.astype(F32)).astype(BF16)
    ss = (jnp.dot(hi, ones_bd, preferred_element_type=F32)
          + jnp.dot(lo, ones_bd, preferred_element_type=F32))
    return chunk * lax.rsqrt(ss * (1.0 / HEAD_DIM) + RMS_EPS) * gain


def _rope(t, cos, sin_a, sin_b):
    return (t * cos + pltpu.roll(t, LANES - AXIS_ROT // 2, 1) * sin_a
            + pltpu.roll(t, AXIS_ROT // 2, 1) * sin_b)


def _inproj_kernel(*refs, rope, with_uq):
    if rope:
        (x_ref, sc_ref, sh_ref, g_ref, w_ref, gq_ref, gk_ref, ones_ref,
         cos_ref, sa_ref, sb_ref) = refs[:11]
        outs = refs[11:]
    else:
        x_ref, sc_ref, sh_ref, g_ref, w_ref, gq_ref, gk_ref, ones_ref = refs[:8]
        outs = refs[8:]
    if with_uq:
        u_ref, qt_ref, k_ref, vt_ref = outs
    else:
        k_ref, vt_ref = outs

    h = _modulated_norm(x_ref[0], g_ref[...], sc_ref[0], sh_ref[0])
    z = jnp.dot(h.astype(BF16), w_ref[...], preferred_element_type=F32)
    ones_bd = ones_ref[...]

    def normed(chunk, gain):
        t = _head_norm(chunk, gain, ones_bd)
        if rope:
            t = _rope(t, cos_ref[...], sa_ref[...], sb_ref[...])
        return t

    col = 0
    if with_uq:
        u_ref[0] = z[:, :FOURIER_WIDTH].astype(BF16)
        col = FOURIER_WIDTH
        qs = [normed(z[:, col + j * LANES: col + (j + 1) * LANES], gq_ref[...])
              for j in range(ATTN_WIDTH // LANES)]
        qt_ref[0] = jnp.concatenate(qs, axis=1).T.astype(BF16)
        col += ATTN_WIDTH
    for j in range(KV_WIDTH // LANES):
        t = normed(z[:, col + j * LANES: col + (j + 1) * LANES], gk_ref[...])
        k_ref[0, 2 * j] = t[:, :HEAD_DIM].astype(BF16)
        k_ref[0, 2 * j + 1] = t[:, HEAD_DIM:].astype(BF16)
    col += KV_WIDTH
    vt_ref[0] = z[:, col:col + KV_WIDTH].T.astype(BF16)


def _inproj(x, sc, sh, g, w, gq, gk, ones_bd, rope_tabs, with_uq):
    b, t, d = x.shape
    tm = min(TOK_TILE, t)
    wcols = w.shape[1]
    bm = sc.shape[0]
    mod_map = (lambda i, j: (i, 0, 0)) if bm == b else (lambda i, j: (0, 0, 0))
    const2 = lambda i, j: (0, 0)
    in_specs = [pl.BlockSpec((1, tm, d), lambda i, j: (i, j, 0)),
                pl.BlockSpec((1, 1, d), mod_map),
                pl.BlockSpec((1, 1, d), mod_map),
                pl.BlockSpec((1, d), const2),
                pl.BlockSpec((d, wcols), const2),
                pl.BlockSpec((1, LANES), const2),
                pl.BlockSpec((1, LANES), const2),
                pl.BlockSpec((LANES, LANES), const2)]
    args = [x, sc, sh, g, w, gq, gk, ones_bd]
    rope = rope_tabs is not None
    if rope:
        in_specs += [pl.BlockSpec((tm, LANES), lambda i, j: (j, 0))] * 3
        args += list(rope_tabs)
    out_specs, out_shape = [], []
    if with_uq:
        out_specs += [pl.BlockSpec((1, tm, FOURIER_WIDTH), lambda i, j: (i, j, 0)),
                      pl.BlockSpec((1, ATTN_WIDTH, tm), lambda i, j: (i, 0, j))]
        out_shape += [jax.ShapeDtypeStruct((b, t, FOURIER_WIDTH), BF16),
                      jax.ShapeDtypeStruct((b, ATTN_WIDTH, t), BF16)]
    out_specs += [pl.BlockSpec((1, KV_HEADS, tm, HEAD_DIM), lambda i, j: (i, 0, j, 0)),
                  pl.BlockSpec((1, KV_WIDTH, tm), lambda i, j: (i, 0, j))]
    out_shape += [jax.ShapeDtypeStruct((b, KV_HEADS, t, HEAD_DIM), BF16),
                  jax.ShapeDtypeStruct((b, KV_WIDTH, t), BF16)]
    return pl.pallas_call(
        functools.partial(_inproj_kernel, rope=rope, with_uq=with_uq),
        grid=(b, t // tm),
        in_specs=in_specs,
        out_specs=out_specs,
        out_shape=out_shape,
        compiler_params=_cp(("arbitrary", "arbitrary")),
        name="inproj_latent" if with_uq else "inproj_ctx",
    )(*args)


def _fourier_a_kernel(u_ref, c_ref, s_ref, yr_ref, yi_ref):
    u = u_ref[0]
    yr_ref[0] = jnp.dot(c_ref[...], u, preferred_element_type=F32).astype(BF16)
    yi_ref[0] = jnp.dot(s_ref[...], u, preferred_element_type=F32).astype(BF16)


def _fourier_b_kernel(yr_ref, yi_ref, m_ref, c_ref, s_ref, w_ref, o_ref, *, kb):
    y = jnp.concatenate([yr_ref[0], yi_ref[0]], axis=1)
    x = jnp.einsum("kab,kbc->kac", m_ref[...], y, preferred_element_type=F32)
    xr = x[:, :DFT_L1].reshape(kb * DFT_L1, FOURIER_WIDTH).astype(BF16)
    xi = x[:, DFT_L1:].reshape(kb * DFT_L1, FOURIER_WIDTH).astype(BF16)
    spec = (jnp.dot(xr, c_ref[...], preferred_element_type=F32)
            + jnp.dot(xi, s_ref[...], preferred_element_type=F32))
    o = jnp.dot(spec.astype(BF16), w_ref[...], preferred_element_type=F32)
    for j in range(kb):
        o_ref[0, :, j, :] = o[j * DFT_L1:(j + 1) * DFT_L1]


def _dft_tables(l):
    l2 = l // DFT_L1
    k2 = np.arange(l2)
    ang2 = 2.0 * np.pi * ((k2[:, None] * k2[None, :]) % l2) / l2
    c2 = np.cos(ang2)
    s2n = -np.sin(ang2)
    n1 = np.arange(DFT_L1)
    k = l2 * n1[None, :, None] + k2[:, None, None]
    ang = 2.0 * np.pi * ((k * n1[None, None, :]) % l) / l
    mr, mi = np.cos(ang), -np.sin(ang)
    m = np.concatenate([np.concatenate([mr, -mi], axis=2),
                        np.concatenate([mi, mr], axis=2)], axis=1)
    d = np.arange(HEAD_DIM)
    angc = 2.0 * np.pi * ((d[:, None] * d[None, :]) % HEAD_DIM) / HEAD_DIM
    scale = 1.0 / np.sqrt(float(l) * HEAD_DIM)
    eye = np.eye(FOURIER_HEADS)
    cbd = np.kron(eye, np.cos(angc) * scale)
    sbd = np.kron(eye, np.sin(angc) * scale)
    as_bf = lambda a: jnp.asarray(a, dtype=F32).astype(BF16)
    return as_bf(c2), as_bf(s2n), as_bf(m), as_bf(cbd), as_bf(sbd)


def _fourier(u, w_bd):
    b, l, fw = u.shape
    l2 = l // DFT_L1
    ncol = DFT_L1 * fw
    tn = min(4096, ncol)
    kb = min(8, l2)
    c2, s2n, m, cbd, sbd = _dft_tables(l)
    yr, yi = pl.pallas_call(
        _fourier_a_kernel,
        grid=(b, ncol // tn),
        in_specs=[pl.BlockSpec((1, l2, tn), lambda i, j: (i, 0, j)),
                  pl.BlockSpec((l2, l2), lambda i, j: (0, 0)),
                  pl.BlockSpec((l2, l2), lambda i, j: (0, 0))],
        out_specs=[pl.BlockSpec((1, l2, tn), lambda i, j: (i, 0, j))] * 2,
        out_shape=[jax.ShapeDtypeStruct((b, l2, ncol), BF16)] * 2,
        compiler_params=_cp(("arbitrary", "arbitrary")),
        name="fourier_a",
    )(u.reshape(b, l2, ncol), c2, s2n)
    yr = yr.reshape(b, l2, DFT_L1, fw)
    yi = yi.reshape(b, l2, DFT_L1, fw)
    out = pl.pallas_call(
        functools.partial(_fourier_b_kernel, kb=kb),
        grid=(b, l2 // kb),
        in_specs=[pl.BlockSpec((1, kb, DFT_L1, fw), lambda i, j: (i, j, 0, 0)),
                  pl.BlockSpec((1, kb, DFT_L1, fw), lambda i, j: (i, j, 0, 0)),
                  pl.BlockSpec((kb, 2 * DFT_L1, 2 * DFT_L1), lambda i, j: (j, 0, 0)),
                  pl.BlockSpec((fw, fw), lambda i, j: (0, 0)),
                  pl.BlockSpec((fw, fw), lambda i, j: (0, 0)),
                  pl.BlockSpec((fw, fw), lambda i, j: (0, 0))],
        out_specs=pl.BlockSpec((1, DFT_L1, kb, fw), lambda i, j: (i, 0, j, 0)),
        out_shape=jax.ShapeDtypeStruct((b, DFT_L1, l2, fw), F32),
        compiler_params=_cp(("arbitrary", "arbitrary")),
        name="fourier_b",
    )(yr, yi, m, cbd, sbd, w_bd)
    return out.reshape(b, l, fw)


def _attn_kernel(qt_ref, k_ref, vt_ref, kc_ref, vct_ref, o_ref, m_scr, acc_scr, *s_scrs, bk, nchunks):
    bq = qt_ref.shape[2]
    m_scr[...] = jnp.full(m_scr.shape, -jnp.inf, F32)
    acc_scr[...] = jnp.zeros(acc_scr.shape, F32)

    def scores(k, buf, h):
        nk = k.shape[0]
        qt = qt_ref[0, h * HEAD_DIM:(h + 1) * HEAD_DIM, :]
        part = nk // QK_SPLIT
        for j in range(QK_SPLIT):
            s_scrs[buf][j * part:(j + 1) * part, :] = jnp.dot(k[j * part:(j + 1) * part], qt,
                                                              preferred_element_type=F32)

    def softmax_pv(buf, h, vt):
        nk = vt.shape[1]
        s = s_scrs[buf][:nk, :]
        m_old = m_scr[h]
        m_new = jnp.maximum(m_old, jnp.max(s, axis=0, keepdims=True))
        alpha = jnp.exp2(m_old - m_new)
        p = jnp.exp2((s - m_new).astype(BF16))
        vt_ones = jnp.concatenate([vt, jnp.ones((ONES_ROWS, nk), BF16)], axis=0)
        acc_scr[h] = alpha * acc_scr[h] + jnp.dot(vt_ones, p, preferred_element_type=F32)
        m_scr[h] = m_new

    def keys(c):
        if isinstance(c, int) and c == nchunks:
            return kc_ref[0, 0]
        return k_ref[0, 0, pl.ds(pl.multiple_of(c * bk, bk), bk), :]

    def values(c):
        if isinstance(c, int) and c == nchunks:
            return vct_ref[0]
        return vt_ref[0, :, pl.ds(pl.multiple_of(c * bk, bk), bk)]

    nbuf = len(s_scrs)
    pair = 1
    ahead = nbuf - pair
    n_units = Q_PER_KV * (nchunks + 1)

    def group(us, chunk0=0):
        for u in us:
            ua = u + ahead
            if not (isinstance(chunk0, int) and ua >= n_units):
                scores(keys(chunk0 + ua // Q_PER_KV), ua % nbuf, ua % Q_PER_KV)
        for u in us:
            softmax_pv(u % nbuf, u % Q_PER_KV, values(chunk0 + u // Q_PER_KV))

    for u in range(min(ahead, n_units)):
        scores(keys(u // Q_PER_KV), u % nbuf, u % Q_PER_KV)

    chunks_per_iter = nbuf // Q_PER_KV
    n_iter = max(0, (Q_PER_KV * nchunks - ahead) // nbuf)

    def body(j, carry):
        for t in range(0, nbuf, pair):
            group(range(t, t + pair), j * chunks_per_iter)
        return carry

    lax.fori_loop(0, n_iter, body, 0)
    for t in range(n_iter * nbuf, n_units, pair):
        group(range(t, min(t + pair, n_units)))
    outs = [acc_scr[h, :HEAD_DIM, :] / acc_scr[h, HEAD_DIM:HEAD_DIM + 1, :] for h in range(Q_PER_KV)]
    o_t = jnp.concatenate(outs + [jnp.zeros((HEAD_DIM, bq), F32)], axis=0)
    o_ref[0, 0] = o_t.T[:, :Q_PER_KV * HEAD_DIM].astype(BF16)


def _attention(qt, k, vt, kc, vct):
    b, _, l = qt.shape
    c = kc.shape[2]
    bq = min(ATTN_BQ, l)
    bk = min(ATTN_BK, l)
    gw = Q_PER_KV * HEAD_DIM
    return pl.pallas_call(
        functools.partial(_attn_kernel, bk=bk, nchunks=l // bk),
        grid=(b, KV_HEADS, l // bq),
        in_specs=[pl.BlockSpec((1, gw, bq), lambda i, g, j: (i, g, j)),
                  pl.BlockSpec((1, 1, l, HEAD_DIM), lambda i, g, j: (i, g, 0, 0)),
                  pl.BlockSpec((1, HEAD_DIM, l), lambda i, g, j: (i, g, 0)),
                  pl.BlockSpec((1, 1, c, HEAD_DIM), lambda i, g, j: (i, g, 0, 0)),
                  pl.BlockSpec((1, HEAD_DIM, c), lambda i, g, j: (i, g, 0))],
        out_specs=pl.BlockSpec((1, 1, bq, gw), lambda i, g, j: (i, g, j, 0)),
        out_shape=jax.ShapeDtypeStruct((b, KV_HEADS, l, gw), BF16),
        scratch_shapes=[pltpu.VMEM((Q_PER_KV, 1, bq), F32),
                        pltpu.VMEM((Q_PER_KV, HEAD_DIM + ONES_ROWS, bq), F32),
                        *[pltpu.VMEM((max(bk, c), bq), F32)] * (2 * Q_PER_KV)],
        compiler_params=_cp(("arbitrary", "arbitrary", "arbitrary")),
        name="attention",
    )(qt, k, vt, kc, vct)


def _outproj_router_kernel(x_ref, f_ref, a_ref, g1_ref, wo_ref, n2g_ref, sc_ref, sh_ref,
                           wrh_ref, wrl_ref, bias_ref, xn_ref, hpa_ref, hpb_ref, te_ref, gt_ref, cnt_ref):
    tm = x_ref.shape[1]
    mix = jnp.dot(f_ref[0].astype(BF16), wo_ref[:FOURIER_WIDTH, :], preferred_element_type=F32)
    gw = Q_PER_KV * HEAD_DIM
    for g in range(KV_HEADS):
        r0 = FOURIER_WIDTH + g * gw
        mix += jnp.dot(a_ref[0, g], wo_ref[r0:r0 + gw, :], preferred_element_type=F32)
    xn = x_ref[0] + g1_ref[0] * mix
    xn_ref[0] = xn
    h2 = _modulated_norm(xn, n2g_ref[...], sc_ref[0], sh_ref[0])
    h2_hi = h2.astype(BF16)
    h2_lo = (h2 - h2_hi.astype(F32)).astype(BF16)
    hpa_ref[...], hpb_ref[...] = _pack_rows(h2)

    wrh = wrh_ref[...]
    logits = (lax.dot_general(wrh, h2_hi, _NT, preferred_element_type=F32)
              + lax.dot_general(wrh, h2_lo, _NT, preferred_element_type=F32)
              + lax.dot_general(wrl_ref[...], h2_hi, _NT, preferred_element_type=F32))
    scores = _sigmoid(logits)
    sel = scores + bias_ref[...]

    neg = jnp.float32(-jnp.inf)
    s3 = sel.reshape(N_EXPERT_GROUPS, EXPERTS_PER_GROUP, tm)
    i3 = lax.broadcasted_iota(I32, s3.shape, 1)
    m1 = jnp.max(s3, axis=1, keepdims=True)
    i1 = jnp.min(jnp.where(s3 == m1, i3, EXPERTS_PER_GROUP), axis=1, keepdims=True)
    m2 = jnp.max(jnp.where(i3 == i1, neg, s3), axis=1)
    gs = m1[:, 0, :] + m2
    gi = lax.broadcasted_iota(I32, gs.shape, 0)
    keep = jnp.zeros(gs.shape, jnp.bool_)
    for _ in range(TOPK_GROUPS):
        m = jnp.max(gs, axis=0, keepdims=True)
        idx = jnp.min(jnp.where(gs == m, gi, N_EXPERT_GROUPS), axis=0, keepdims=True)
        hit = gi == idx
        keep = keep | hit
        gs = jnp.where(hit, neg, gs)
    keep3 = jnp.broadcast_to(keep[:, None, :], s3.shape)
    selm = jnp.where(keep3, s3, neg).reshape(N_EXPERTS, tm)

    ei = lax.broadcasted_iota(I32, selm.shape, 0)
    multi = jnp.zeros(selm.shape, F32)
    idxs, gates = [], []
    for _ in range(TOP_K):
        m = jnp.max(selm, axis=0, keepdims=True)
        idx = jnp.min(jnp.where(selm == m, ei, N_EXPERTS), axis=0, keepdims=True)
        hit = ei == idx
        gates.append(jnp.sum(jnp.where(hit, scores, 0.0), axis=0, keepdims=True))
        idxs.append(idx)
        selm = jnp.where(hit, neg, selm)
        multi = multi + hit.astype(F32)
    gate = jnp.concatenate(gates, axis=0)
    gate = gate / jnp.sum(gate, axis=0, keepdims=True) * ROUTED_SCALE
    te_ref[...] = jnp.concatenate(idxs, axis=0)
    gt_ref[...] = gate
    ones = jnp.ones((8, tm), BF16)
    cnt_ref[0] = lax.dot_general(ones, multi.astype(BF16), _NT, preferred_element_type=F32)


def _outproj_router(x, four, attn, g1, w_out, n2g, sc2, sh2, wr_hi, wr_lo, bias):
    b, l, d = x.shape
    tm = min(TOK_TILE, l)
    tpb = l // tm
    n = b * l
    const2 = lambda i, j: (0, 0)
    mod_map = lambda i, j: (i, 0, 0)
    tok_map = lambda i, j: (0, i * tpb + j)
    return pl.pallas_call(
        _outproj_router_kernel,
        grid=(b, tpb),
        in_specs=[pl.BlockSpec((1, tm, d), lambda i, j: (i, j, 0)),
                  pl.BlockSpec((1, tm, FOURIER_WIDTH), lambda i, j: (i, j, 0)),
                  pl.BlockSpec((1, KV_HEADS, tm, Q_PER_KV * HEAD_DIM), lambda i, j: (i, 0, j, 0)),
                  pl.BlockSpec((1, 1, d), mod_map),
                  pl.BlockSpec((d, d), const2),
                  pl.BlockSpec((1, d), const2),
                  pl.BlockSpec((1, 1, d), mod_map),
                  pl.BlockSpec((1, 1, d), mod_map),
                  pl.BlockSpec((N_EXPERTS, d), const2),
                  pl.BlockSpec((N_EXPERTS, d), const2),
                  pl.BlockSpec((N_EXPERTS, 1), const2)],
        out_specs=[pl.BlockSpec((1, tm, d), lambda i, j: (i, j, 0)),
                   pl.BlockSpec((tm, PACK_W), lambda i, j: (i * tpb + j, 0)),
                   pl.BlockSpec((tm, PACK_W), lambda i, j: (i * tpb + j, 0)),
                   pl.BlockSpec((TOP_K, tm), tok_map),
                   pl.BlockSpec((TOP_K, tm), tok_map),
                   pl.BlockSpec((1, 8, N_EXPERTS), lambda i, j: (i * tpb + j, 0, 0))],
        out_shape=[jax.ShapeDtypeStruct((b, l, d), F32),
                   jax.ShapeDtypeStruct((n, PACK_W), I32),
                   jax.ShapeDtypeStruct((n, PACK_W), I32),
                   jax.ShapeDtypeStruct((TOP_K, n), I32),
                   jax.ShapeDtypeStruct((TOP_K, n), F32),
                   jax.ShapeDtypeStruct((n // tm, 8, N_EXPERTS), F32)],
        compiler_params=_cp(("arbitrary", "arbitrary")),
        name="outproj_router",
    )(x, four, attn, g1, w_out, n2g, sc2, sh2, wr_hi, wr_lo, bias)


def _dest_kernel(te_ref, base_ref, tri_ref, d_ref):
    te = te_ref[...]
    tm = te.shape[1]
    ei = lax.broadcasted_iota(I32, (N_EXPERTS, tm), 0)
    hits = [ei == te[k:k + 1, :] for k in range(TOP_K)]
    multi = hits[0].astype(F32)
    for k in range(1, TOP_K):
        multi = multi + hits[k].astype(F32)
    rank = jnp.dot(multi.astype(BF16), tri_ref[...], preferred_element_type=F32)
    pos = rank + base_ref[0]
    rows = [jnp.sum(jnp.where(hits[k], pos, 0.0), axis=0, keepdims=True) for k in range(TOP_K)]
    d_ref[...] = jnp.concatenate(rows, axis=0).astype(I32)


def _dest_rows(top_e, base, tm):
    n = top_e.shape[1]
    tri = jnp.asarray(np.triu(np.ones((tm, tm), np.float32), 1)).astype(BF16)
    return pl.pallas_call(
        _dest_kernel,
        grid=(n // tm,),
        in_specs=[pl.BlockSpec((TOP_K, tm), lambda i: (0, i)),
                  pl.BlockSpec((1, N_EXPERTS, 1), lambda i: (i, 0, 0)),
                  pl.BlockSpec((tm, tm), lambda i: (0, 0))],
        out_specs=pl.BlockSpec((TOP_K, tm), lambda i: (0, i)),
        out_shape=jax.ShapeDtypeStruct((TOP_K, n), I32),
        compiler_params=_cp(("arbitrary",)),
        name="dest_rows",
    )(top_e, base, tri)


def _sc_mesh():
    return plsc.VectorSubcoreMesh(core_axis_name="c", subcore_axis_name="s",
                                  num_cores=SC_CORES, num_subcores=SC_SUBCORES)


def _sc_scatter_rows(src, dest, n_rows):
    n, w = src.shape
    kk = dest.shape[0]

    @pl.kernel(out_type=jax.ShapeDtypeStruct((n_rows, w), src.dtype), mesh=_sc_mesh(), name="sc_scatter_rows")
    def scatter(x_hbm, d_hbm, o_hbm):
        def body(x_vmem, d_vmem):
            for k in range(kk):
                pltpu.sync_copy(x_vmem, o_hbm.at[d_vmem.at[k]])

        pltpu.emit_pipeline(
            body,
            grid=(n // SC_WINDOW,),
            in_specs=[pl.BlockSpec((SC_WINDOW, w), lambda i: (i, 0)),
                      pl.BlockSpec((kk, SC_WINDOW), lambda i: (0, i))],
            out_specs=[],
            core_axis_name=("c", "s"),
            dimension_semantics=(pltpu.PARALLEL,),
        )(x_hbm, d_hbm)

    return scatter(src, dest)


def _swiglu(quarters, wgu, wdn):
    ag = None
    for j, q in enumerate(quarters):
        part = jnp.dot(q, wgu[j * PACK_W:(j + 1) * PACK_W, :], preferred_element_type=F32)
        ag = part if ag is None else ag + part
    ff = ag.shape[1] // 2
    a, g = ag[:, :ff], ag[:, ff:]
    mid = (a * _sigmoid(a) * g).astype(BF16)
    return jnp.dot(mid, wdn[...], preferred_element_type=F32)


def _expert_kernel(blk0_ref, nblk_ref, cnt_ref, xa_hbm, xb_hbm, wgu_ref, wdn_ref, ya_hbm, yb_hbm,
                   xbuf, ybuf, wgu_s, wdn_s, in_sem, out_sem, pending):
    e = pl.program_id(0)
    last = pl.num_programs(0) - 1
    rb = xbuf.shape[2]
    nb = nblk_ref[e]
    blk0 = blk0_ref[e]
    n_used = blk0_ref[last] + nblk_ref[last]
    x_hbm = (xa_hbm, xb_hbm)
    y_hbm = (ya_hbm, yb_hbm)

    def x_copy(g, t):
        slot = g % X_SLOTS
        return pltpu.make_async_copy(x_hbm[t].at[pl.ds(g * rb, rb)], xbuf.at[slot, t], in_sem.at[slot, t])

    def x_start(g):
        @pl.when(g < n_used)
        def _():
            for t in range(2):
                x_copy(g, t).start()

    def y_copy(g, slot, t):
        return pltpu.make_async_copy(ybuf.at[slot, t], y_hbm[t].at[pl.ds(g * rb, rb)], out_sem.at[slot, t])

    def drain(slot):
        @pl.when(pending[slot] == 1)
        def _():
            for t in range(2):
                y_copy(0, slot, t).wait()
            pending[slot] = 0

    @pl.when(e == 0)
    def _():
        pending[0] = 0
        pending[1] = 0
        for g in range(X_SLOTS - 1):
            x_start(g)

    @pl.when(nb > 0)
    def _():
        wgu_s[...] = wgu_ref[0].astype(BF16)
        wdn_s[...] = wdn_ref[0].astype(BF16)

        def body(j, carry):
            g = blk0 + j
            for t in range(2):
                x_copy(g, t).wait()
            x_start(g + X_SLOTS - 1)
            xs = g % X_SLOTS
            ys = g % 2
            drain(ys)
            valid = lax.broadcasted_iota(I32, (rb, PACK_W), 0) < cnt_ref[e] - j * rb
            xa = jnp.where(valid, xbuf[xs, 0], 0)
            xb = jnp.where(valid, xbuf[xs, 1], 0)
            quarters = [q.astype(BF16) for q in _unpack_rows(xa, xb)]
            ya, yb = _pack_rows(_swiglu(quarters, wgu_s, wdn_s))
            ybuf[ys, 0] = ya
            ybuf[ys, 1] = yb
            for t in range(2):
                y_copy(g, ys, t).start()
            pending[ys] = 1
            return carry

        lax.fori_loop(0, nb, body, 0)

    @pl.when(e == last)
    def _():
        drain(0)
        drain(1)
        n_total = ya_hbm.shape[0] // rb
        ybuf[0, 0] = jnp.zeros((rb, PACK_W), I32)

        def zero_copy(b, t):
            return pltpu.make_async_copy(ybuf.at[0, 0], y_hbm[t].at[pl.ds(b * rb, rb)], out_sem.at[0, t])

        def start_zero(b, carry):
            for t in range(2):
                zero_copy(b, t).start()
            return carry

        def wait_zero(b, carry):
            for t in range(2):
                zero_copy(b, t).wait()
            return carry

        lax.fori_loop(n_used, n_total, start_zero, 0)
        lax.fori_loop(n_used, n_total, wait_zero, 0)


def _experts(blk0, nblk, cnt, xa, xb, w_gu, w_down):
    n_rows = xa.shape[0]
    rb = ROW_BLOCK
    n_experts, d, ff2 = w_gu.shape
    w_map = lambda e, b0, nb, ct: (e, 0, 0)
    grid_spec = pltpu.PrefetchScalarGridSpec(
        num_scalar_prefetch=3,
        grid=(n_experts,),
        in_specs=[pl.BlockSpec(memory_space=pl.ANY),
                  pl.BlockSpec(memory_space=pl.ANY),
                  pl.BlockSpec((1, d, ff2), w_map),
                  pl.BlockSpec((1, ff2 // 2, d), w_map)],
        out_specs=[pl.BlockSpec(memory_space=pl.ANY)] * 2,
        scratch_shapes=[pltpu.VMEM((X_SLOTS, 2, rb, PACK_W), I32), pltpu.VMEM((2, 2, rb, PACK_W), I32),
                        pltpu.VMEM((d, ff2), BF16), pltpu.VMEM((ff2 // 2, d), BF16),
                        pltpu.SemaphoreType.DMA((X_SLOTS, 2)), pltpu.SemaphoreType.DMA((2, 2)),
                        pltpu.SMEM((2,), I32)],
    )
    return pl.pallas_call(
        _expert_kernel,
        grid_spec=grid_spec,
        out_shape=[jax.ShapeDtypeStruct((n_rows, PACK_W), I32)] * 2,
        compiler_params=_cp(("arbitrary",)),
        name="experts",
    )(blk0, nblk, cnt, xa, xb, w_gu, w_down)


def _sc_gather_rows(table, idx):
    m = idx.shape[1]
    w = table.shape[1]

    @pl.kernel(out_type=jax.ShapeDtypeStruct((m, w), table.dtype), mesh=_sc_mesh(), name="sc_gather_rows")
    def gather(t_hbm, i_hbm, o_hbm):
        def body(i_vmem, o_vmem):
            pltpu.sync_copy(t_hbm.at[i_vmem.at[0]], o_vmem)

        pltpu.emit_pipeline(
            body,
            grid=(m // SC_WINDOW,),
            in_specs=[pl.BlockSpec((1, SC_WINDOW), lambda i: (0, i))],
            out_specs=[pl.BlockSpec((SC_WINDOW, w), lambda i: (i, 0))],
            core_axis_name=("c", "s"),
            dimension_semantics=(pltpu.PARALLEL,),
        )(i_hbm, o_hbm)

    return gather(table, idx)


def _combine_kernel(gate_ref, xn_ref, hpa_ref, hpb_ref, g2_ref, wsgu_ref, wsdn_ref, fng_ref,
                    yga_ref, ygb_ref, o_ref):
    quarters = [q.astype(BF16) for q in _unpack_rows(hpa_ref[...], hpb_ref[...])]
    shared = _swiglu(quarters, wsgu_ref, wsdn_ref)
    gate = gate_ref[...]
    acc = None
    for k in range(TOP_K):
        gk = gate[:, k:k + 1]
        rows = [gk * q for q in _unpack_rows(yga_ref[k], ygb_ref[k])]
        acc = rows if acc is None else [a + r for a, r in zip(acc, rows)]
    y = jnp.concatenate(acc, axis=1) + shared
    xo = xn_ref[...] + g2_ref[0] * y
    ms = jnp.mean(xo * xo, axis=-1, keepdims=True)
    o_ref[...] = xo * lax.rsqrt(ms + RMS_EPS) * fng_ref[...]


def _combine(dest, gate_t, xn, hpa, hpb, g2, ws_gu, ws_dn, fng, ya, yb, tokens_per_batch):
    n, d = xn.shape
    tm = min(COMB_TILE, tokens_per_batch)
    nt = n // tm
    tpb = tokens_per_batch // tm
    const2 = lambda i: (0, 0)
    row_map = lambda i: (i, 0)
    idx = dest.reshape(1, TOP_K * n)
    yga = _sc_gather_rows(ya, idx).reshape(TOP_K, n, PACK_W)
    ygb = _sc_gather_rows(yb, idx).reshape(TOP_K, n, PACK_W)
    return pl.pallas_call(
        _combine_kernel,
        grid=(nt,),
        in_specs=[pl.BlockSpec((tm, TOP_K), row_map),
                  pl.BlockSpec((tm, d), row_map),
                  pl.BlockSpec((tm, PACK_W), row_map),
                  pl.BlockSpec((tm, PACK_W), row_map),
                  pl.BlockSpec((1, 1, d), lambda i: (i // tpb, 0, 0)),
                  pl.BlockSpec(ws_gu.shape, const2),
                  pl.BlockSpec(ws_dn.shape, const2),
                  pl.BlockSpec((1, d), const2),
                  pl.BlockSpec((TOP_K, tm, PACK_W), lambda i: (0, i, 0)),
                  pl.BlockSpec((TOP_K, tm, PACK_W), lambda i: (0, i, 0))],
        out_specs=pl.BlockSpec((tm, d), row_map),
        out_shape=jax.ShapeDtypeStruct((n, d), F32),
        compiler_params=_cp(("arbitrary",)),
        name="combine",
    )(gate_t, xn, hpa, hpb, g2, ws_gu, ws_dn, fng, yga, ygb)


def _rope_tables(l):
    rows = l // GRID_W
    row = np.repeat(np.arange(rows, dtype=np.float32), GRID_W)
    col = np.tile(np.arange(GRID_W, dtype=np.float32), rows)
    n_freq = AXIS_ROT // 2
    inv_freq = (np.float32(ROPE_THETA) ** (-np.arange(n_freq, dtype=np.float32) / n_freq)).astype(np.float32)
    ang_r = row[:, None] * inv_freq
    ang_c = col[:, None] * inv_freq
    ang = np.concatenate([ang_r, ang_r, ang_c, ang_c], axis=-1).astype(np.float64)
    cos, sin = np.cos(ang), np.sin(ang)
    lane = np.arange(HEAD_DIM) % AXIS_ROT
    first = lane < AXIS_ROT // 2
    sin_a = np.where(first[None, :], -sin, 0.0)
    sin_b = np.where(first[None, :], 0.0, sin)
    two = lambda a: jnp.asarray(np.concatenate([a, a], axis=1), dtype=F32)
    return two(cos), two(sin_a), two(sin_b)


def _layer(x, ctx, c, c_ctx, lw, moe_w, fng):
    norm1_g, w_ada, b_ada, w_in, w_fourier, q_norm_g, k_norm_g, w_out, norm2_g = lw
    w_router, router_bias, w_expert_gu, w_expert_down, w_shared_gu, w_shared_down = moe_w
    b, l, d = x.shape
    n = b * l

    cond = jnp.concatenate([c, c_ctx[None, :], jnp.zeros((8 - b - 1, d), F32)], axis=0)
    mod = _adaln(cond, w_ada, b_ada)
    sh1, sc1, g1, sh2, sc2, g2 = [m[:b, None, :] for m in jnp.split(mod, 6, axis=-1)]
    csh1, csc1 = [m[b:b + 1, None, :] for m in jnp.split(mod, 6, axis=-1)[:2]]

    w_in_bf = w_in.astype(BF16)
    two = lambda g, s: jnp.tile(g * s, 2).reshape(1, LANES)
    gq, gk = two(q_norm_g, ATTN_SCALE * float(np.log2(np.e))), two(k_norm_g, 1.0)
    ones_bd = jnp.asarray(np.kron(np.eye(2), np.ones((HEAD_DIM, HEAD_DIM))), dtype=F32).astype(BF16)
    n1g = norm1_g.reshape(1, d)

    kc, vct = _inproj(ctx, csc1, csh1, n1g, w_in_bf[:, KV_COL0:], gq, gk, ones_bd, None, False)
    u, qt, k, vt = _inproj(x, sc1, sh1, n1g, w_in_bf, gq, gk, ones_bd, _rope_tables(l), True)

    w_bd = jnp.zeros((FOURIER_WIDTH, FOURIER_WIDTH), F32)
    for h in range(FOURIER_HEADS):
        w_bd = lax.dynamic_update_slice(w_bd, w_fourier[h], (h * HEAD_DIM, h * HEAD_DIM))
    four = _fourier(u, w_bd.astype(BF16))
    attn = _attention(qt, k, vt, kc, vct)

    wr_t = w_router.T
    wr_hi = wr_t.astype(BF16)
    wr_lo = (wr_t - wr_hi.astype(F32)).astype(BF16)
    xn, hpa, hpb, top_e, gate, cnt = _outproj_router(
        x, four, attn, g1, w_out.astype(BF16), norm2_g.reshape(1, d), sc2, sh2,
        wr_hi, wr_lo, router_bias.reshape(N_EXPERTS, 1))

    tm = min(TOK_TILE, l)
    rb = ROW_BLOCK
    counts = cnt[:, 0, :].astype(I32)
    total = jnp.sum(counts, axis=0)
    padded = (total + rb - 1) // rb * rb
    pad_end = jnp.cumsum(padded)
    pad_start = pad_end - padded
    base = pad_start[None, :] + jnp.cumsum(counts, axis=0) - counts
    n_blocks = n * TOP_K // rb + N_EXPERTS

    dest = _dest_rows(top_e, base.astype(F32)[:, :, None], tm)
    xa = _sc_scatter_rows(hpa, dest, n_blocks * rb)
    xb = _sc_scatter_rows(hpb, dest, n_blocks * rb)
    ya, yb = _experts(pad_start // rb, padded // rb, total, xa, xb, w_expert_gu, w_expert_down)
    return _combine(dest, gate.T, xn.reshape(n, d), hpa, hpb, g2, w_shared_gu.astype(BF16),
                    w_shared_down.astype(BF16), fng, ya, yb, l)


def kernel(x, c, ctx, c_ctx, norm1_g, w_ada, b_ada, w_in, w_fourier, q_norm_g, k_norm_g, w_out, norm2_g,
           w_router, router_bias, w_expert_gu, w_expert_down, w_shared_gu, w_shared_down, final_norm_g):
    depth = norm1_g.shape[0]
    assert depth == 1, "context update between layers is not implemented"
    b, l, d = x.shape
    lw = (norm1_g[0], w_ada[0], b_ada[0], w_in[0], w_fourier[0], q_norm_g[0], k_norm_g[0], w_out[0], norm2_g[0])
    moe_w = (w_router[0], router_bias[0], w_expert_gu[0], w_expert_down[0], w_shared_gu[0], w_shared_down[0])
    out = _layer(x, ctx, c, c_ctx, lw, moe_w, final_norm_g.reshape(1, d))
    return out.reshape(b, l, d)
```

```python
import functools

import numpy as np
import jax
import jax.numpy as jnp
from jax import lax
from jax.experimental import pallas as pl
from jax.experimental.pallas import tpu as pltpu
from jax.experimental.pallas import tpu_sc as plsc

F32 = jnp.float32
BF16 = jnp.bfloat16
I32 = jnp.int32

D_MODEL = 1024
GRID_W = 64
HEAD_DIM = 64
FOURIER_HEADS = 4
FOURIER_WIDTH = 256
ATTN_HEADS = 12
KV_HEADS = 4
Q_PER_KV = 3
ATTN_WIDTH = 768
KV_WIDTH = 256
KV_COL0 = 1024
IN_WIDTH = 1536
ATTN_SCALE = HEAD_DIM ** -0.5
ROPE_THETA = 10000.0
AXIS_ROT = HEAD_DIM // 2
N_EXPERTS = 256
TOP_K = 8
N_EXPERT_GROUPS = 8
TOPK_GROUPS = 4
EXPERTS_PER_GROUP = 32
EXPERT_FF = 256
ROUTED_SCALE = 2.5
RMS_EPS = 1e-6

LANES = 128
SUBLANES = 8
PACK_W = D_MODEL // 4
DFT_L1 = 64
VMEM_LIMIT = 48 * 1024 * 1024

TOK_TILE = 512
ATTN_BQ = 256
ATTN_BK = 1024
QK_SPLIT = 2
ONES_ROWS = 16
ROW_BLOCK = 512
X_SLOTS = 4
COMB_TILE = 256
SC_WINDOW = 128
SC_CORES = 2
SC_SUBCORES = 16

_HI = lax.Precision.HIGHEST
_NT = (((1,), (1,)), ((), ()))


def _cp(sem, vmem=VMEM_LIMIT):
    return pltpu.CompilerParams(dimension_semantics=sem, vmem_limit_bytes=vmem)


def _sigmoid(v):
    return 1.0 / (1.0 + jnp.exp(-v))


def _pack_rows(v):
    w = v.shape[1] // 4

    def pack(lo, hi):
        lo = lax.bitcast_convert_type(lo.astype(BF16).astype(F32), I32)
        hi = lax.bitcast_convert_type(hi.astype(BF16).astype(F32), I32)
        return lax.shift_right_logical(lo, 16) | (hi & jnp.int32(-65536))

    return pack(v[:, :w], v[:, 2 * w:3 * w]), pack(v[:, w:2 * w], v[:, 3 * w:])


def _unpack_rows(a, b):
    lo = lambda t: lax.bitcast_convert_type(lax.shift_left(t, 16), F32)
    hi = lambda t: lax.bitcast_convert_type(t & jnp.int32(-65536), F32)
    return [lo(a), lo(b), hi(a), hi(b)]


def _adaln_kernel(c_ref, w_ref, b_ref, o_ref):
    c = c_ref[...]
    s = c * _sigmoid(c)
    o_ref[...] = jnp.dot(s, w_ref[...], precision=_HI, preferred_element_type=F32) + b_ref[...]


def _adaln(cond, w_ada, b_ada):
    rows, d = cond.shape
    n = w_ada.shape[1]
    tn = 1536
    return pl.pallas_call(
        _adaln_kernel,
        grid=(n // tn,),
        in_specs=[pl.BlockSpec((rows, d), lambda j: (0, 0)),
                  pl.BlockSpec((d, tn), lambda j: (0, j)),
                  pl.BlockSpec((1, tn), lambda j: (0, j))],
        out_specs=pl.BlockSpec((rows, tn), lambda j: (0, j)),
        out_shape=jax.ShapeDtypeStruct((rows, n), F32),
        compiler_params=_cp(("arbitrary",)),
        name="adaln",
    )(cond, w_ada, b_ada.reshape(1, n))


def _modulated_norm(x, g, sc, sh):
    ms = jnp.mean(x * x, axis=-1, keepdims=True)
    return x * lax.rsqrt(ms + RMS_EPS) * g * (1.0 + sc) + sh


def _head_norm(chunk, gain, ones_bd):
    sq = chunk * chunk
    hi = sq.astype(BF16)
    lo = (sq - hi.astype(F32)).astype(BF16)
    ss = (jnp.dot(hi, ones_bd, preferred_element_type=F32)
          + jnp.dot(lo, ones_bd, preferred_element_type=F32))
    return chunk * lax.rsqrt(ss * (1.0 / HEAD_DIM) + RMS_EPS) * gain


def _rope(t, cos, sin_a, sin_b):
    return (t * cos + pltpu.roll(t, LANES - AXIS_ROT // 2, 1) * sin_a
            + pltpu.roll(t, AXIS_ROT // 2, 1) * sin_b)


def _inproj_kernel(*refs, rope, with_uq):
    if rope:
        (x_ref, sc_ref, sh_ref, g_ref, w_ref, gq_ref, gk_ref, ones_ref,
         cos_ref, sa_ref, sb_ref) = refs[:11]
        outs = refs[11:]
    else:
        x_ref, sc_ref, sh_ref, g_ref, w_ref, gq_ref, gk_ref, ones_ref = refs[:8]
        outs = refs[8:]
    if with_uq:
        u_ref, qt_ref, k_ref, vt_ref = outs
    else:
        k_ref, vt_ref = outs

    h = _modulated_norm(x_ref[0], g_ref[...], sc_ref[0], sh_ref[0])
    z = jnp.dot(h.astype(BF16), w_ref[...], preferred_element_type=F32)
    ones_bd = ones_ref[...]

    def normed(chunk, gain):
        t = _head_norm(chunk, gain, ones_bd)
        if rope:
            t = _rope(t, cos_ref[...], sa_ref[...], sb_ref[...])
        return t

    col = 0
    if with_uq:
        u_ref[0] = z[:, :FOURIER_WIDTH].astype(BF16)
        col = FOURIER_WIDTH
        qs = [normed(z[:, col + j * LANES: col + (j + 1) * LANES], gq_ref[...])
              for j in range(ATTN_WIDTH // LANES)]
        qt_ref[0] = jnp.concatenate(qs, axis=1).T.astype(BF16)
        col += ATTN_WIDTH
    for j in range(KV_WIDTH // LANES):
        t = normed(z[:, col + j * LANES: col + (j + 1) * LANES], gk_ref[...])
        k_ref[0, 2 * j] = t[:, :HEAD_DIM].astype(BF16)
        k_ref[0, 2 * j + 1] = t[:, HEAD_DIM:].astype(BF16)
    col += KV_WIDTH
    vt_ref[0] = z[:, col:col + KV_WIDTH].T.astype(BF16)


def _inproj(x, sc, sh, g, w, gq, gk, ones_bd, rope_tabs, with_uq):
    b, t, d = x.shape
    tm = min(TOK_TILE, t)
    wcols = w.shape[1]
    bm = sc.shape[0]
    mod_map = (lambda i, j: (i, 0, 0)) if bm == b else (lambda i, j: (0, 0, 0))
    const2 = lambda i, j: (0, 0)
    in_specs = [pl.BlockSpec((1, tm, d), lambda i, j: (i, j, 0)),
                pl.BlockSpec((1, 1, d), mod_map),
                pl.BlockSpec((1, 1, d), mod_map),
                pl.BlockSpec((1, d), const2),
                pl.BlockSpec((d, wcols), const2),
                pl.BlockSpec((1, LANES), const2),
                pl.BlockSpec((1, LANES), const2),
                pl.BlockSpec((LANES, LANES), const2)]
    args = [x, sc, sh, g, w, gq, gk, ones_bd]
    rope = rope_tabs is not None
    if rope:
        in_specs += [pl.BlockSpec((tm, LANES), lambda i, j: (j, 0))] * 3
        args += list(rope_tabs)
    out_specs, out_shape = [], []
    if with_uq:
        out_specs += [pl.BlockSpec((1, tm, FOURIER_WIDTH), lambda i, j: (i, j, 0)),
                      pl.BlockSpec((1, ATTN_WIDTH, tm), lambda i, j: (i, 0, j))]
        out_shape += [jax.ShapeDtypeStruct((b, t, FOURIER_WIDTH), BF16),
                      jax.ShapeDtypeStruct((b, ATTN_WIDTH, t), BF16)]
    out_specs += [pl.BlockSpec((1, KV_HEADS, tm, HEAD_DIM), lambda i, j: (i, 0, j, 0)),
                  pl.BlockSpec((1, KV_WIDTH, tm), lambda i, j: (i, 0, j))]
    out_shape += [jax.ShapeDtypeStruct((b, KV_HEADS, t, HEAD_DIM), BF16),
                  jax.ShapeDtypeStruct((b, KV_WIDTH, t), BF16)]
    return pl.pallas_call(
        functools.partial(_inproj_kernel, rope=rope, with_uq=with_uq),
        grid=(b, t // tm),
        in_specs=in_specs,
        out_specs=out_specs,
        out_shape=out_shape,
        compiler_params=_cp(("arbitrary", "arbitrary")),
        name="inproj_latent" if with_uq else "inproj_ctx",
    )(*args)


def _fourier_a_kernel(u_ref, c_ref, s_ref, yr_ref, yi_ref):
    u = u_ref[0]
    yr_ref[0] = jnp.dot(c_ref[...], u, preferred_element_type=F32).astype(BF16)
    yi_ref[0] = jnp.dot(s_ref[...], u, preferred_element_type=F32).astype(BF16)


def _fourier_b_kernel(yr_ref, yi_ref, m_ref, c_ref, s_ref, w_ref, o_ref, *, kb):
    y = jnp.concatenate([yr_ref[0], yi_ref[0]], axis=1)
    x = jnp.einsum("kab,kbc->kac", m_ref[...], y, preferred_element_type=F32)
    xr = x[:, :DFT_L1].reshape(kb * DFT_L1, FOURIER_WIDTH).astype(BF16)
    xi = x[:, DFT_L1:].reshape(kb * DFT_L1, FOURIER_WIDTH).astype(BF16)
    spec = (jnp.dot(xr, c_ref[...], preferred_element_type=F32)
            + jnp.dot(xi, s_ref[...], preferred_element_type=F32))
    o = jnp.dot(spec.astype(BF16), w_ref[...], preferred_element_type=F32)
    for j in range(kb):
        o_ref[0, :, j, :] = o[j * DFT_L1:(j + 1) * DFT_L1]


def _dft_tables(l):
    l2 = l // DFT_L1
    k2 = np.arange(l2)
    ang2 = 2.0 * np.pi * ((k2[:, None] * k2[None, :]) % l2) / l2
    c2 = np.cos(ang2)
    s2n = -np.sin(ang2)
    n1 = np.arange(DFT_L1)
    k = l2 * n1[None, :, None] + k2[:, None, None]
    ang = 2.0 * np.pi * ((k * n1[None, None, :]) % l) / l
    mr, mi = np.cos(ang), -np.sin(ang)
    m = np.concatenate([np.concatenate([mr, -mi], axis=2),
                        np.concatenate([mi, mr], axis=2)], axis=1)
    d = np.arange(HEAD_DIM)
    angc = 2.0 * np.pi * ((d[:, None] * d[None, :]) % HEAD_DIM) / HEAD_DIM
    scale = 1.0 / np.sqrt(float(l) * HEAD_DIM)
    eye = np.eye(FOURIER_HEADS)
    cbd = np.kron(eye, np.cos(angc) * scale)
    sbd = np.kron(eye, np.sin(angc) * scale)
    as_bf = lambda a: jnp.asarray(a, dtype=F32).astype(BF16)
    return as_bf(c2), as_bf(s2n), as_bf(m), as_bf(cbd), as_bf(sbd)


def _fourier(u, w_bd):
    b, l, fw = u.shape
    l2 = l // DFT_L1
    ncol = DFT_L1 * fw
    tn = min(4096, ncol)
    kb = min(SUBLANES, l2)
    c2, s2n, m, cbd, sbd = _dft_tables(l)
    yr, yi = pl.pallas_call(
        _fourier_a_kernel,
        grid=(b, ncol // tn),
        in_specs=[pl.BlockSpec((1, l2, tn), lambda i, j: (i, 0, j)),
                  pl.BlockSpec((l2, l2), lambda i, j: (0, 0)),
                  pl.BlockSpec((l2, l2), lambda i, j: (0, 0))],
        out_specs=[pl.BlockSpec((1, l2, tn), lambda i, j: (i, 0, j))] * 2,
        out_shape=[jax.ShapeDtypeStruct((b, l2, ncol), BF16)] * 2,
        compiler_params=_cp(("arbitrary", "arbitrary")),
        name="fourier_a",
    )(u.reshape(b, l2, ncol), c2, s2n)
    yr = yr.reshape(b, l2, DFT_L1, fw)
    yi = yi.reshape(b, l2, DFT_L1, fw)
    out = pl.pallas_call(
        functools.partial(_fourier_b_kernel, kb=kb),
        grid=(b, l2 // kb),
        in_specs=[pl.BlockSpec((1, kb, DFT_L1, fw), lambda i, j: (i, j, 0, 0)),
                  pl.BlockSpec((1, kb, DFT_L1, fw), lambda i, j: (i, j, 0, 0)),
                  pl.BlockSpec((kb, 2 * DFT_L1, 2 * DFT_L1), lambda i, j: (j, 0, 0)),
                  pl.BlockSpec((fw, fw), lambda i, j: (0, 0)),
                  pl.BlockSpec((fw, fw), lambda i, j: (0, 0)),
                  pl.BlockSpec((fw, fw), lambda i, j: (0, 0))],
        out_specs=pl.BlockSpec((1, DFT_L1, kb, fw), lambda i, j: (i, 0, j, 0)),
        out_shape=jax.ShapeDtypeStruct((b, DFT_L1, l2, fw), F32),
        compiler_params=_cp(("arbitrary", "arbitrary")),
        name="fourier_b",
    )(yr, yi, m, cbd, sbd, w_bd)
    return out.reshape(b, l, fw)


def _attn_kernel(qt_ref, k_ref, vt_ref, kc_ref, vct_ref, o_ref, m_scr, acc_scr, *s_scrs, bk, nchunks):
    bq = qt_ref.shape[2]
    m_scr[...] = jnp.full(m_scr.shape, -jnp.inf, F32)
    acc_scr[...] = jnp.zeros(acc_scr.shape, F32)

    def scores(k, buf, h):
        nk = k.shape[0]
        qt = qt_ref[0, h * HEAD_DIM:(h + 1) * HEAD_DIM, :]
        part = nk // QK_SPLIT
        for j in range(QK_SPLIT):
            s_scrs[buf][j * part:(j + 1) * part, :] = jnp.dot(k[j * part:(j + 1) * part], qt,
                                                              preferred_element_type=F32)

    def softmax_pv(buf, h, vt):
        nk = vt.shape[1]
        s = s_scrs[buf][:nk, :]
        m_old = m_scr[h]
        m_new = jnp.maximum(m_old, jnp.max(s, axis=0, keepdims=True))
        alpha = jnp.exp2(m_old - m_new)
        p = jnp.exp2((s - m_new).astype(BF16))
        vt_ones = jnp.concatenate([vt, jnp.ones((ONES_ROWS, nk), BF16)], axis=0)
        acc_scr[h] = alpha * acc_scr[h] + jnp.dot(vt_ones, p, preferred_element_type=F32)
        m_scr[h] = m_new

    def keys(c):
        if isinstance(c, int) and c == nchunks:
            return kc_ref[0, 0]
        return k_ref[0, 0, pl.ds(pl.multiple_of(c * bk, bk), bk), :]

    def values(c):
        if isinstance(c, int) and c == nchunks:
            return vct_ref[0]
        return vt_ref[0, :, pl.ds(pl.multiple_of(c * bk, bk), bk)]

    nbuf = len(s_scrs)
    pair = 1
    ahead = nbuf - pair
    n_units = Q_PER_KV * (nchunks + 1)

    def group(us, chunk0=0):
        for u in us:
            ua = u + ahead
            if not (isinstance(chunk0, int) and ua >= n_units):
                scores(keys(chunk0 + ua // Q_PER_KV), ua % nbuf, ua % Q_PER_KV)
        for u in us:
            softmax_pv(u % nbuf, u % Q_PER_KV, values(chunk0 + u // Q_PER_KV))

    for u in range(min(ahead, n_units)):
        scores(keys(u // Q_PER_KV), u % nbuf, u % Q_PER_KV)

    chunks_per_iter = nbuf // Q_PER_KV
    n_iter = max(0, (Q_PER_KV * nchunks - ahead) // nbuf)

    def body(j, carry):
        for t in range(0, nbuf, pair):
            group(range(t, t + pair), j * chunks_per_iter)
        return carry

    lax.fori_loop(0, n_iter, body, 0)
    for t in range(n_iter * nbuf, n_units, pair):
        group(range(t, min(t + pair, n_units)))
    outs = [acc_scr[h, :HEAD_DIM, :] / acc_scr[h, HEAD_DIM:HEAD_DIM + 1, :] for h in range(Q_PER_KV)]
    o_t = jnp.concatenate(outs + [jnp.zeros((HEAD_DIM, bq), F32)], axis=0)
    o_ref[0, 0] = o_t.T[:, :Q_PER_KV * HEAD_DIM].astype(BF16)


def _attention(qt, k, vt, kc, vct):
    b, _, l = qt.shape
    c = kc.shape[2]
    bq = min(ATTN_BQ, l)
    bk = min(ATTN_BK, l)
    gw = Q_PER_KV * HEAD_DIM
    return pl.pallas_call(
        functools.partial(_attn_kernel, bk=bk, nchunks=l // bk),
        grid=(b, KV_HEADS, l // bq),
        in_specs=[pl.BlockSpec((1, gw, bq), lambda i, g, j: (i, g, j)),
                  pl.BlockSpec((1, 1, l, HEAD_DIM), lambda i, g, j: (i, g, 0, 0)),
                  pl.BlockSpec((1, HEAD_DIM, l), lambda i, g, j: (i, g, 0)),
                  pl.BlockSpec((1, 1, c, HEAD_DIM), lambda i, g, j: (i, g, 0, 0)),
                  pl.BlockSpec((1, HEAD_DIM, c), lambda i, g, j: (i, g, 0))],
        out_specs=pl.BlockSpec((1, 1, bq, gw), lambda i, g, j: (i, g, j, 0)),
        out_shape=jax.ShapeDtypeStruct((b, KV_HEADS, l, gw), BF16),
        scratch_shapes=[pltpu.VMEM((Q_PER_KV, 1, bq), F32),
                        pltpu.VMEM((Q_PER_KV, HEAD_DIM + ONES_ROWS, bq), F32),
                        *[pltpu.VMEM((max(bk, c), bq), F32)] * (2 * Q_PER_KV)],
        compiler_params=_cp(("arbitrary", "arbitrary", "arbitrary")),
        name="attention",
    )(qt, k, vt, kc, vct)


def _outproj_router_kernel(x_ref, f_ref, a_ref, g1_ref, wo_ref, n2g_ref, sc_ref, sh_ref,
                           wrh_ref, wrl_ref, bias_ref, xn_ref, hpa_ref, hpb_ref, te_ref, gt_ref, cnt_ref):
    tm = x_ref.shape[1]
    mix = jnp.dot(f_ref[0].astype(BF16), wo_ref[:FOURIER_WIDTH, :], preferred_element_type=F32)
    gw = Q_PER_KV * HEAD_DIM
    for g in range(KV_HEADS):
        r0 = FOURIER_WIDTH + g * gw
        mix += jnp.dot(a_ref[0, g], wo_ref[r0:r0 + gw, :], preferred_element_type=F32)
    xn = x_ref[0] + g1_ref[0] * mix
    xn_ref[0] = xn
    h2 = _modulated_norm(xn, n2g_ref[...], sc_ref[0], sh_ref[0])
    h2_hi = h2.astype(BF16)
    h2_lo = (h2 - h2_hi.astype(F32)).astype(BF16)
    hpa_ref[...], hpb_ref[...] = _pack_rows(h2)

    wrh = wrh_ref[...]
    logits = (lax.dot_general(wrh, h2_hi, _NT, preferred_element_type=F32)
              + lax.dot_general(wrh, h2_lo, _NT, preferred_element_type=F32)
              + lax.dot_general(wrl_ref[...], h2_hi, _NT, preferred_element_type=F32))
    scores = _sigmoid(logits)
    sel = scores + bias_ref[...]

    neg = jnp.float32(-jnp.inf)
    s3 = sel.reshape(N_EXPERT_GROUPS, EXPERTS_PER_GROUP, tm)
    i3 = lax.broadcasted_iota(I32, s3.shape, 1)
    m1 = jnp.max(s3, axis=1, keepdims=True)
    i1 = jnp.min(jnp.where(s3 == m1, i3, EXPERTS_PER_GROUP), axis=1, keepdims=True)
    m2 = jnp.max(jnp.where(i3 == i1, neg, s3), axis=1)
    gs = m1[:, 0, :] + m2
    gi = lax.broadcasted_iota(I32, gs.shape, 0)
    keep = jnp.zeros(gs.shape, jnp.bool_)
    for _ in range(TOPK_GROUPS):
        m = jnp.max(gs, axis=0, keepdims=True)
        idx = jnp.min(jnp.where(gs == m, gi, N_EXPERT_GROUPS), axis=0, keepdims=True)
        hit = gi == idx
        keep = keep | hit
        gs = jnp.where(hit, neg, gs)
    keep3 = jnp.broadcast_to(keep[:, None, :], s3.shape)
    selm = jnp.where(keep3, s3, neg).reshape(N_EXPERTS, tm)

    ei = lax.broadcasted_iota(I32, selm.shape, 0)
    multi = jnp.zeros(selm.shape, F32)
    idxs, gates = [], []
    for _ in range(TOP_K):
        m = jnp.max(selm, axis=0, keepdims=True)
        idx = jnp.min(jnp.where(selm == m, ei, N_EXPERTS), axis=0, keepdims=True)
        hit = ei == idx
        gates.append(jnp.sum(jnp.where(hit, scores, 0.0), axis=0, keepdims=True))
        idxs.append(idx)
        selm = jnp.where(hit, neg, selm)
        multi = multi + hit.astype(F32)
    gate = jnp.concatenate(gates, axis=0)
    gate = gate / jnp.sum(gate, axis=0, keepdims=True) * ROUTED_SCALE
    te_ref[...] = jnp.concatenate(idxs, axis=0)
    gt_ref[...] = gate
    ones = jnp.ones((SUBLANES, tm), BF16)
    cnt_ref[0] = lax.dot_general(ones, multi.astype(BF16), _NT, preferred_element_type=F32)


def _outproj_router(x, four, attn, g1, w_out, n2g, sc2, sh2, wr_hi, wr_lo, bias):
    b, l, d = x.shape
    tm = min(TOK_TILE, l)
    tpb = l // tm
    n = b * l
    const2 = lambda i, j: (0, 0)
    mod_map = lambda i, j: (i, 0, 0)
    tok_map = lambda i, j: (0, i * tpb + j)
    return pl.pallas_call(
        _outproj_router_kernel,
        grid=(b, tpb),
        in_specs=[pl.BlockSpec((1, tm, d), lambda i, j: (i, j, 0)),
                  pl.BlockSpec((1, tm, FOURIER_WIDTH), lambda i, j: (i, j, 0)),
                  pl.BlockSpec((1, KV_HEADS, tm, Q_PER_KV * HEAD_DIM), lambda i, j: (i, 0, j, 0)),
                  pl.BlockSpec((1, 1, d), mod_map),
                  pl.BlockSpec((d, d), const2),
                  pl.BlockSpec((1, d), const2),
                  pl.BlockSpec((1, 1, d), mod_map),
                  pl.BlockSpec((1, 1, d), mod_map),
                  pl.BlockSpec((N_EXPERTS, d), const2),
                  pl.BlockSpec((N_EXPERTS, d), const2),
                  pl.BlockSpec((N_EXPERTS, 1), const2)],
        out_specs=[pl.BlockSpec((1, tm, d), lambda i, j: (i, j, 0)),
                   pl.BlockSpec((tm, PACK_W), lambda i, j: (i * tpb + j, 0)),
                   pl.BlockSpec((tm, PACK_W), lambda i, j: (i * tpb + j, 0)),
                   pl.BlockSpec((TOP_K, tm), tok_map),
                   pl.BlockSpec((TOP_K, tm), tok_map),
                   pl.BlockSpec((1, SUBLANES, N_EXPERTS), lambda i, j: (i * tpb + j, 0, 0))],
        out_shape=[jax.ShapeDtypeStruct((b, l, d), F32),
                   jax.ShapeDtypeStruct((n, PACK_W), I32),
                   jax.ShapeDtypeStruct((n, PACK_W), I32),
                   jax.ShapeDtypeStruct((TOP_K, n), I32),
                   jax.ShapeDtypeStruct((TOP_K, n), F32),
                   jax.ShapeDtypeStruct((n // tm, SUBLANES, N_EXPERTS), F32)],
        compiler_params=_cp(("arbitrary", "arbitrary")),
        name="outproj_router",
    )(x, four, attn, g1, w_out, n2g, sc2, sh2, wr_hi, wr_lo, bias)


def _dest_kernel(te_ref, base_ref, tri_ref, d_ref):
    te = te_ref[...]
    tm = te.shape[1]
    ei = lax.broadcasted_iota(I32, (N_EXPERTS, tm), 0)
    hits = [ei == te[k:k + 1, :] for k in range(TOP_K)]
    multi = hits[0].astype(F32)
    for k in range(1, TOP_K):
        multi = multi + hits[k].astype(F32)
    rank = jnp.dot(multi.astype(BF16), tri_ref[...], preferred_element_type=F32)
    pos = rank + base_ref[0]
    rows = [jnp.sum(jnp.where(hits[k], pos, 0.0), axis=0, keepdims=True) for k in range(TOP_K)]
    d_ref[...] = jnp.concatenate(rows, axis=0).astype(I32)


def _dest_rows(top_e, base, tm):
    n = top_e.shape[1]
    tri = jnp.asarray(np.triu(np.ones((tm, tm), np.float32), 1)).astype(BF16)
    return pl.pallas_call(
        _dest_kernel,
        grid=(n // tm,),
        in_specs=[pl.BlockSpec((TOP_K, tm), lambda i: (0, i)),
                  pl.BlockSpec((1, N_EXPERTS, 1), lambda i: (i, 0, 0)),
                  pl.BlockSpec((tm, tm), lambda i: (0, 0))],
        out_specs=pl.BlockSpec((TOP_K, tm), lambda i: (0, i)),
        out_shape=jax.ShapeDtypeStruct((TOP_K, n), I32),
        compiler_params=_cp(("arbitrary",)),
        name="dest_rows",
    )(top_e, base, tri)


def _sc_mesh():
    return plsc.VectorSubcoreMesh(core_axis_name="c", subcore_axis_name="s",
                                  num_cores=SC_CORES, num_subcores=SC_SUBCORES)


def _sc_scatter_rows(src, dest, n_rows):
    n, w = src.shape
    kk = dest.shape[0]

    @pl.kernel(out_type=jax.ShapeDtypeStruct((n_rows, w), src.dtype), mesh=_sc_mesh(), name="sc_scatter_rows")
    def scatter(x_hbm, d_hbm, o_hbm):
        def body(x_vmem, d_vmem):
            for k in range(kk):
                pltpu.sync_copy(x_vmem, o_hbm.at[d_vmem.at[k]])

        pltpu.emit_pipeline(
            body,
            grid=(n // SC_WINDOW,),
            in_specs=[pl.BlockSpec((SC_WINDOW, w), lambda i: (i, 0)),
                      pl.BlockSpec((kk, SC_WINDOW), lambda i: (0, i))],
            out_specs=[],
            core_axis_name=("c", "s"),
            dimension_semantics=(pltpu.PARALLEL,),
        )(x_hbm, d_hbm)

    return scatter(src, dest)


def _swiglu(quarters, wgu, wdn):
    ag = None
    for j, q in enumerate(quarters):
        part = jnp.dot(q, wgu[j * PACK_W:(j + 1) * PACK_W, :], preferred_element_type=F32)
        ag = part if ag is None else ag + part
    ff = ag.shape[1] // 2
    a, g = ag[:, :ff], ag[:, ff:]
    mid = (a * _sigmoid(a) * g).astype(BF16)
    return jnp.dot(mid, wdn[...], preferred_element_type=F32)


def _expert_kernel(blk0_ref, nblk_ref, cnt_ref, xa_hbm, xb_hbm, wgu_ref, wdn_ref, ya_hbm, yb_hbm,
                   xbuf, ybuf, wgu_s, wdn_s, in_sem, out_sem, pending):
    e = pl.program_id(0)
    last = pl.num_programs(0) - 1
    rb = xbuf.shape[2]
    nb = nblk_ref[e]
    blk0 = blk0_ref[e]
    n_used = blk0_ref[last] + nblk_ref[last]
    x_hbm = (xa_hbm, xb_hbm)
    y_hbm = (ya_hbm, yb_hbm)

    def x_copy(g, t):
        slot = g % X_SLOTS
        return pltpu.make_async_copy(x_hbm[t].at[pl.ds(g * rb, rb)], xbuf.at[slot, t], in_sem.at[slot, t])

    def x_start(g):
        @pl.when(g < n_used)
        def _():
            for t in range(2):
                x_copy(g, t).start()

    def y_copy(g, slot, t):
        return pltpu.make_async_copy(ybuf.at[slot, t], y_hbm[t].at[pl.ds(g * rb, rb)], out_sem.at[slot, t])

    def drain(slot):
        @pl.when(pending[slot] == 1)
        def _():
            for t in range(2):
                y_copy(0, slot, t).wait()
            pending[slot] = 0

    @pl.when(e == 0)
    def _():
        pending[0] = 0
        pending[1] = 0
        for g in range(X_SLOTS - 1):
            x_start(g)

    @pl.when(nb > 0)
    def _():
        wgu_s[...] = wgu_ref[0].astype(BF16)
        wdn_s[...] = wdn_ref[0].astype(BF16)

        def body(j, carry):
            g = blk0 + j
            for t in range(2):
                x_copy(g, t).wait()
            x_start(g + X_SLOTS - 1)
            xs = g % X_SLOTS
            ys = g % 2
            drain(ys)
            valid = lax.broadcasted_iota(I32, (rb, PACK_W), 0) < cnt_ref[e] - j * rb
            xa = jnp.where(valid, xbuf[xs, 0], 0)
            xb = jnp.where(valid, xbuf[xs, 1], 0)
            quarters = [q.astype(BF16) for q in _unpack_rows(xa, xb)]
            ya, yb = _pack_rows(_swiglu(quarters, wgu_s, wdn_s))
            ybuf[ys, 0] = ya
            ybuf[ys, 1] = yb
            for t in range(2):
                y_copy(g, ys, t).start()
            pending[ys] = 1
            return carry

        lax.fori_loop(0, nb, body, 0)

    @pl.when(e == last)
    def _():
        drain(0)
        drain(1)
        n_total = ya_hbm.shape[0] // rb
        ybuf[0, 0] = jnp.zeros((rb, PACK_W), I32)

        def zero_copy(b, t):
            return pltpu.make_async_copy(ybuf.at[0, 0], y_hbm[t].at[pl.ds(b * rb, rb)], out_sem.at[0, t])

        def start_zero(b, carry):
            for t in range(2):
                zero_copy(b, t).start()
            return carry

        def wait_zero(b, carry):
            for t in range(2):
                zero_copy(b, t).wait()
            return carry

        lax.fori_loop(n_used, n_total, start_zero, 0)
        lax.fori_loop(n_used, n_total, wait_zero, 0)


def _experts(blk0, nblk, cnt, xa, xb, w_gu, w_down):
    n_rows = xa.shape[0]
    rb = ROW_BLOCK
    n_experts, d, ff2 = w_gu.shape
    w_map = lambda e, b0, nb, ct: (e, 0, 0)
    grid_spec = pltpu.PrefetchScalarGridSpec(
        num_scalar_prefetch=3,
        grid=(n_experts,),
        in_specs=[pl.BlockSpec(memory_space=pl.ANY),
                  pl.BlockSpec(memory_space=pl.ANY),
                  pl.BlockSpec((1, d, ff2), w_map),
                  pl.BlockSpec((1, ff2 // 2, d), w_map)],
        out_specs=[pl.BlockSpec(memory_space=pl.ANY)] * 2,
        scratch_shapes=[pltpu.VMEM((X_SLOTS, 2, rb, PACK_W), I32), pltpu.VMEM((2, 2, rb, PACK_W), I32),
                        pltpu.VMEM((d, ff2), BF16), pltpu.VMEM((ff2 // 2, d), BF16),
                        pltpu.SemaphoreType.DMA((X_SLOTS, 2)), pltpu.SemaphoreType.DMA((2, 2)),
                        pltpu.SMEM((2,), I32)],
    )
    return pl.pallas_call(
        _expert_kernel,
        grid_spec=grid_spec,
        out_shape=[jax.ShapeDtypeStruct((n_rows, PACK_W), I32)] * 2,
        compiler_params=_cp(("arbitrary",)),
        name="experts",
    )(blk0, nblk, cnt, xa, xb, w_gu, w_down)


def _sc_gather_rows(table, idx):
    m = idx.shape[1]
    w = table.shape[1]

    @pl.kernel(out_type=jax.ShapeDtypeStruct((m, w), table.dtype), mesh=_sc_mesh(), name="sc_gather_rows")
    def gather(t_hbm, i_hbm, o_hbm):
        def body(i_vmem, o_vmem):
            pltpu.sync_copy(t_hbm.at[i_vmem.at[0]], o_vmem)

        pltpu.emit_pipeline(
            body,
            grid=(m // SC_WINDOW,),
            in_specs=[pl.BlockSpec((1, SC_WINDOW), lambda i: (0, i))],
            out_specs=[pl.BlockSpec((SC_WINDOW, w), lambda i: (i, 0))],
            core_axis_name=("c", "s"),
            dimension_semantics=(pltpu.PARALLEL,),
        )(i_hbm, o_hbm)

    return gather(table, idx)


def _combine_kernel(gate_ref, xn_ref, hpa_ref, hpb_ref, g2_ref, wsgu_ref, wsdn_ref, fng_ref,
                    yga_ref, ygb_ref, o_ref):
    quarters = [q.astype(BF16) for q in _unpack_rows(hpa_ref[...], hpb_ref[...])]
    shared = _swiglu(quarters, wsgu_ref, wsdn_ref)
    gate = gate_ref[...]
    acc = None
    for k in range(TOP_K):
        gk = gate[:, k:k + 1]
        rows = [gk * q for q in _unpack_rows(yga_ref[k], ygb_ref[k])]
        acc = rows if acc is None else [a + r for a, r in zip(acc, rows)]
    y = jnp.concatenate(acc, axis=1) + shared
    xo = xn_ref[...] + g2_ref[0] * y
    ms = jnp.mean(xo * xo, axis=-1, keepdims=True)
    o_ref[...] = xo * lax.rsqrt(ms + RMS_EPS) * fng_ref[...]


def _combine(dest, gate_t, xn, hpa, hpb, g2, ws_gu, ws_dn, fng, ya, yb, tokens_per_batch):
    n, d = xn.shape
    tm = min(COMB_TILE, tokens_per_batch)
    nt = n // tm
    tpb = tokens_per_batch // tm
    const2 = lambda i: (0, 0)
    row_map = lambda i: (i, 0)
    idx = dest.reshape(1, TOP_K * n)
    yga = _sc_gather_rows(ya, idx).reshape(TOP_K, n, PACK_W)
    ygb = _sc_gather_rows(yb, idx).reshape(TOP_K, n, PACK_W)
    return pl.pallas_call(
        _combine_kernel,
        grid=(nt,),
        in_specs=[pl.BlockSpec((tm, TOP_K), row_map),
                  pl.BlockSpec((tm, d), row_map),
                  pl.BlockSpec((tm, PACK_W), row_map),
                  pl.BlockSpec((tm, PACK_W), row_map),
                  pl.BlockSpec((1, 1, d), lambda i: (i // tpb, 0, 0)),
                  pl.BlockSpec(ws_gu.shape, const2),
                  pl.BlockSpec(ws_dn.shape, const2),
                  pl.BlockSpec((1, d), const2),
                  pl.BlockSpec((TOP_K, tm, PACK_W), lambda i: (0, i, 0)),
                  pl.BlockSpec((TOP_K, tm, PACK_W), lambda i: (0, i, 0))],
        out_specs=pl.BlockSpec((tm, d), row_map),
        out_shape=jax.ShapeDtypeStruct((n, d), F32),
        compiler_params=_cp(("arbitrary",)),
        name="combine",
    )(gate_t, xn, hpa, hpb, g2, ws_gu, ws_dn, fng, yga, ygb)


def _rope_tables(l):
    rows = l // GRID_W
    row = np.repeat(np.arange(rows, dtype=np.float32), GRID_W)
    col = np.tile(np.arange(GRID_W, dtype=np.float32), rows)
    n_freq = AXIS_ROT // 2
    inv_freq = (np.float32(ROPE_THETA) ** (-np.arange(n_freq, dtype=np.float32) / n_freq)).astype(np.float32)
    ang_r = row[:, None] * inv_freq
    ang_c = col[:, None] * inv_freq
    ang = np.concatenate([ang_r, ang_r, ang_c, ang_c], axis=-1).astype(np.float64)
    cos, sin = np.cos(ang), np.sin(ang)
    lane = np.arange(HEAD_DIM) % AXIS_ROT
    first = lane < AXIS_ROT // 2
    sin_a = np.where(first[None, :], -sin, 0.0)
    sin_b = np.where(first[None, :], 0.0, sin)
    two = lambda a: jnp.asarray(np.concatenate([a, a], axis=1), dtype=F32)
    return two(cos), two(sin_a), two(sin_b)


def _layer(x, ctx, c, c_ctx, lw, moe_w, fng):
    norm1_g, w_ada, b_ada, w_in, w_fourier, q_norm_g, k_norm_g, w_out, norm2_g = lw
    w_router, router_bias, w_expert_gu, w_expert_down, w_shared_gu, w_shared_down = moe_w
    b, l, d = x.shape
    n = b * l

    assert b + 1 <= SUBLANES
    cond = jnp.concatenate([c, c_ctx[None, :], jnp.zeros((SUBLANES - b - 1, d), F32)], axis=0)
    mod = _adaln(cond, w_ada, b_ada)
    sh1, sc1, g1, sh2, sc2, g2 = [m[:b, None, :] for m in jnp.split(mod, 6, axis=-1)]
    csh1, csc1 = [m[b:b + 1, None, :] for m in jnp.split(mod, 6, axis=-1)[:2]]

    w_in_bf = w_in.astype(BF16)
    two = lambda g, s: jnp.tile(g * s, 2).reshape(1, LANES)
    gq, gk = two(q_norm_g, ATTN_SCALE * float(np.log2(np.e))), two(k_norm_g, 1.0)
    ones_bd = jnp.asarray(np.kron(np.eye(2), np.ones((HEAD_DIM, HEAD_DIM))), dtype=F32).astype(BF16)
    n1g = norm1_g.reshape(1, d)

    kc, vct = _inproj(ctx, csc1, csh1, n1g, w_in_bf[:, KV_COL0:], gq, gk, ones_bd, None, False)
    u, qt, k, vt = _inproj(x, sc1, sh1, n1g, w_in_bf, gq, gk, ones_bd, _rope_tables(l), True)

    w_bd = jnp.zeros((FOURIER_WIDTH, FOURIER_WIDTH), F32)
    for h in range(FOURIER_HEADS):
        w_bd = lax.dynamic_update_slice(w_bd, w_fourier[h], (h * HEAD_DIM, h * HEAD_DIM))
    four = _fourier(u, w_bd.astype(BF16))
    attn = _attention(qt, k, vt, kc, vct)

    wr_t = w_router.T
    wr_hi = wr_t.astype(BF16)
    wr_lo = (wr_t - wr_hi.astype(F32)).astype(BF16)
    xn, hpa, hpb, top_e, gate, cnt = _outproj_router(
        x, four, attn, g1, w_out.astype(BF16), norm2_g.reshape(1, d), sc2, sh2,
        wr_hi, wr_lo, router_bias.reshape(N_EXPERTS, 1))

    tm = min(TOK_TILE, l)
    rb = ROW_BLOCK
    counts = cnt[:, 0, :].astype(I32)
    total = jnp.sum(counts, axis=0)
    padded = (total + rb - 1) // rb * rb
    pad_end = jnp.cumsum(padded)
    pad_start = pad_end - padded
    base = pad_start[None, :] + jnp.cumsum(counts, axis=0) - counts
    n_blocks = n * TOP_K // rb + N_EXPERTS

    dest = _dest_rows(top_e, base.astype(F32)[:, :, None], tm)
    xa = _sc_scatter_rows(hpa, dest, n_blocks * rb)
    xb = _sc_scatter_rows(hpb, dest, n_blocks * rb)
    ya, yb = _experts(pad_start // rb, padded // rb, total, xa, xb, w_expert_gu, w_expert_down)
    return _combine(dest, gate.T, xn.reshape(n, d), hpa, hpb, g2, w_shared_gu.astype(BF16),
                    w_shared_down.astype(BF16), fng, ya, yb, l)


def kernel(x, c, ctx, c_ctx, norm1_g, w_ada, b_ada, w_in, w_fourier, q_norm_g, k_norm_g, w_out, norm2_g,
           w_router, router_bias, w_expert_gu, w_expert_down, w_shared_gu, w_shared_down, final_norm_g):
    depth = norm1_g.shape[0]
    assert depth == 1, "context update between layers is not implemented"
    b, l, d = x.shape
    lw = (norm1_g[0], w_ada[0], b_ada[0], w_in[0], w_fourier[0], q_norm_g[0], k_norm_g[0], w_out[0], norm2_g[0])
    moe_w = (w_router[0], router_bias[0], w_expert_gu[0], w_expert_down[0], w_shared_gu[0], w_shared_down[0])
    out = _layer(x, ctx, c, c_ctx, lw, moe_w, final_norm_g.reshape(1, d))
    return out.reshape(b, l, d)
```

```python
import functools

import numpy as np
import jax
import jax.numpy as jnp
from jax import lax
from jax.experimental import pallas as pl
from jax.experimental.pallas import tpu as pltpu
from jax.experimental.pallas import tpu_sc as plsc

F32 = jnp.float32
BF16 = jnp.bfloat16
I32 = jnp.int32

D_MODEL = 1024
GRID_W = 64
HEAD_DIM = 64
FOURIER_HEADS = 4
FOURIER_WIDTH = 256
ATTN_HEADS = 12
KV_HEADS = 4
Q_PER_KV = 3
ATTN_WIDTH = 768
KV_WIDTH = 256
KV_COL0 = 1024
IN_WIDTH = 1536
ATTN_SCALE = HEAD_DIM ** -0.5
ROPE_THETA = 10000.0
AXIS_ROT = HEAD_DIM // 2
N_EXPERTS = 256
TOP_K = 8
N_EXPERT_GROUPS = 8
TOPK_GROUPS = 4
EXPERTS_PER_GROUP = 32
EXPERT_FF = 256
ROUTED_SCALE = 2.5
RMS_EPS = 1e-6

LANES = 128
SUBLANES = 8
PACK_W = D_MODEL // 4
DFT_L1 = 64
VMEM_LIMIT = 48 * 1024 * 1024

TOK_TILE = 512
NORM_W = 128
FOURIER_KB = 16
ATTN_BQ = 256
ATTN_BK = 1024
QK_SPLIT = 2
ONES_ROWS = 16
ROW_BLOCK = 512
X_SLOTS = 4
COMB_TILE = 256
SC_WINDOW = 128
SC_CORES = 2
SC_SUBCORES = 16

_HI = lax.Precision.HIGHEST
_NT = (((1,), (1,)), ((), ()))


def _cp(sem, vmem=VMEM_LIMIT):
    return pltpu.CompilerParams(dimension_semantics=sem, vmem_limit_bytes=vmem)


def _sigmoid(v):
    return 1.0 / (1.0 + jnp.exp(-v))


def _pack_rows(v):
    w = v.shape[1] // 4

    def pack(lo, hi):
        lo = lax.bitcast_convert_type(lo.astype(BF16).astype(F32), I32)
        hi = lax.bitcast_convert_type(hi.astype(BF16).astype(F32), I32)
        return lax.shift_right_logical(lo, 16) | (hi & jnp.int32(-65536))

    return pack(v[:, :w], v[:, 2 * w:3 * w]), pack(v[:, w:2 * w], v[:, 3 * w:])


def _unpack_rows(a, b):
    lo = lambda t: lax.bitcast_convert_type(lax.shift_left(t, 16), F32)
    hi = lambda t: lax.bitcast_convert_type(t & jnp.int32(-65536), F32)
    return [lo(a), lo(b), hi(a), hi(b)]


def _adaln_kernel(c_ref, w_ref, b_ref, o_ref):
    c = c_ref[...]
    s = c * _sigmoid(c)
    o_ref[...] = jnp.dot(s, w_ref[...], precision=_HI, preferred_element_type=F32) + b_ref[...]


def _adaln(cond, w_ada, b_ada):
    rows, d = cond.shape
    n = w_ada.shape[1]
    tn = 1536
    return pl.pallas_call(
        _adaln_kernel,
        grid=(n // tn,),
        in_specs=[pl.BlockSpec((rows, d), lambda j: (0, 0)),
                  pl.BlockSpec((d, tn), lambda j: (0, j)),
                  pl.BlockSpec((1, tn), lambda j: (0, j))],
        out_specs=pl.BlockSpec((rows, tn), lambda j: (0, j)),
        out_shape=jax.ShapeDtypeStruct((rows, n), F32),
        compiler_params=_cp(("arbitrary",)),
        name="adaln",
    )(cond, w_ada, b_ada.reshape(1, n))


def _modulated_norm(x, g, sc, sh):
    ms = jnp.mean(x * x, axis=-1, keepdims=True)
    return x * lax.rsqrt(ms + RMS_EPS) * g * (1.0 + sc) + sh


def _head_rms_inv(chunk, ones_bd):
    sq = chunk * chunk
    hi = sq.astype(BF16)
    lo = (sq - hi.astype(F32)).astype(BF16)
    ss = (jnp.dot(hi, ones_bd, preferred_element_type=F32)
          + jnp.dot(lo, ones_bd, preferred_element_type=F32))
    return lax.rsqrt(ss * (1.0 / HEAD_DIM) + RMS_EPS)


def _rope(t, cos, sin_a, sin_b):
    return (t * cos + pltpu.roll(t, LANES - AXIS_ROT // 2, 1) * sin_a
            + pltpu.roll(t, AXIS_ROT // 2, 1) * sin_b)


def _inproj_kernel(*refs, rope, with_uq):
    if rope:
        (x_ref, sc_ref, sh_ref, g_ref, w_ref, gq_ref, gk_ref, ones_ref,
         cos_ref, sa_ref, sb_ref) = refs[:11]
        outs = refs[11:]
    else:
        x_ref, sc_ref, sh_ref, g_ref, w_ref, gq_ref, gk_ref, ones_ref = refs[:8]
        outs = refs[8:]
    if with_uq:
        u_ref, qt_ref, k_ref, vt_ref = outs
    else:
        k_ref, vt_ref = outs

    h = _modulated_norm(x_ref[0], g_ref[...], sc_ref[0], sh_ref[0])
    z = jnp.dot(h.astype(BF16), w_ref[...], preferred_element_type=F32)
    ones_bd = ones_ref[...]

    def normed(col0, width, gain):
        chunks = []
        for j in range(width // NORM_W):
            zz = z[:, col0 + j * NORM_W: col0 + (j + 1) * NORM_W]
            inv = _head_rms_inv(zz, ones_bd)
            for c in range(NORM_W // LANES):
                t = zz[:, c * LANES:(c + 1) * LANES] * inv[:, c * LANES:(c + 1) * LANES] * gain
                if rope:
                    t = _rope(t, cos_ref[...], sa_ref[...], sb_ref[...])
                chunks.append(t)
        return chunks

    col = 0
    if with_uq:
        u_ref[0] = z[:, :FOURIER_WIDTH].astype(BF16)
        col = FOURIER_WIDTH
        qt_ref[0] = jnp.concatenate(normed(col, ATTN_WIDTH, gq_ref[...]), axis=1).T.astype(BF16)
        col += ATTN_WIDTH
    for j, t in enumerate(normed(col, KV_WIDTH, gk_ref[...])):
        k_ref[0, 2 * j] = t[:, :HEAD_DIM].astype(BF16)
        k_ref[0, 2 * j + 1] = t[:, HEAD_DIM:].astype(BF16)
    col += KV_WIDTH
    vt_ref[0] = z[:, col:col + KV_WIDTH].T.astype(BF16)


def _inproj(x, sc, sh, g, w, gq, gk, ones_bd, rope_tabs, with_uq):
    b, t, d = x.shape
    tm = min(TOK_TILE, t)
    wcols = w.shape[1]
    bm = sc.shape[0]
    mod_map = (lambda i, j: (i, 0, 0)) if bm == b else (lambda i, j: (0, 0, 0))
    const2 = lambda i, j: (0, 0)
    in_specs = [pl.BlockSpec((1, tm, d), lambda i, j: (i, j, 0)),
                pl.BlockSpec((1, 1, d), mod_map),
                pl.BlockSpec((1, 1, d), mod_map),
                pl.BlockSpec((1, d), const2),
                pl.BlockSpec((d, wcols), const2),
                pl.BlockSpec((1, LANES), const2),
                pl.BlockSpec((1, LANES), const2),
                pl.BlockSpec((NORM_W, NORM_W), const2)]
    args = [x, sc, sh, g, w, gq, gk, ones_bd]
    rope = rope_tabs is not None
    if rope:
        in_specs += [pl.BlockSpec((tm, LANES), lambda i, j: (j, 0))] * 3
        args += list(rope_tabs)
    out_specs, out_shape = [], []
    if with_uq:
        out_specs += [pl.BlockSpec((1, tm, FOURIER_WIDTH), lambda i, j: (i, j, 0)),
                      pl.BlockSpec((1, ATTN_WIDTH, tm), lambda i, j: (i, 0, j))]
        out_shape += [jax.ShapeDtypeStruct((b, t, FOURIER_WIDTH), BF16),
                      jax.ShapeDtypeStruct((b, ATTN_WIDTH, t), BF16)]
    out_specs += [pl.BlockSpec((1, KV_HEADS, tm, HEAD_DIM), lambda i, j: (i, 0, j, 0)),
                  pl.BlockSpec((1, KV_WIDTH, tm), lambda i, j: (i, 0, j))]
    out_shape += [jax.ShapeDtypeStruct((b, KV_HEADS, t, HEAD_DIM), BF16),
                  jax.ShapeDtypeStruct((b, KV_WIDTH, t), BF16)]
    return pl.pallas_call(
        functools.partial(_inproj_kernel, rope=rope, with_uq=with_uq),
        grid=(b, t // tm),
        in_specs=in_specs,
        out_specs=out_specs,
        out_shape=out_shape,
        compiler_params=_cp(("arbitrary", "arbitrary")),
        name="inproj_latent" if with_uq else "inproj_ctx",
    )(*args)


def _fourier_a_kernel(u_ref, c_ref, s_ref, yr_ref, yi_ref):
    u = u_ref[0]
    yr_ref[0] = jnp.dot(c_ref[...], u, preferred_element_type=F32).astype(BF16)
    yi_ref[0] = jnp.dot(s_ref[...], u, preferred_element_type=F32).astype(BF16)


def _fourier_b_kernel(yr_ref, yi_ref, m_ref, c_ref, s_ref, w_ref, o_ref, *, kb):
    y = jnp.concatenate([yr_ref[0], yi_ref[0]], axis=1)
    x = jnp.einsum("kab,kbc->kac", m_ref[...], y, preferred_element_type=F32)
    xr = x[:, :DFT_L1].reshape(kb * DFT_L1, FOURIER_WIDTH).astype(BF16)
    xi = x[:, DFT_L1:].reshape(kb * DFT_L1, FOURIER_WIDTH).astype(BF16)
    spec = (jnp.dot(xr, c_ref[...], preferred_element_type=F32)
            + jnp.dot(xi, s_ref[...], preferred_element_type=F32))
    o = jnp.dot(spec.astype(BF16), w_ref[...], preferred_element_type=F32)
    for j in range(kb):
        o_ref[0, :, j, :] = o[j * DFT_L1:(j + 1) * DFT_L1]


def _dft_tables(l):
    l2 = l // DFT_L1
    k2 = np.arange(l2)
    ang2 = 2.0 * np.pi * ((k2[:, None] * k2[None, :]) % l2) / l2
    c2 = np.cos(ang2)
    s2n = -np.sin(ang2)
    n1 = np.arange(DFT_L1)
    k = l2 * n1[None, :, None] + k2[:, None, None]
    ang = 2.0 * np.pi * ((k * n1[None, None, :]) % l) / l
    mr, mi = np.cos(ang), -np.sin(ang)
    m = np.concatenate([np.concatenate([mr, -mi], axis=2),
                        np.concatenate([mi, mr], axis=2)], axis=1)
    d = np.arange(HEAD_DIM)
    angc = 2.0 * np.pi * ((d[:, None] * d[None, :]) % HEAD_DIM) / HEAD_DIM
    scale = 1.0 / np.sqrt(float(l) * HEAD_DIM)
    eye = np.eye(FOURIER_HEADS)
    cbd = np.kron(eye, np.cos(angc) * scale)
    sbd = np.kron(eye, np.sin(angc) * scale)
    as_bf = lambda a: jnp.asarray(a, dtype=F32).astype(BF16)
    return as_bf(c2), as_bf(s2n), as_bf(m), as_bf(cbd), as_bf(sbd)


def _fourier(u, w_bd):
    b, l, fw = u.shape
    l2 = l // DFT_L1
    ncol = DFT_L1 * fw
    tn = min(4096, ncol)
    kb = min(FOURIER_KB, l2)
    c2, s2n, m, cbd, sbd = _dft_tables(l)
    yr, yi = pl.pallas_call(
        _fourier_a_kernel,
        grid=(b, ncol // tn),
        in_specs=[pl.BlockSpec((1, l2, tn), lambda i, j: (i, 0, j)),
                  pl.BlockSpec((l2, l2), lambda i, j: (0, 0)),
                  pl.BlockSpec((l2, l2), lambda i, j: (0, 0))],
        out_specs=[pl.BlockSpec((1, l2, tn), lambda i, j: (i, 0, j))] * 2,
        out_shape=[jax.ShapeDtypeStruct((b, l2, ncol), BF16)] * 2,
        compiler_params=_cp(("arbitrary", "arbitrary")),
        name="fourier_a",
    )(u.reshape(b, l2, ncol), c2, s2n)
    yr = yr.reshape(b, l2, DFT_L1, fw)
    yi = yi.reshape(b, l2, DFT_L1, fw)
    out = pl.pallas_call(
        functools.partial(_fourier_b_kernel, kb=kb),
        grid=(b, l2 // kb),
        in_specs=[pl.BlockSpec((1, kb, DFT_L1, fw), lambda i, j: (i, j, 0, 0)),
                  pl.BlockSpec((1, kb, DFT_L1, fw), lambda i, j: (i, j, 0, 0)),
                  pl.BlockSpec((kb, 2 * DFT_L1, 2 * DFT_L1), lambda i, j: (j, 0, 0)),
                  pl.BlockSpec((fw, fw), lambda i, j: (0, 0)),
                  pl.BlockSpec((fw, fw), lambda i, j: (0, 0)),
                  pl.BlockSpec((fw, fw), lambda i, j: (0, 0))],
        out_specs=pl.BlockSpec((1, DFT_L1, kb, fw), lambda i, j: (i, 0, j, 0)),
        out_shape=jax.ShapeDtypeStruct((b, DFT_L1, l2, fw), F32),
        compiler_params=_cp(("arbitrary", "arbitrary")),
        name="fourier_b",
    )(yr, yi, m, cbd, sbd, w_bd)
    return out.reshape(b, l, fw)


def _attn_kernel(qt_ref, k_ref, vt_ref, kc_ref, vct_ref, o_ref, m_scr, acc_scr, *s_scrs, bk, nchunks):
    bq = qt_ref.shape[2]
    m_scr[...] = jnp.full(m_scr.shape, -jnp.inf, F32)
    acc_scr[...] = jnp.zeros(acc_scr.shape, F32)

    def scores(k, buf, h):
        nk = k.shape[0]
        qt = qt_ref[0, h * HEAD_DIM:(h + 1) * HEAD_DIM, :]
        part = nk // QK_SPLIT
        for j in range(QK_SPLIT):
            s_scrs[buf][j * part:(j + 1) * part, :] = jnp.dot(k[j * part:(j + 1) * part], qt,
                                                              preferred_element_type=F32)

    def softmax_pv(buf, h, vt):
        nk = vt.shape[1]
        s = s_scrs[buf][:nk, :]
        m_old = m_scr[h]
        m_new = jnp.maximum(m_old, jnp.max(s, axis=0, keepdims=True))
        alpha = jnp.exp2(m_old - m_new)
        p = jnp.exp2((s - m_new).astype(BF16))
        vt_ones = jnp.concatenate([vt, jnp.ones((ONES_ROWS, nk), BF16)], axis=0)
        acc_scr[h] = alpha * acc_scr[h] + jnp.dot(vt_ones, p, preferred_element_type=F32)
        m_scr[h] = m_new

    def keys(c):
        if isinstance(c, int) and c == nchunks:
            return kc_ref[0, 0]
        return k_ref[0, 0, pl.ds(pl.multiple_of(c * bk, bk), bk), :]

    def values(c):
        if isinstance(c, int) and c == nchunks:
            return vct_ref[0]
        return vt_ref[0, :, pl.ds(pl.multiple_of(c * bk, bk), bk)]

    nbuf = len(s_scrs)
    pair = 1
    ahead = nbuf - pair
    n_units = Q_PER_KV * (nchunks + 1)

    def group(us, chunk0=0):
        for u in us:
            ua = u + ahead
            if not (isinstance(chunk0, int) and ua >= n_units):
                scores(keys(chunk0 + ua // Q_PER_KV), ua % nbuf, ua % Q_PER_KV)
        for u in us:
            softmax_pv(u % nbuf, u % Q_PER_KV, values(chunk0 + u // Q_PER_KV))

    for u in range(min(ahead, n_units)):
        scores(keys(u // Q_PER_KV), u % nbuf, u % Q_PER_KV)

    chunks_per_iter = nbuf // Q_PER_KV
    n_iter = max(0, (Q_PER_KV * nchunks - ahead) // nbuf)

    def body(j, carry):
        for t in range(0, nbuf, pair):
            group(range(t, t + pair), j * chunks_per_iter)
        return carry

    lax.fori_loop(0, n_iter, body, 0)
    for t in range(n_iter * nbuf, n_units, pair):
        group(range(t, min(t + pair, n_units)))
    outs = [acc_scr[h, :HEAD_DIM, :] / acc_scr[h, HEAD_DIM:HEAD_DIM + 1, :] for h in range(Q_PER_KV)]
    o_t = jnp.concatenate(outs + [jnp.zeros((HEAD_DIM, bq), F32)], axis=0)
    o_ref[0, 0] = o_t.T[:, :Q_PER_KV * HEAD_DIM].astype(BF16)


def _attention(qt, k, vt, kc, vct):
    b, _, l = qt.shape
    c = kc.shape[2]
    bq = min(ATTN_BQ, l)
    bk = min(ATTN_BK, l)
    gw = Q_PER_KV * HEAD_DIM
    return pl.pallas_call(
        functools.partial(_attn_kernel, bk=bk, nchunks=l // bk),
        grid=(b, KV_HEADS, l // bq),
        in_specs=[pl.BlockSpec((1, gw, bq), lambda i, g, j: (i, g, j)),
                  pl.BlockSpec((1, 1, l, HEAD_DIM), lambda i, g, j: (i, g, 0, 0)),
                  pl.BlockSpec((1, HEAD_DIM, l), lambda i, g, j: (i, g, 0)),
                  pl.BlockSpec((1, 1, c, HEAD_DIM), lambda i, g, j: (i, g, 0, 0)),
                  pl.BlockSpec((1, HEAD_DIM, c), lambda i, g, j: (i, g, 0))],
        out_specs=pl.BlockSpec((1, 1, bq, gw), lambda i, g, j: (i, g, j, 0)),
        out_shape=jax.ShapeDtypeStruct((b, KV_HEADS, l, gw), BF16),
        scratch_shapes=[pltpu.VMEM((Q_PER_KV, 1, bq), F32),
                        pltpu.VMEM((Q_PER_KV, HEAD_DIM + ONES_ROWS, bq), F32),
                        *[pltpu.VMEM((max(bk, c), bq), F32)] * (2 * Q_PER_KV)],
        compiler_params=_cp(("arbitrary", "arbitrary", "arbitrary")),
        name="attention",
    )(qt, k, vt, kc, vct)


def _outproj_router_kernel(x_ref, f_ref, a_ref, g1_ref, wo_ref, n2g_ref, sc_ref, sh_ref,
                           wrh_ref, wrl_ref, bias_ref, xn_ref, hpa_ref, hpb_ref, te_ref, gt_ref, cnt_ref):
    tm = x_ref.shape[1]
    mix = jnp.dot(f_ref[0].astype(BF16), wo_ref[:FOURIER_WIDTH, :], preferred_element_type=F32)
    gw = Q_PER_KV * HEAD_DIM
    for g in range(KV_HEADS):
        r0 = FOURIER_WIDTH + g * gw
        mix += jnp.dot(a_ref[0, g], wo_ref[r0:r0 + gw, :], preferred_element_type=F32)
    xn = x_ref[0] + g1_ref[0] * mix
    xn_ref[0] = xn
    h2 = _modulated_norm(xn, n2g_ref[...], sc_ref[0], sh_ref[0])
    h2_hi = h2.astype(BF16)
    h2_lo = (h2 - h2_hi.astype(F32)).astype(BF16)
    hpa_ref[...], hpb_ref[...] = _pack_rows(h2)

    wrh = wrh_ref[...]
    logits = (lax.dot_general(wrh, h2_hi, _NT, preferred_element_type=F32)
              + lax.dot_general(wrh, h2_lo, _NT, preferred_element_type=F32)
              + lax.dot_general(wrl_ref[...], h2_hi, _NT, preferred_element_type=F32))
    scores = _sigmoid(logits)
    sel = scores + bias_ref[...]

    neg = jnp.float32(-jnp.inf)
    s3 = sel.reshape(N_EXPERT_GROUPS, EXPERTS_PER_GROUP, tm)
    i3 = lax.broadcasted_iota(I32, s3.shape, 1)
    m1 = jnp.max(s3, axis=1, keepdims=True)
    i1 = jnp.min(jnp.where(s3 == m1, i3, EXPERTS_PER_GROUP), axis=1, keepdims=True)
    m2 = jnp.max(jnp.where(i3 == i1, neg, s3), axis=1)
    gs = m1[:, 0, :] + m2
    gi = lax.broadcasted_iota(I32, gs.shape, 0)
    keep = jnp.zeros(gs.shape, jnp.bool_)
    for _ in range(TOPK_GROUPS):
        m = jnp.max(gs, axis=0, keepdims=True)
        idx = jnp.min(jnp.where(gs == m, gi, N_EXPERT_GROUPS), axis=0, keepdims=True)
        hit = gi == idx
        keep = keep | hit
        gs = jnp.where(hit, neg, gs)
    keep3 = jnp.broadcast_to(keep[:, None, :], s3.shape)
    selm = jnp.where(keep3, s3, neg).reshape(N_EXPERTS, tm)

    ei = lax.broadcasted_iota(I32, selm.shape, 0)
    eligible = selm
    idxs, gates = [], []
    for _ in range(TOP_K):
        m = jnp.max(selm, axis=0, keepdims=True)
        idx = jnp.min(jnp.where(selm == m, ei, N_EXPERTS), axis=0, keepdims=True)
        hit = ei == idx
        gates.append(jnp.sum(jnp.where(hit, scores, 0.0), axis=0, keepdims=True))
        idxs.append(idx)
        selm = jnp.where(hit, neg, selm)
    multi = (eligible != selm).astype(F32)
    gate = jnp.concatenate(gates, axis=0)
    gate = gate / jnp.sum(gate, axis=0, keepdims=True) * ROUTED_SCALE
    te_ref[...] = jnp.concatenate(idxs, axis=0)
    gt_ref[...] = gate
    ones = jnp.ones((SUBLANES, tm), BF16)
    cnt_ref[0] = lax.dot_general(ones, multi.astype(BF16), _NT, preferred_element_type=F32)


def _outproj_router(x, four, attn, g1, w_out, n2g, sc2, sh2, wr_hi, wr_lo, bias):
    b, l, d = x.shape
    tm = min(TOK_TILE, l)
    tpb = l // tm
    n = b * l
    const2 = lambda i, j: (0, 0)
    mod_map = lambda i, j: (i, 0, 0)
    tok_map = lambda i, j: (0, i * tpb + j)
    return pl.pallas_call(
        _outproj_router_kernel,
        grid=(b, tpb),
        in_specs=[pl.BlockSpec((1, tm, d), lambda i, j: (i, j, 0)),
                  pl.BlockSpec((1, tm, FOURIER_WIDTH), lambda i, j: (i, j, 0)),
                  pl.BlockSpec((1, KV_HEADS, tm, Q_PER_KV * HEAD_DIM), lambda i, j: (i, 0, j, 0)),
                  pl.BlockSpec((1, 1, d), mod_map),
                  pl.BlockSpec((d, d), const2),
                  pl.BlockSpec((1, d), const2),
                  pl.BlockSpec((1, 1, d), mod_map),
                  pl.BlockSpec((1, 1, d), mod_map),
                  pl.BlockSpec((N_EXPERTS, d), const2),
                  pl.BlockSpec((N_EXPERTS, d), const2),
                  pl.BlockSpec((N_EXPERTS, 1), const2)],
        out_specs=[pl.BlockSpec((1, tm, d), lambda i, j: (i, j, 0)),
                   pl.BlockSpec((tm, PACK_W), lambda i, j: (i * tpb + j, 0)),
                   pl.BlockSpec((tm, PACK_W), lambda i, j: (i * tpb + j, 0)),
                   pl.BlockSpec((TOP_K, tm), tok_map),
                   pl.BlockSpec((TOP_K, tm), tok_map),
                   pl.BlockSpec((1, SUBLANES, N_EXPERTS), lambda i, j: (i * tpb + j, 0, 0))],
        out_shape=[jax.ShapeDtypeStruct((b, l, d), F32),
                   jax.ShapeDtypeStruct((n, PACK_W), I32),
                   jax.ShapeDtypeStruct((n, PACK_W), I32),
                   jax.ShapeDtypeStruct((TOP_K, n), I32),
                   jax.ShapeDtypeStruct((TOP_K, n), F32),
                   jax.ShapeDtypeStruct((n // tm, SUBLANES, N_EXPERTS), F32)],
        compiler_params=_cp(("arbitrary", "arbitrary")),
        name="outproj_router",
    )(x, four, attn, g1, w_out, n2g, sc2, sh2, wr_hi, wr_lo, bias)


def _dest_kernel(te_ref, base_ref, tri_ref, d_ref):
    te = te_ref[...]
    tm = te.shape[1]
    ei = lax.broadcasted_iota(I32, (N_EXPERTS, tm), 0)
    hits = [ei == te[k:k + 1, :] for k in range(TOP_K)]
    multi = hits[0].astype(F32)
    for k in range(1, TOP_K):
        multi = multi + hits[k].astype(F32)
    rank = jnp.dot(multi.astype(BF16), tri_ref[...], preferred_element_type=F32)
    pos = rank + base_ref[0]
    rows = [jnp.sum(jnp.where(hits[k], pos, 0.0), axis=0, keepdims=True) for k in range(TOP_K)]
    d_ref[...] = jnp.concatenate(rows, axis=0).astype(I32)


def _dest_rows(top_e, base, tm):
    n = top_e.shape[1]
    tri = jnp.asarray(np.triu(np.ones((tm, tm), np.float32), 1)).astype(BF16)
    return pl.pallas_call(
        _dest_kernel,
        grid=(n // tm,),
        in_specs=[pl.BlockSpec((TOP_K, tm), lambda i: (0, i)),
                  pl.BlockSpec((1, N_EXPERTS, 1), lambda i: (i, 0, 0)),
                  pl.BlockSpec((tm, tm), lambda i: (0, 0))],
        out_specs=pl.BlockSpec((TOP_K, tm), lambda i: (0, i)),
        out_shape=jax.ShapeDtypeStruct((TOP_K, n), I32),
        compiler_params=_cp(("arbitrary",)),
        name="dest_rows",
    )(top_e, base, tri)


def _sc_mesh():
    return plsc.VectorSubcoreMesh(core_axis_name="c", subcore_axis_name="s",
                                  num_cores=SC_CORES, num_subcores=SC_SUBCORES)


def _sc_scatter_rows(src, dest, n_rows):
    n, w = src.shape
    kk = dest.shape[0]

    @pl.kernel(out_type=jax.ShapeDtypeStruct((n_rows, w), src.dtype), mesh=_sc_mesh(), name="sc_scatter_rows")
    def scatter(x_hbm, d_hbm, o_hbm):
        def body(x_vmem, d_vmem):
            for k in range(kk):
                pltpu.sync_copy(x_vmem, o_hbm.at[d_vmem.at[k]])

        pltpu.emit_pipeline(
            body,
            grid=(n // SC_WINDOW,),
            in_specs=[pl.BlockSpec((SC_WINDOW, w), lambda i: (i, 0)),
                      pl.BlockSpec((kk, SC_WINDOW), lambda i: (0, i))],
            out_specs=[],
            core_axis_name=("c", "s"),
            dimension_semantics=(pltpu.PARALLEL,),
        )(x_hbm, d_hbm)

    return scatter(src, dest)


def _swiglu(quarters, wgu, wdn):
    ag = None
    for j, q in enumerate(quarters):
        part = jnp.dot(q, wgu[j * PACK_W:(j + 1) * PACK_W, :], preferred_element_type=F32)
        ag = part if ag is None else ag + part
    ff = ag.shape[1] // 2
    a, g = ag[:, :ff], ag[:, ff:]
    mid = (a * _sigmoid(a) * g).astype(BF16)
    return jnp.dot(mid, wdn[...], preferred_element_type=F32)


def _expert_kernel(blk0_ref, nblk_ref, cnt_ref, xa_hbm, xb_hbm, wgu_ref, wdn_ref, ya_hbm, yb_hbm,
                   xbuf, ybuf, wgu_s, wdn_s, in_sem, out_sem, pending):
    e = pl.program_id(0)
    last = pl.num_programs(0) - 1
    rb = xbuf.shape[2]
    nb = nblk_ref[e]
    blk0 = blk0_ref[e]
    n_used = blk0_ref[last] + nblk_ref[last]
    x_hbm = (xa_hbm, xb_hbm)
    y_hbm = (ya_hbm, yb_hbm)

    def x_copy(g, t):
        slot = g % X_SLOTS
        return pltpu.make_async_copy(x_hbm[t].at[pl.ds(g * rb, rb)], xbuf.at[slot, t], in_sem.at[slot, t])

    def x_start(g):
        @pl.when(g < n_used)
        def _():
            for t in range(2):
                x_copy(g, t).start()

    def y_copy(g, slot, t):
        return pltpu.make_async_copy(ybuf.at[slot, t], y_hbm[t].at[pl.ds(g * rb, rb)], out_sem.at[slot, t])

    def drain(slot):
        @pl.when(pending[slot] == 1)
        def _():
            for t in range(2):
                y_copy(0, slot, t).wait()
            pending[slot] = 0

    @pl.when(e == 0)
    def _():
        pending[0] = 0
        pending[1] = 0
        for g in range(X_SLOTS - 1):
            x_start(g)

    @pl.when(nb > 0)
    def _():
        wgu_s[...] = wgu_ref[0].astype(BF16)
        wdn_s[...] = wdn_ref[0].astype(BF16)

        def body(j, carry):
            g = blk0 + j
            for t in range(2):
                x_copy(g, t).wait()
            x_start(g + X_SLOTS - 1)
            xs = g % X_SLOTS
            ys = g % 2
            drain(ys)
            valid = lax.broadcasted_iota(I32, (rb, PACK_W), 0) < cnt_ref[e] - j * rb
            xa = jnp.where(valid, xbuf[xs, 0], 0)
            xb = jnp.where(valid, xbuf[xs, 1], 0)
            quarters = [q.astype(BF16) for q in _unpack_rows(xa, xb)]
            ya, yb = _pack_rows(_swiglu(quarters, wgu_s, wdn_s))
            ybuf[ys, 0] = ya
            ybuf[ys, 1] = yb
            for t in range(2):
                y_copy(g, ys, t).start()
            pending[ys] = 1
            return carry

        lax.fori_loop(0, nb, body, 0)

    @pl.when(e == last)
    def _():
        drain(0)
        drain(1)
        n_total = ya_hbm.shape[0] // rb
        ybuf[0, 0] = jnp.zeros((rb, PACK_W), I32)

        def zero_copy(b, t):
            return pltpu.make_async_copy(ybuf.at[0, 0], y_hbm[t].at[pl.ds(b * rb, rb)], out_sem.at[0, t])

        def start_zero(b, carry):
            for t in range(2):
                zero_copy(b, t).start()
            return carry

        def wait_zero(b, carry):
            for t in range(2):
                zero_copy(b, t).wait()
            return carry

        lax.fori_loop(n_used, n_total, start_zero, 0)
        lax.fori_loop(n_used, n_total, wait_zero, 0)


def _experts(blk0, nblk, cnt, xa, xb, w_gu, w_down):
    n_rows = xa.shape[0]
    rb = ROW_BLOCK
    n_experts, d, ff2 = w_gu.shape
    w_map = lambda e, b0, nb, ct: (e, 0, 0)
    grid_spec = pltpu.PrefetchScalarGridSpec(
        num_scalar_prefetch=3,
        grid=(n_experts,),
        in_specs=[pl.BlockSpec(memory_space=pl.ANY),
                  pl.BlockSpec(memory_space=pl.ANY),
                  pl.BlockSpec((1, d, ff2), w_map),
                  pl.BlockSpec((1, ff2 // 2, d), w_map)],
        out_specs=[pl.BlockSpec(memory_space=pl.ANY)] * 2,
        scratch_shapes=[pltpu.VMEM((X_SLOTS, 2, rb, PACK_W), I32), pltpu.VMEM((2, 2, rb, PACK_W), I32),
                        pltpu.VMEM((d, ff2), BF16), pltpu.VMEM((ff2 // 2, d), BF16),
                        pltpu.SemaphoreType.DMA((X_SLOTS, 2)), pltpu.SemaphoreType.DMA((2, 2)),
                        pltpu.SMEM((2,), I32)],
    )
    return pl.pallas_call(
        _expert_kernel,
        grid_spec=grid_spec,
        out_shape=[jax.ShapeDtypeStruct((n_rows, PACK_W), I32)] * 2,
        compiler_params=_cp(("arbitrary",)),
        name="experts",
    )(blk0, nblk, cnt, xa, xb, w_gu, w_down)


def _sc_gather_rows(table, idx):
    m = idx.shape[1]
    w = table.shape[1]

    @pl.kernel(out_type=jax.ShapeDtypeStruct((m, w), table.dtype), mesh=_sc_mesh(), name="sc_gather_rows")
    def gather(t_hbm, i_hbm, o_hbm):
        def body(i_vmem, o_vmem):
            pltpu.sync_copy(t_hbm.at[i_vmem.at[0]], o_vmem)

        pltpu.emit_pipeline(
            body,
            grid=(m // SC_WINDOW,),
            in_specs=[pl.BlockSpec((1, SC_WINDOW), lambda i: (0, i))],
            out_specs=[pl.BlockSpec((SC_WINDOW, w), lambda i: (i, 0))],
            core_axis_name=("c", "s"),
            dimension_semantics=(pltpu.PARALLEL,),
        )(i_hbm, o_hbm)

    return gather(table, idx)


def _combine_kernel(gate_ref, xn_ref, hpa_ref, hpb_ref, g2_ref, wsgu_ref, wsdn_ref, fng_ref,
                    yga_ref, ygb_ref, o_ref):
    quarters = [q.astype(BF16) for q in _unpack_rows(hpa_ref[...], hpb_ref[...])]
    shared = _swiglu(quarters, wsgu_ref, wsdn_ref)
    gate = gate_ref[...]
    acc = None
    for k in range(TOP_K):
        gk = gate[:, k:k + 1]
        rows = [gk * q for q in _unpack_rows(yga_ref[k], ygb_ref[k])]
        acc = rows if acc is None else [a + r for a, r in zip(acc, rows)]
    y = jnp.concatenate(acc, axis=1) + shared
    xo = xn_ref[...] + g2_ref[0] * y
    ms = jnp.mean(xo * xo, axis=-1, keepdims=True)
    o_ref[...] = xo * lax.rsqrt(ms + RMS_EPS) * fng_ref[...]


def _combine(dest, gate_t, xn, hpa, hpb, g2, ws_gu, ws_dn, fng, ya, yb, tokens_per_batch):
    n, d = xn.shape
    tm = min(COMB_TILE, tokens_per_batch)
    nt = n // tm
    tpb = tokens_per_batch // tm
    const2 = lambda i: (0, 0)
    row_map = lambda i: (i, 0)
    idx = dest.reshape(1, TOP_K * n)
    yga = _sc_gather_rows(ya, idx).reshape(TOP_K, n, PACK_W)
    ygb = _sc_gather_rows(yb, idx).reshape(TOP_K, n, PACK_W)
    return pl.pallas_call(
        _combine_kernel,
        grid=(nt,),
        in_specs=[pl.BlockSpec((tm, TOP_K), row_map),
                  pl.BlockSpec((tm, d), row_map),
                  pl.BlockSpec((tm, PACK_W), row_map),
                  pl.BlockSpec((tm, PACK_W), row_map),
                  pl.BlockSpec((1, 1, d), lambda i: (i // tpb, 0, 0)),
                  pl.BlockSpec(ws_gu.shape, const2),
                  pl.BlockSpec(ws_dn.shape, const2),
                  pl.BlockSpec((1, d), const2),
                  pl.BlockSpec((TOP_K, tm, PACK_W), lambda i: (0, i, 0)),
                  pl.BlockSpec((TOP_K, tm, PACK_W), lambda i: (0, i, 0))],
        out_specs=pl.BlockSpec((tm, d), row_map),
        out_shape=jax.ShapeDtypeStruct((n, d), F32),
        compiler_params=_cp(("arbitrary",)),
        name="combine",
    )(gate_t, xn, hpa, hpb, g2, ws_gu, ws_dn, fng, yga, ygb)


def _rope_tables(l):
    rows = l // GRID_W
    row = np.repeat(np.arange(rows, dtype=np.float32), GRID_W)
    col = np.tile(np.arange(GRID_W, dtype=np.float32), rows)
    n_freq = AXIS_ROT // 2
    inv_freq = (np.float32(ROPE_THETA) ** (-np.arange(n_freq, dtype=np.float32) / n_freq)).astype(np.float32)
    ang_r = row[:, None] * inv_freq
    ang_c = col[:, None] * inv_freq
    ang = np.concatenate([ang_r, ang_r, ang_c, ang_c], axis=-1).astype(np.float64)
    cos, sin = np.cos(ang), np.sin(ang)
    lane = np.arange(HEAD_DIM) % AXIS_ROT
    first = lane < AXIS_ROT // 2
    sin_a = np.where(first[None, :], -sin, 0.0)
    sin_b = np.where(first[None, :], 0.0, sin)
    two = lambda a: jnp.asarray(np.concatenate([a, a], axis=1), dtype=F32)
    return two(cos), two(sin_a), two(sin_b)


def _layer(x, ctx, c, c_ctx, lw, moe_w, fng):
    norm1_g, w_ada, b_ada, w_in, w_fourier, q_norm_g, k_norm_g, w_out, norm2_g = lw
    w_router, router_bias, w_expert_gu, w_expert_down, w_shared_gu, w_shared_down = moe_w
    b, l, d = x.shape
    n = b * l

    assert b + 1 <= SUBLANES
    cond = jnp.concatenate([c, c_ctx[None, :], jnp.zeros((SUBLANES - b - 1, d), F32)], axis=0)
    mod = _adaln(cond, w_ada, b_ada)
    sh1, sc1, g1, sh2, sc2, g2 = [m[:b, None, :] for m in jnp.split(mod, 6, axis=-1)]
    csh1, csc1 = [m[b:b + 1, None, :] for m in jnp.split(mod, 6, axis=-1)[:2]]

    w_in_bf = w_in.astype(BF16)
    two = lambda g, s: jnp.tile(g * s, 2).reshape(1, LANES)
    gq, gk = two(q_norm_g, ATTN_SCALE * float(np.log2(np.e))), two(k_norm_g, 1.0)
    ones_bd = jnp.asarray(np.kron(np.eye(NORM_W // HEAD_DIM), np.ones((HEAD_DIM, HEAD_DIM))),
                          dtype=F32).astype(BF16)
    n1g = norm1_g.reshape(1, d)

    kc, vct = _inproj(ctx, csc1, csh1, n1g, w_in_bf[:, KV_COL0:], gq, gk, ones_bd, None, False)
    u, qt, k, vt = _inproj(x, sc1, sh1, n1g, w_in_bf, gq, gk, ones_bd, _rope_tables(l), True)

    w_bd = jnp.zeros((FOURIER_WIDTH, FOURIER_WIDTH), F32)
    for h in range(FOURIER_HEADS):
        w_bd = lax.dynamic_update_slice(w_bd, w_fourier[h], (h * HEAD_DIM, h * HEAD_DIM))
    four = _fourier(u, w_bd.astype(BF16))
    attn = _attention(qt, k, vt, kc, vct)

    wr_t = w_router.T
    wr_hi = wr_t.astype(BF16)
    wr_lo = (wr_t - wr_hi.astype(F32)).astype(BF16)
    xn, hpa, hpb, top_e, gate, cnt = _outproj_router(
        x, four, attn, g1, w_out.astype(BF16), norm2_g.reshape(1, d), sc2, sh2,
        wr_hi, wr_lo, router_bias.reshape(N_EXPERTS, 1))

    tm = min(TOK_TILE, l)
    rb = ROW_BLOCK
    counts = cnt[:, 0, :].astype(I32)
    total = jnp.sum(counts, axis=0)
    padded = (total + rb - 1) // rb * rb
    pad_end = jnp.cumsum(padded)
    pad_start = pad_end - padded
    base = pad_start[None, :] + jnp.cumsum(counts, axis=0) - counts
    n_blocks = n * TOP_K // rb + N_EXPERTS

    dest = _dest_rows(top_e, base.astype(F32)[:, :, None], tm)
    xa = _sc_scatter_rows(hpa, dest, n_blocks * rb)
    xb = _sc_scatter_rows(hpb, dest, n_blocks * rb)
    ya, yb = _experts(pad_start // rb, padded // rb, total, xa, xb, w_expert_gu, w_expert_down)
    return _combine(dest, gate.T, xn.reshape(n, d), hpa, hpb, g2, w_shared_gu.astype(BF16),
                    w_shared_down.astype(BF16), fng, ya, yb, l)


def kernel(x, c, ctx, c_ctx, norm1_g, w_ada, b_ada, w_in, w_fourier, q_norm_g, k_norm_g, w_out, norm2_g,
           w_router, router_bias, w_expert_gu, w_expert_down, w_shared_gu, w_shared_down, final_norm_g):
    depth = norm1_g.shape[0]
    assert depth == 1, "context update between layers is not implemented"
    b, l, d = x.shape
    lw = (norm1_g[0], w_ada[0], b_ada[0], w_in[0], w_fourier[0], q_norm_g[0], k_norm_g[0], w_out[0], norm2_g[0])
    moe_w = (w_router[0], router_bias[0], w_expert_gu[0], w_expert_down[0], w_shared_gu[0], w_shared_down[0])
    out = _layer(x, ctx, c, c_ctx, lw, moe_w, final_norm_g.reshape(1, d))
    return out.reshape(b, l, d)
```

```python
import functools

import numpy as np
import jax
import jax.numpy as jnp
from jax import lax
from jax.experimental import pallas as pl
from jax.experimental.pallas import tpu as pltpu
from jax.experimental.pallas import tpu_sc as plsc

F32 = jnp.float32
BF16 = jnp.bfloat16
I32 = jnp.int32

D_MODEL = 1024
GRID_W = 64
HEAD_DIM = 64
FOURIER_HEADS = 4
FOURIER_WIDTH = 256
ATTN_HEADS = 12
KV_HEADS = 4
Q_PER_KV = 3
ATTN_WIDTH = 768
KV_WIDTH = 256
KV_COL0 = 1024
IN_WIDTH = 1536
ATTN_SCALE = HEAD_DIM ** -0.5
ROPE_THETA = 10000.0
AXIS_ROT = HEAD_DIM // 2
N_EXPERTS = 256
TOP_K = 8
N_EXPERT_GROUPS = 8
TOPK_GROUPS = 4
EXPERTS_PER_GROUP = 32
EXPERT_FF = 256
ROUTED_SCALE = 2.5
RMS_EPS = 1e-6

LANES = 128
SUBLANES = 8
PACK_W = D_MODEL // 4
DFT_L1 = 64
VMEM_LIMIT = 48 * 1024 * 1024

TOK_TILE = 512
NORM_W = 128
FOURIER_KB = 16
ATTN_BQ = 256
ATTN_Q_TILES = 4
ATTN_BK = 1024
QK_SPLIT = 2
ONES_ROWS = 16
ROW_BLOCK = 512
X_SLOTS = 4
COMB_TILE = 256
SC_WINDOW = 128
SC_CORES = 2
SC_SUBCORES = 16

_HI = lax.Precision.HIGHEST
_NT = (((1,), (1,)), ((), ()))


def _cp(sem, vmem=VMEM_LIMIT):
    return pltpu.CompilerParams(dimension_semantics=sem, vmem_limit_bytes=vmem)


def _sigmoid(v):
    return 1.0 / (1.0 + jnp.exp(-v))


def _pack_rows(v):
    w = v.shape[1] // 4

    def pack(lo, hi):
        lo = lax.bitcast_convert_type(lo.astype(BF16).astype(F32), I32)
        hi = lax.bitcast_convert_type(hi.astype(BF16).astype(F32), I32)
        return lax.shift_right_logical(lo, 16) | (hi & jnp.int32(-65536))

    return pack(v[:, :w], v[:, 2 * w:3 * w]), pack(v[:, w:2 * w], v[:, 3 * w:])


def _unpack_rows(a, b):
    lo = lambda t: lax.bitcast_convert_type(lax.shift_left(t, 16), F32)
    hi = lambda t: lax.bitcast_convert_type(t & jnp.int32(-65536), F32)
    return [lo(a), lo(b), hi(a), hi(b)]


def _adaln_kernel(c_ref, w_ref, b_ref, o_ref):
    c = c_ref[...]
    s = c * _sigmoid(c)
    o_ref[...] = jnp.dot(s, w_ref[...], precision=_HI, preferred_element_type=F32) + b_ref[...]


def _adaln(cond, w_ada, b_ada):
    rows, d = cond.shape
    n = w_ada.shape[1]
    tn = 1536
    return pl.pallas_call(
        _adaln_kernel,
        grid=(n // tn,),
        in_specs=[pl.BlockSpec((rows, d), lambda j: (0, 0)),
                  pl.BlockSpec((d, tn), lambda j: (0, j)),
                  pl.BlockSpec((1, tn), lambda j: (0, j))],
        out_specs=pl.BlockSpec((rows, tn), lambda j: (0, j)),
        out_shape=jax.ShapeDtypeStruct((rows, n), F32),
        compiler_params=_cp(("arbitrary",)),
        name="adaln",
    )(cond, w_ada, b_ada.reshape(1, n))


def _modulated_norm(x, g, sc, sh):
    ms = jnp.mean(x * x, axis=-1, keepdims=True)
    return x * lax.rsqrt(ms + RMS_EPS) * g * (1.0 + sc) + sh


def _head_rms_inv(chunk, ones_bd):
    sq = chunk * chunk
    hi = sq.astype(BF16)
    lo = (sq - hi.astype(F32)).astype(BF16)
    ss = (jnp.dot(hi, ones_bd, preferred_element_type=F32)
          + jnp.dot(lo, ones_bd, preferred_element_type=F32))
    return lax.rsqrt(ss * (1.0 / HEAD_DIM) + RMS_EPS)


def _rope(t, cos, sin_a, sin_b):
    return (t * cos + pltpu.roll(t, LANES - AXIS_ROT // 2, 1) * sin_a
            + pltpu.roll(t, AXIS_ROT // 2, 1) * sin_b)


def _inproj_kernel(*refs, rope, with_uq):
    if rope:
        (x_ref, sc_ref, sh_ref, g_ref, w_ref, gq_ref, gk_ref, ones_ref,
         cos_ref, sa_ref, sb_ref) = refs[:11]
        outs = refs[11:]
    else:
        x_ref, sc_ref, sh_ref, g_ref, w_ref, gq_ref, gk_ref, ones_ref = refs[:8]
        outs = refs[8:]
    if with_uq:
        u_ref, qt_ref, k_ref, vt_ref = outs
    else:
        k_ref, vt_ref = outs

    h = _modulated_norm(x_ref[0], g_ref[...], sc_ref[0], sh_ref[0])
    z = jnp.dot(h.astype(BF16), w_ref[...], preferred_element_type=F32)
    ones_bd = ones_ref[...]

    def normed(col0, width, gain):
        chunks = []
        for j in range(width // NORM_W):
            zz = z[:, col0 + j * NORM_W: col0 + (j + 1) * NORM_W]
            inv = _head_rms_inv(zz, ones_bd)
            for c in range(NORM_W // LANES):
                t = zz[:, c * LANES:(c + 1) * LANES] * inv[:, c * LANES:(c + 1) * LANES] * gain
                if rope:
                    t = _rope(t, cos_ref[...], sa_ref[...], sb_ref[...])
                chunks.append(t)
        return chunks

    col = 0
    if with_uq:
        u_ref[0] = z[:, :FOURIER_WIDTH].astype(BF16)
        col = FOURIER_WIDTH
        qt_ref[0] = jnp.concatenate(normed(col, ATTN_WIDTH, gq_ref[...]), axis=1).T.astype(BF16)
        col += ATTN_WIDTH
    for j, t in enumerate(normed(col, KV_WIDTH, gk_ref[...])):
        k_ref[0, 2 * j] = t[:, :HEAD_DIM].astype(BF16)
        k_ref[0, 2 * j + 1] = t[:, HEAD_DIM:].astype(BF16)
    col += KV_WIDTH
    vt_ref[0] = z[:, col:col + KV_WIDTH].T.astype(BF16)


def _inproj(x, sc, sh, g, w, gq, gk, ones_bd, rope_tabs, with_uq):
    b, t, d = x.shape
    tm = min(TOK_TILE, t)
    wcols = w.shape[1]
    bm = sc.shape[0]
    mod_map = (lambda i, j: (i, 0, 0)) if bm == b else (lambda i, j: (0, 0, 0))
    const2 = lambda i, j: (0, 0)
    in_specs = [pl.BlockSpec((1, tm, d), lambda i, j: (i, j, 0)),
                pl.BlockSpec((1, 1, d), mod_map),
                pl.BlockSpec((1, 1, d), mod_map),
                pl.BlockSpec((1, d), const2),
                pl.BlockSpec((d, wcols), const2),
                pl.BlockSpec((1, LANES), const2),
                pl.BlockSpec((1, LANES), const2),
                pl.BlockSpec((NORM_W, NORM_W), const2)]
    args = [x, sc, sh, g, w, gq, gk, ones_bd]
    rope = rope_tabs is not None
    if rope:
        in_specs += [pl.BlockSpec((tm, LANES), lambda i, j: (j, 0))] * 3
        args += list(rope_tabs)
    out_specs, out_shape = [], []
    if with_uq:
        out_specs += [pl.BlockSpec((1, tm, FOURIER_WIDTH), lambda i, j: (i, j, 0)),
                      pl.BlockSpec((1, ATTN_WIDTH, tm), lambda i, j: (i, 0, j))]
        out_shape += [jax.ShapeDtypeStruct((b, t, FOURIER_WIDTH), BF16),
                      jax.ShapeDtypeStruct((b, ATTN_WIDTH, t), BF16)]
    out_specs += [pl.BlockSpec((1, KV_HEADS, tm, HEAD_DIM), lambda i, j: (i, 0, j, 0)),
                  pl.BlockSpec((1, KV_WIDTH, tm), lambda i, j: (i, 0, j))]
    out_shape += [jax.ShapeDtypeStruct((b, KV_HEADS, t, HEAD_DIM), BF16),
                  jax.ShapeDtypeStruct((b, KV_WIDTH, t), BF16)]
    return pl.pallas_call(
        functools.partial(_inproj_kernel, rope=rope, with_uq=with_uq),
        grid=(b, t // tm),
        in_specs=in_specs,
        out_specs=out_specs,
        out_shape=out_shape,
        compiler_params=_cp(("arbitrary", "arbitrary")),
        name="inproj_latent" if with_uq else "inproj_ctx",
    )(*args)


def _fourier_a_kernel(u_ref, c_ref, s_ref, yr_ref, yi_ref):
    u = u_ref[0]
    yr_ref[0] = jnp.dot(c_ref[...], u, preferred_element_type=F32).astype(BF16)
    yi_ref[0] = jnp.dot(s_ref[...], u, preferred_element_type=F32).astype(BF16)


def _fourier_b_kernel(yr_ref, yi_ref, m_ref, c_ref, s_ref, w_ref, o_ref, *, kb):
    y = jnp.concatenate([yr_ref[0], yi_ref[0]], axis=1)
    x = jnp.einsum("kab,kbc->kac", m_ref[...], y, preferred_element_type=F32)
    xr = x[:, :DFT_L1].reshape(kb * DFT_L1, FOURIER_WIDTH).astype(BF16)
    xi = x[:, DFT_L1:].reshape(kb * DFT_L1, FOURIER_WIDTH).astype(BF16)
    spec = (jnp.dot(xr, c_ref[...], preferred_element_type=F32)
            + jnp.dot(xi, s_ref[...], preferred_element_type=F32))
    o = jnp.dot(spec.astype(BF16), w_ref[...], preferred_element_type=F32)
    for j in range(kb):
        o_ref[0, :, j, :] = o[j * DFT_L1:(j + 1) * DFT_L1]


def _dft_tables(l):
    l2 = l // DFT_L1
    k2 = np.arange(l2)
    ang2 = 2.0 * np.pi * ((k2[:, None] * k2[None, :]) % l2) / l2
    c2 = np.cos(ang2)
    s2n = -np.sin(ang2)
    n1 = np.arange(DFT_L1)
    k = l2 * n1[None, :, None] + k2[:, None, None]
    ang = 2.0 * np.pi * ((k * n1[None, None, :]) % l) / l
    mr, mi = np.cos(ang), -np.sin(ang)
    m = np.concatenate([np.concatenate([mr, -mi], axis=2),
                        np.concatenate([mi, mr], axis=2)], axis=1)
    d = np.arange(HEAD_DIM)
    angc = 2.0 * np.pi * ((d[:, None] * d[None, :]) % HEAD_DIM) / HEAD_DIM
    scale = 1.0 / np.sqrt(float(l) * HEAD_DIM)
    eye = np.eye(FOURIER_HEADS)
    cbd = np.kron(eye, np.cos(angc) * scale)
    sbd = np.kron(eye, np.sin(angc) * scale)
    as_bf = lambda a: jnp.asarray(a, dtype=F32).astype(BF16)
    return as_bf(c2), as_bf(s2n), as_bf(m), as_bf(cbd), as_bf(sbd)


def _fourier(u, w_bd):
    b, l, fw = u.shape
    l2 = l // DFT_L1
    ncol = DFT_L1 * fw
    tn = min(4096, ncol)
    kb = min(FOURIER_KB, l2)
    c2, s2n, m, cbd, sbd = _dft_tables(l)
    yr, yi = pl.pallas_call(
        _fourier_a_kernel,
        grid=(b, ncol // tn),
        in_specs=[pl.BlockSpec((1, l2, tn), lambda i, j: (i, 0, j)),
                  pl.BlockSpec((l2, l2), lambda i, j: (0, 0)),
                  pl.BlockSpec((l2, l2), lambda i, j: (0, 0))],
        out_specs=[pl.BlockSpec((1, l2, tn), lambda i, j: (i, 0, j))] * 2,
        out_shape=[jax.ShapeDtypeStruct((b, l2, ncol), BF16)] * 2,
        compiler_params=_cp(("arbitrary", "arbitrary")),
        name="fourier_a",
    )(u.reshape(b, l2, ncol), c2, s2n)
    yr = yr.reshape(b, l2, DFT_L1, fw)
    yi = yi.reshape(b, l2, DFT_L1, fw)
    out = pl.pallas_call(
        functools.partial(_fourier_b_kernel, kb=kb),
        grid=(b, l2 // kb),
        in_specs=[pl.BlockSpec((1, kb, DFT_L1, fw), lambda i, j: (i, j, 0, 0)),
                  pl.BlockSpec((1, kb, DFT_L1, fw), lambda i, j: (i, j, 0, 0)),
                  pl.BlockSpec((kb, 2 * DFT_L1, 2 * DFT_L1), lambda i, j: (j, 0, 0)),
                  pl.BlockSpec((fw, fw), lambda i, j: (0, 0)),
                  pl.BlockSpec((fw, fw), lambda i, j: (0, 0)),
                  pl.BlockSpec((fw, fw), lambda i, j: (0, 0))],
        out_specs=pl.BlockSpec((1, DFT_L1, kb, fw), lambda i, j: (i, 0, j, 0)),
        out_shape=jax.ShapeDtypeStruct((b, DFT_L1, l2, fw), F32),
        compiler_params=_cp(("arbitrary", "arbitrary")),
        name="fourier_b",
    )(yr, yi, m, cbd, sbd, w_bd)
    return out.reshape(b, l, fw)


def _attn_kernel(qt_ref, k_ref, vt_ref, kc_ref, vct_ref, o_ref, m_scr, acc_scr, *s_scrs, bk, nchunks, bq):
    def tile(sub, carry):
        _attn_tile(pl.multiple_of(sub * bq, bq), qt_ref, k_ref, vt_ref, kc_ref, vct_ref, o_ref, m_scr, acc_scr,
                   s_scrs, bk=bk, nchunks=nchunks, bq=bq)
        return carry

    lax.fori_loop(0, qt_ref.shape[2] // bq, tile, 0)


def _attn_tile(qoff, qt_ref, k_ref, vt_ref, kc_ref, vct_ref, o_ref, m_scr, acc_scr, s_scrs, *, bk, nchunks, bq):
    m_scr[...] = jnp.full(m_scr.shape, -jnp.inf, F32)
    acc_scr[...] = jnp.zeros(acc_scr.shape, F32)

    def scores(k, buf, h):
        nk = k.shape[0]
        qt = qt_ref[0, h * HEAD_DIM:(h + 1) * HEAD_DIM, pl.ds(qoff, bq)]
        part = nk // QK_SPLIT
        for j in range(QK_SPLIT):
            s_scrs[buf][j * part:(j + 1) * part, :] = jnp.dot(k[j * part:(j + 1) * part], qt,
                                                              preferred_element_type=F32)

    def softmax_pv(buf, h, vt):
        nk = vt.shape[1]
        s = s_scrs[buf][:nk, :]
        m_old = m_scr[h]
        m_new = jnp.maximum(m_old, jnp.max(s, axis=0, keepdims=True))
        alpha = jnp.exp2(m_old - m_new)
        p = jnp.exp2((s - m_new).astype(BF16))
        vt_ones = jnp.concatenate([vt, jnp.ones((ONES_ROWS, nk), BF16)], axis=0)
        acc_scr[h] = alpha * acc_scr[h] + jnp.dot(vt_ones, p, preferred_element_type=F32)
        m_scr[h] = m_new

    def keys(c):
        if isinstance(c, int) and c == nchunks:
            return kc_ref[0, 0]
        return k_ref[0, 0, pl.ds(pl.multiple_of(c * bk, bk), bk), :]

    def values(c):
        if isinstance(c, int) and c == nchunks:
            return vct_ref[0]
        return vt_ref[0, :, pl.ds(pl.multiple_of(c * bk, bk), bk)]

    nbuf = len(s_scrs)
    pair = 1
    ahead = nbuf - pair
    n_units = Q_PER_KV * (nchunks + 1)

    def group(us, chunk0=0):
        for u in us:
            ua = u + ahead
            if not (isinstance(chunk0, int) and ua >= n_units):
                scores(keys(chunk0 + ua // Q_PER_KV), ua % nbuf, ua % Q_PER_KV)
        for u in us:
            softmax_pv(u % nbuf, u % Q_PER_KV, values(chunk0 + u // Q_PER_KV))

    for u in range(min(ahead, n_units)):
        scores(keys(u // Q_PER_KV), u % nbuf, u % Q_PER_KV)

    chunks_per_iter = nbuf // Q_PER_KV
    n_iter = max(0, (Q_PER_KV * nchunks - ahead) // nbuf)

    def body(j, carry):
        for t in range(0, nbuf, pair):
            group(range(t, t + pair), j * chunks_per_iter)
        return carry

    lax.fori_loop(0, n_iter, body, 0)
    for t in range(n_iter * nbuf, n_units, pair):
        group(range(t, min(t + pair, n_units)))
    outs = [acc_scr[h, :HEAD_DIM, :] / acc_scr[h, HEAD_DIM:HEAD_DIM + 1, :] for h in range(Q_PER_KV)]
    o_t = jnp.concatenate(outs + [jnp.zeros((HEAD_DIM, bq), F32)], axis=0)
    o_ref[0, 0, pl.ds(qoff, bq), :] = o_t.T[:, :Q_PER_KV * HEAD_DIM].astype(BF16)


def _attention(qt, k, vt, kc, vct):
    b, _, l = qt.shape
    c = kc.shape[2]
    bq = min(ATTN_BQ, l)
    bstep = min(ATTN_BQ * ATTN_Q_TILES, l)
    bk = min(ATTN_BK, l)
    gw = Q_PER_KV * HEAD_DIM
    return pl.pallas_call(
        functools.partial(_attn_kernel, bk=bk, nchunks=l // bk, bq=bq),
        grid=(b, KV_HEADS, l // bstep),
        in_specs=[pl.BlockSpec((1, gw, bstep), lambda i, g, j: (i, g, j)),
                  pl.BlockSpec((1, 1, l, HEAD_DIM), lambda i, g, j: (i, g, 0, 0)),
                  pl.BlockSpec((1, HEAD_DIM, l), lambda i, g, j: (i, g, 0)),
                  pl.BlockSpec((1, 1, c, HEAD_DIM), lambda i, g, j: (i, g, 0, 0)),
                  pl.BlockSpec((1, HEAD_DIM, c), lambda i, g, j: (i, g, 0))],
        out_specs=pl.BlockSpec((1, 1, bstep, gw), lambda i, g, j: (i, g, j, 0)),
        out_shape=jax.ShapeDtypeStruct((b, KV_HEADS, l, gw), BF16),
        scratch_shapes=[pltpu.VMEM((Q_PER_KV, 1, bq), F32),
                        pltpu.VMEM((Q_PER_KV, HEAD_DIM + ONES_ROWS, bq), F32),
                        *[pltpu.VMEM((max(bk, c), bq), F32)] * (2 * Q_PER_KV)],
        compiler_params=_cp(("arbitrary", "arbitrary", "arbitrary")),
        name="attention",
    )(qt, k, vt, kc, vct)


def _outproj_router_kernel(x_ref, f_ref, a_ref, g1_ref, wo_ref, n2g_ref, sc_ref, sh_ref,
                           wrh_ref, wrl_ref, bias_ref, xn_ref, hpa_ref, hpb_ref, te_ref, gt_ref, cnt_ref):
    tm = x_ref.shape[1]
    mix = jnp.dot(f_ref[0].astype(BF16), wo_ref[:FOURIER_WIDTH, :], preferred_element_type=F32)
    gw = Q_PER_KV * HEAD_DIM
    for g in range(KV_HEADS):
        r0 = FOURIER_WIDTH + g * gw
        mix += jnp.dot(a_ref[0, g], wo_ref[r0:r0 + gw, :], preferred_element_type=F32)
    xn = x_ref[0] + g1_ref[0] * mix
    xn_ref[0] = xn
    h2 = _modulated_norm(xn, n2g_ref[...], sc_ref[0], sh_ref[0])
    h2_hi = h2.astype(BF16)
    h2_lo = (h2 - h2_hi.astype(F32)).astype(BF16)
    hpa_ref[...], hpb_ref[...] = _pack_rows(h2)

    wrh = wrh_ref[...]
    logits = (lax.dot_general(wrh, h2_hi, _NT, preferred_element_type=F32)
              + lax.dot_general(wrh, h2_lo, _NT, preferred_element_type=F32)
              + lax.dot_general(wrl_ref[...], h2_hi, _NT, preferred_element_type=F32))
    scores = _sigmoid(logits)
    sel = scores + bias_ref[...]

    neg = jnp.float32(-jnp.inf)
    s3 = sel.reshape(N_EXPERT_GROUPS, EXPERTS_PER_GROUP, tm)
    i3 = lax.broadcasted_iota(I32, s3.shape, 1)
    m1 = jnp.max(s3, axis=1, keepdims=True)
    i1 = jnp.min(jnp.where(s3 == m1, i3, EXPERTS_PER_GROUP), axis=1, keepdims=True)
    m2 = jnp.max(jnp.where(i3 == i1, neg, s3), axis=1)
    gs = m1[:, 0, :] + m2
    gi = lax.broadcasted_iota(I32, gs.shape, 0)
    keep = jnp.zeros(gs.shape, jnp.bool_)
    for _ in range(TOPK_GROUPS):
        m = jnp.max(gs, axis=0, keepdims=True)
        idx = jnp.min(jnp.where(gs == m, gi, N_EXPERT_GROUPS), axis=0, keepdims=True)
        hit = gi == idx
        keep = keep | hit
        gs = jnp.where(hit, neg, gs)
    keep3 = jnp.broadcast_to(keep[:, None, :], s3.shape)
    selm = jnp.where(keep3, s3, neg).reshape(N_EXPERTS, tm)

    ei = lax.broadcasted_iota(I32, selm.shape, 0)
    eligible = selm
    idxs, gates = [], []
    for _ in range(TOP_K):
        m = jnp.max(selm, axis=0, keepdims=True)
        idx = jnp.min(jnp.where(selm == m, ei, N_EXPERTS), axis=0, keepdims=True)
        hit = ei == idx
        gates.append(jnp.sum(jnp.where(hit, scores, 0.0), axis=0, keepdims=True))
        idxs.append(idx)
        selm = jnp.where(hit, neg, selm)
    multi = (eligible != selm).astype(F32)
    gate = jnp.concatenate(gates, axis=0)
    gate = gate / jnp.sum(gate, axis=0, keepdims=True) * ROUTED_SCALE
    te_ref[...] = jnp.concatenate(idxs, axis=0)
    gt_ref[...] = gate
    ones = jnp.ones((SUBLANES, tm), BF16)
    cnt_ref[0] = lax.dot_general(ones, multi.astype(BF16), _NT, preferred_element_type=F32)


def _outproj_router(x, four, attn, g1, w_out, n2g, sc2, sh2, wr_hi, wr_lo, bias):
    b, l, d = x.shape
    tm = min(TOK_TILE, l)
    tpb = l // tm
    n = b * l
    const2 = lambda i, j: (0, 0)
    mod_map = lambda i, j: (i, 0, 0)
    tok_map = lambda i, j: (0, i * tpb + j)
    return pl.pallas_call(
        _outproj_router_kernel,
        grid=(b, tpb),
        in_specs=[pl.BlockSpec((1, tm, d), lambda i, j: (i, j, 0)),
                  pl.BlockSpec((1, tm, FOURIER_WIDTH), lambda i, j: (i, j, 0)),
                  pl.BlockSpec((1, KV_HEADS, tm, Q_PER_KV * HEAD_DIM), lambda i, j: (i, 0, j, 0)),
                  pl.BlockSpec((1, 1, d), mod_map),
                  pl.BlockSpec((d, d), const2),
                  pl.BlockSpec((1, d), const2),
                  pl.BlockSpec((1, 1, d), mod_map),
                  pl.BlockSpec((1, 1, d), mod_map),
                  pl.BlockSpec((N_EXPERTS, d), const2),
                  pl.BlockSpec((N_EXPERTS, d), const2),
                  pl.BlockSpec((N_EXPERTS, 1), const2)],
        out_specs=[pl.BlockSpec((1, tm, d), lambda i, j: (i, j, 0)),
                   pl.BlockSpec((tm, PACK_W), lambda i, j: (i * tpb + j, 0)),
                   pl.BlockSpec((tm, PACK_W), lambda i, j: (i * tpb + j, 0)),
                   pl.BlockSpec((TOP_K, tm), tok_map),
                   pl.BlockSpec((TOP_K, tm), tok_map),
                   pl.BlockSpec((1, SUBLANES, N_EXPERTS), lambda i, j: (i * tpb + j, 0, 0))],
        out_shape=[jax.ShapeDtypeStruct((b, l, d), F32),
                   jax.ShapeDtypeStruct((n, PACK_W), I32),
                   jax.ShapeDtypeStruct((n, PACK_W), I32),
                   jax.ShapeDtypeStruct((TOP_K, n), I32),
                   jax.ShapeDtypeStruct((TOP_K, n), F32),
                   jax.ShapeDtypeStruct((n // tm, SUBLANES, N_EXPERTS), F32)],
        compiler_params=_cp(("arbitrary", "arbitrary")),
        name="outproj_router",
    )(x, four, attn, g1, w_out, n2g, sc2, sh2, wr_hi, wr_lo, bias)


def _dest_kernel(te_ref, base_ref, tri_ref, d_ref):
    te = te_ref[...]
    tm = te.shape[1]
    ei = lax.broadcasted_iota(I32, (N_EXPERTS, tm), 0)
    hits = [ei == te[k:k + 1, :] for k in range(TOP_K)]
    multi = hits[0].astype(F32)
    for k in range(1, TOP_K):
        multi = multi + hits[k].astype(F32)
    rank = jnp.dot(multi.astype(BF16), tri_ref[...], preferred_element_type=F32)
    pos = rank + base_ref[0]
    rows = [jnp.sum(jnp.where(hits[k], pos, 0.0), axis=0, keepdims=True) for k in range(TOP_K)]
    d_ref[...] = jnp.concatenate(rows, axis=0).astype(I32)


def _dest_rows(top_e, base, tm):
    n = top_e.shape[1]
    tri = jnp.asarray(np.triu(np.ones((tm, tm), np.float32), 1)).astype(BF16)
    return pl.pallas_call(
        _dest_kernel,
        grid=(n // tm,),
        in_specs=[pl.BlockSpec((TOP_K, tm), lambda i: (0, i)),
                  pl.BlockSpec((1, N_EXPERTS, 1), lambda i: (i, 0, 0)),
                  pl.BlockSpec((tm, tm), lambda i: (0, 0))],
        out_specs=pl.BlockSpec((TOP_K, tm), lambda i: (0, i)),
        out_shape=jax.ShapeDtypeStruct((TOP_K, n), I32),
        compiler_params=_cp(("arbitrary",)),
        name="dest_rows",
    )(top_e, base, tri)


def _sc_mesh():
    return plsc.VectorSubcoreMesh(core_axis_name="c", subcore_axis_name="s",
                                  num_cores=SC_CORES, num_subcores=SC_SUBCORES)


def _sc_scatter_rows(src, dest, n_rows):
    n, w = src.shape
    kk = dest.shape[0]

    @pl.kernel(out_type=jax.ShapeDtypeStruct((n_rows, w), src.dtype), mesh=_sc_mesh(), name="sc_scatter_rows")
    def scatter(x_hbm, d_hbm, o_hbm):
        def body(x_vmem, d_vmem):
            for k in range(kk):
                pltpu.sync_copy(x_vmem, o_hbm.at[d_vmem.at[k]])

        pltpu.emit_pipeline(
            body,
            grid=(n // SC_WINDOW,),
            in_specs=[pl.BlockSpec((SC_WINDOW, w), lambda i: (i, 0)),
                      pl.BlockSpec((kk, SC_WINDOW), lambda i: (0, i))],
            out_specs=[],
            core_axis_name=("c", "s"),
            dimension_semantics=(pltpu.PARALLEL,),
        )(x_hbm, d_hbm)

    return scatter(src, dest)


def _swiglu(quarters, wgu, wdn):
    ag = None
    for j, q in enumerate(quarters):
        part = jnp.dot(q, wgu[j * PACK_W:(j + 1) * PACK_W, :], preferred_element_type=F32)
        ag = part if ag is None else ag + part
    ff = ag.shape[1] // 2
    a, g = ag[:, :ff], ag[:, ff:]
    mid = (a * _sigmoid(a) * g).astype(BF16)
    return jnp.dot(mid, wdn[...], preferred_element_type=F32)


def _expert_kernel(blk0_ref, nblk_ref, cnt_ref, xa_hbm, xb_hbm, wgu_ref, wdn_ref, ya_hbm, yb_hbm,
                   xbuf, ybuf, wgu_s, wdn_s, in_sem, out_sem, pending):
    e = pl.program_id(0)
    last = pl.num_programs(0) - 1
    rb = xbuf.shape[2]
    nb = nblk_ref[e]
    blk0 = blk0_ref[e]
    n_used = blk0_ref[last] + nblk_ref[last]
    x_hbm = (xa_hbm, xb_hbm)
    y_hbm = (ya_hbm, yb_hbm)

    def x_copy(g, t):
        slot = g % X_SLOTS
        return pltpu.make_async_copy(x_hbm[t].at[pl.ds(g * rb, rb)], xbuf.at[slot, t], in_sem.at[slot, t])

    def x_start(g):
        @pl.when(g < n_used)
        def _():
            for t in range(2):
                x_copy(g, t).start()

    def y_copy(g, slot, t):
        return pltpu.make_async_copy(ybuf.at[slot, t], y_hbm[t].at[pl.ds(g * rb, rb)], out_sem.at[slot, t])

    def drain(slot):
        @pl.when(pending[slot] == 1)
        def _():
            for t in range(2):
                y_copy(0, slot, t).wait()
            pending[slot] = 0

    @pl.when(e == 0)
    def _():
        pending[0] = 0
        pending[1] = 0
        for g in range(X_SLOTS - 1):
            x_start(g)

    @pl.when(nb > 0)
    def _():
        wgu_s[...] = wgu_ref[0].astype(BF16)
        wdn_s[...] = wdn_ref[0].astype(BF16)

        def body(j, carry):
            g = blk0 + j
            for t in range(2):
                x_copy(g, t).wait()
            x_start(g + X_SLOTS - 1)
            xs = g % X_SLOTS
            ys = g % 2
            drain(ys)
            valid = lax.broadcasted_iota(I32, (rb, PACK_W), 0) < cnt_ref[e] - j * rb
            xa = jnp.where(valid, xbuf[xs, 0], 0)
            xb = jnp.where(valid, xbuf[xs, 1], 0)
            quarters = [q.astype(BF16) for q in _unpack_rows(xa, xb)]
            ya, yb = _pack_rows(_swiglu(quarters, wgu_s, wdn_s))
            ybuf[ys, 0] = ya
            ybuf[ys, 1] = yb
            for t in range(2):
                y_copy(g, ys, t).start()
            pending[ys] = 1
            return carry

        lax.fori_loop(0, nb, body, 0)

    @pl.when(e == last)
    def _():
        drain(0)
        drain(1)
        n_total = ya_hbm.shape[0] // rb
        ybuf[0, 0] = jnp.zeros((rb, PACK_W), I32)

        def zero_copy(b, t):
            return pltpu.make_async_copy(ybuf.at[0, 0], y_hbm[t].at[pl.ds(b * rb, rb)], out_sem.at[0, t])

        def start_zero(b, carry):
            for t in range(2):
                zero_copy(b, t).start()
            return carry

        def wait_zero(b, carry):
            for t in range(2):
                zero_copy(b, t).wait()
            return carry

        lax.fori_loop(n_used, n_total, start_zero, 0)
        lax.fori_loop(n_used, n_total, wait_zero, 0)


def _experts(blk0, nblk, cnt, xa, xb, w_gu, w_down):
    n_rows = xa.shape[0]
    rb = ROW_BLOCK
    n_experts, d, ff2 = w_gu.shape
    w_map = lambda e, b0, nb, ct: (e, 0, 0)
    grid_spec = pltpu.PrefetchScalarGridSpec(
        num_scalar_prefetch=3,
        grid=(n_experts,),
        in_specs=[pl.BlockSpec(memory_space=pl.ANY),
                  pl.BlockSpec(memory_space=pl.ANY),
                  pl.BlockSpec((1, d, ff2), w_map),
                  pl.BlockSpec((1, ff2 // 2, d), w_map)],
        out_specs=[pl.BlockSpec(memory_space=pl.ANY)] * 2,
        scratch_shapes=[pltpu.VMEM((X_SLOTS, 2, rb, PACK_W), I32), pltpu.VMEM((2, 2, rb, PACK_W), I32),
                        pltpu.VMEM((d, ff2), BF16), pltpu.VMEM((ff2 // 2, d), BF16),
                        pltpu.SemaphoreType.DMA((X_SLOTS, 2)), pltpu.SemaphoreType.DMA((2, 2)),
                        pltpu.SMEM((2,), I32)],
    )
    return pl.pallas_call(
        _expert_kernel,
        grid_spec=grid_spec,
        out_shape=[jax.ShapeDtypeStruct((n_rows, PACK_W), I32)] * 2,
        compiler_params=_cp(("arbitrary",)),
        name="experts",
    )(blk0, nblk, cnt, xa, xb, w_gu, w_down)


def _sc_gather_rows(table, idx):
    m = idx.shape[1]
    w = table.shape[1]

    @pl.kernel(out_type=jax.ShapeDtypeStruct((m, w), table.dtype), mesh=_sc_mesh(), name="sc_gather_rows")
    def gather(t_hbm, i_hbm, o_hbm):
        def body(i_vmem, o_vmem):
            pltpu.sync_copy(t_hbm.at[i_vmem.at[0]], o_vmem)

        pltpu.emit_pipeline(
            body,
            grid=(m // SC_WINDOW,),
            in_specs=[pl.BlockSpec((1, SC_WINDOW), lambda i: (0, i))],
            out_specs=[pl.BlockSpec((SC_WINDOW, w), lambda i: (i, 0))],
            core_axis_name=("c", "s"),
            dimension_semantics=(pltpu.PARALLEL,),
        )(i_hbm, o_hbm)

    return gather(table, idx)


def _combine_kernel(gate_ref, xn_ref, hpa_ref, hpb_ref, g2_ref, wsgu_ref, wsdn_ref, fng_ref,
                    yga_ref, ygb_ref, o_ref):
    quarters = [q.astype(BF16) for q in _unpack_rows(hpa_ref[...], hpb_ref[...])]
    shared = _swiglu(quarters, wsgu_ref, wsdn_ref)
    gate = gate_ref[...]
    acc = None
    for k in range(TOP_K):
        gk = gate[:, k:k + 1]
        rows = [gk * q for q in _unpack_rows(yga_ref[k], ygb_ref[k])]
        acc = rows if acc is None else [a + r for a, r in zip(acc, rows)]
    y = jnp.concatenate(acc, axis=1) + shared
    xo = xn_ref[...] + g2_ref[0] * y
    ms = jnp.mean(xo * xo, axis=-1, keepdims=True)
    o_ref[...] = xo * lax.rsqrt(ms + RMS_EPS) * fng_ref[...]


def _combine(dest, gate_t, xn, hpa, hpb, g2, ws_gu, ws_dn, fng, ya, yb, tokens_per_batch):
    n, d = xn.shape
    tm = min(COMB_TILE, tokens_per_batch)
    nt = n // tm
    tpb = tokens_per_batch // tm
    const2 = lambda i: (0, 0)
    row_map = lambda i: (i, 0)
    idx = dest.reshape(1, TOP_K * n)
    yga = _sc_gather_rows(ya, idx).reshape(TOP_K, n, PACK_W)
    ygb = _sc_gather_rows(yb, idx).reshape(TOP_K, n, PACK_W)
    return pl.pallas_call(
        _combine_kernel,
        grid=(nt,),
        in_specs=[pl.BlockSpec((tm, TOP_K), row_map),
                  pl.BlockSpec((tm, d), row_map),
                  pl.BlockSpec((tm, PACK_W), row_map),
                  pl.BlockSpec((tm, PACK_W), row_map),
                  pl.BlockSpec((1, 1, d), lambda i: (i // tpb, 0, 0)),
                  pl.BlockSpec(ws_gu.shape, const2),
                  pl.BlockSpec(ws_dn.shape, const2),
                  pl.BlockSpec((1, d), const2),
                  pl.BlockSpec((TOP_K, tm, PACK_W), lambda i: (0, i, 0)),
                  pl.BlockSpec((TOP_K, tm, PACK_W), lambda i: (0, i, 0))],
        out_specs=pl.BlockSpec((tm, d), row_map),
        out_shape=jax.ShapeDtypeStruct((n, d), F32),
        compiler_params=_cp(("arbitrary",)),
        name="combine",
    )(gate_t, xn, hpa, hpb, g2, ws_gu, ws_dn, fng, yga, ygb)


def _rope_tables(l):
    rows = l // GRID_W
    row = np.repeat(np.arange(rows, dtype=np.float32), GRID_W)
    col = np.tile(np.arange(GRID_W, dtype=np.float32), rows)
    n_freq = AXIS_ROT // 2
    inv_freq = (np.float32(ROPE_THETA) ** (-np.arange(n_freq, dtype=np.float32) / n_freq)).astype(np.float32)
    ang_r = row[:, None] * inv_freq
    ang_c = col[:, None] * inv_freq
    ang = np.concatenate([ang_r, ang_r, ang_c, ang_c], axis=-1).astype(np.float64)
    cos, sin = np.cos(ang), np.sin(ang)
    lane = np.arange(HEAD_DIM) % AXIS_ROT
    first = lane < AXIS_ROT // 2
    sin_a = np.where(first[None, :], -sin, 0.0)
    sin_b = np.where(first[None, :], 0.0, sin)
    two = lambda a: jnp.asarray(np.concatenate([a, a], axis=1), dtype=F32)
    return two(cos), two(sin_a), two(sin_b)


def _layer(x, ctx, c, c_ctx, lw, moe_w, fng):
    norm1_g, w_ada, b_ada, w_in, w_fourier, q_norm_g, k_norm_g, w_out, norm2_g = lw
    w_router, router_bias, w_expert_gu, w_expert_down, w_shared_gu, w_shared_down = moe_w
    b, l, d = x.shape
    n = b * l

    assert b + 1 <= SUBLANES
    cond = jnp.concatenate([c, c_ctx[None, :], jnp.zeros((SUBLANES - b - 1, d), F32)], axis=0)
    mod = _adaln(cond, w_ada, b_ada)
    sh1, sc1, g1, sh2, sc2, g2 = [m[:b, None, :] for m in jnp.split(mod, 6, axis=-1)]
    csh1, csc1 = [m[b:b + 1, None, :] for m in jnp.split(mod, 6, axis=-1)[:2]]

    w_in_bf = w_in.astype(BF16)
    two = lambda g, s: jnp.tile(g * s, 2).reshape(1, LANES)
    gq, gk = two(q_norm_g, ATTN_SCALE * float(np.log2(np.e))), two(k_norm_g, 1.0)
    ones_bd = jnp.asarray(np.kron(np.eye(NORM_W // HEAD_DIM), np.ones((HEAD_DIM, HEAD_DIM))),
                          dtype=F32).astype(BF16)
    n1g = norm1_g.reshape(1, d)

    kc, vct = _inproj(ctx, csc1, csh1, n1g, w_in_bf[:, KV_COL0:], gq, gk, ones_bd, None, False)
    u, qt, k, vt = _inproj(x, sc1, sh1, n1g, w_in_bf, gq, gk, ones_bd, _rope_tables(l), True)

    w_bd = jnp.zeros((FOURIER_WIDTH, FOURIER_WIDTH), F32)
    for h in range(FOURIER_HEADS):
        w_bd = lax.dynamic_update_slice(w_bd, w_fourier[h], (h * HEAD_DIM, h * HEAD_DIM))
    four = _fourier(u, w_bd.astype(BF16))
    attn = _attention(qt, k, vt, kc, vct)

    wr_t = w_router.T
    wr_hi = wr_t.astype(BF16)
    wr_lo = (wr_t - wr_hi.astype(F32)).astype(BF16)
    xn, hpa, hpb, top_e, gate, cnt = _outproj_router(
        x, four, attn, g1, w_out.astype(BF16), norm2_g.reshape(1, d), sc2, sh2,
        wr_hi, wr_lo, router_bias.reshape(N_EXPERTS, 1))

    tm = min(TOK_TILE, l)
    rb = ROW_BLOCK
    counts = cnt[:, 0, :].astype(I32)
    total = jnp.sum(counts, axis=0)
    padded = (total + rb - 1) // rb * rb
    pad_end = jnp.cumsum(padded)
    pad_start = pad_end - padded
    base = pad_start[None, :] + jnp.cumsum(counts, axis=0) - counts
    n_blocks = n * TOP_K // rb + N_EXPERTS

    dest = _dest_rows(top_e, base.astype(F32)[:, :, None], tm)
    xa = _sc_scatter_rows(hpa, dest, n_blocks * rb)
    xb = _sc_scatter_rows(hpb, dest, n_blocks * rb)
    ya, yb = _experts(pad_start // rb, padded // rb, total, xa, xb, w_expert_gu, w_expert_down)
    return _combine(dest, gate.T, xn.reshape(n, d), hpa, hpb, g2, w_shared_gu.astype(BF16),
                    w_shared_down.astype(BF16), fng, ya, yb, l)


def kernel(x, c, ctx, c_ctx, norm1_g, w_ada, b_ada, w_in, w_fourier, q_norm_g, k_norm_g, w_out, norm2_g,
           w_router, router_bias, w_expert_gu, w_expert_down, w_shared_gu, w_shared_down, final_norm_g):
    depth = norm1_g.shape[0]
    assert depth == 1, "context update between layers is not implemented"
    b, l, d = x.shape
    lw = (norm1_g[0], w_ada[0], b_ada[0], w_in[0], w_fourier[0], q_norm_g[0], k_norm_g[0], w_out[0], norm2_g[0])
    moe_w = (w_router[0], router_bias[0], w_expert_gu[0], w_expert_down[0], w_shared_gu[0], w_shared_down[0])
    out = _layer(x, ctx, c, c_ctx, lw, moe_w, final_norm_g.reshape(1, d))
    return out.reshape(b, l, d)
```
